```python
import math
import jax, jax.numpy as jnp
from jax import lax
import numpy as np

D_MODEL = 2048
BATCH = 1
SEQ = 8192
DEPTH = 1

MEM_LEN = 256
MEM_HEADS = 4
MEM_HEAD_DIM = D_MODEL // MEM_HEADS

BLOCK_Q = 128

SB_HEADS = 8
SB_HEAD_DIM = D_MODEL // 16
SB_WIDTH = SB_HEADS * SB_HEAD_DIM

DIFF_HEADS = 8
DIFF_V_DIM = D_MODEL // 16
DIFF_QK_DIM = DIFF_V_DIM // 2
DIFF_Q_WIDTH = DIFF_HEADS * 2 * DIFF_QK_DIM
DIFF_WIDTH = DIFF_HEADS * DIFF_V_DIM

MIX_WIDTH = SB_WIDTH + DIFF_WIDTH
IN_SPLITS = (SB_WIDTH, SB_WIDTH, SB_WIDTH, DIFF_Q_WIDTH, DIFF_Q_WIDTH, DIFF_WIDTH)
IN_WIDTH = sum(IN_SPLITS)

ROPE_DIM = DIFF_QK_DIM // 4
ROPE_THETA = 500000.0

PEER_HEADS = 8
N_KEYS = 128
N_EXPERTS = N_KEYS * N_KEYS
PEER_TOPK = 16
PEER_QUERY_DIM = 256
PEER_HALF = PEER_QUERY_DIM // 2
PEER_TOKEN_BLOCK = 128

LN_EPS = 1e-5
RMS_EPS = 1e-6
DEEPNORM_ALPHA = (2 * DEPTH) ** 0.25
DEEPNORM_BETA = (8 * DEPTH) ** -0.25

kernel_name = "hybrid_stickbreak_diffattn_peer_deepnorm"


def layer_norm(x, g, b):
    xf = x.astype(jnp.float32)
    mu = jnp.mean(xf, axis=-1, keepdims=True)
    var = jnp.mean(jnp.square(xf - mu), axis=-1, keepdims=True)
    return ((xf - mu) * lax.rsqrt(var + LN_EPS) * g.astype(jnp.float32)
            + b.astype(jnp.float32)).astype(x.dtype)


def rms_norm(x, g):
    xf = x.astype(jnp.float32)
    xf = xf * lax.rsqrt(jnp.mean(jnp.square(xf), axis=-1, keepdims=True) + RMS_EPS)
    return (xf * g.astype(jnp.float32)).astype(x.dtype)


def rope_partial(t, positions):
    half = ROPE_DIM // 2
    inv_freq = jnp.power(ROPE_THETA, -jnp.arange(half, dtype=jnp.float32) * 2.0 / ROPE_DIM)
    ang = positions.astype(jnp.float32)[:, None] * inv_freq[None, :]
    cos = jnp.cos(ang).astype(t.dtype)
    sin = jnp.sin(ang).astype(t.dtype)
    t1 = t[..., :half]
    t2 = t[..., half:ROPE_DIM]
    rest = t[..., ROPE_DIM:]
    return jnp.concatenate([t1 * cos - t2 * sin, t2 * cos + t1 * sin, rest], axis=-1)


def to_query_blocks(t):
    b, h, s, d = t.shape
    return t.reshape(b, h, s // BLOCK_Q, BLOCK_Q, d).transpose(2, 0, 1, 3, 4)


def from_query_blocks(o):
    nb, b, h, q, d = o.shape
    return o.transpose(1, 0, 3, 2, 4).reshape(b, nb * q, h, d)


def stick_breaking_attention(q, k, v):
    s_len, d = q.shape[2], q.shape[3]
    scale = 1.0 / math.sqrt(d)
    key_pos = jnp.arange(s_len)
    starts = jnp.arange(s_len // BLOCK_Q) * BLOCK_Q

    def block(args):
        qb, start = args
        z = jnp.einsum('bhqd,bhkd->bhqk', qb, k).astype(jnp.float32) * scale
        q_pos = start + jnp.arange(BLOCK_Q)
        before = key_pos[None, :] < q_pos[:, None]
        log_beta = jax.nn.log_sigmoid(z)
        log_keep = jnp.where(before, jax.nn.log_sigmoid(-z), 0.0)
        log_stick = lax.cumsum(log_keep, axis=3, reverse=True) - log_keep
        w = jnp.where(before, jnp.exp(log_beta + log_stick), 0.0)
        return jnp.einsum('bhqk,bhkd->bhqd', w.astype(v.dtype), v)

    return from_query_blocks(lax.map(block, (to_query_blocks(q), starts)))


def differential_attention(q1, k1, q2, k2, v, lam):
    s_len = q1.shape[2]
    scale = 1.0 / math.sqrt(DIFF_QK_DIM)
    key_pos = jnp.arange(s_len)
    starts = jnp.arange(s_len // BLOCK_Q) * BLOCK_Q

    def block(args):
        q1b, q2b, start = args
        q_pos = start + jnp.arange(BLOCK_Q)
        causal = key_pos[None, :] <= q_pos[:, None]
        s1 = jnp.einsum('bhqd,bhkd->bhqk', q1b, k1).astype(jnp.float32) * scale
        s2 = jnp.einsum('bhqd,bhkd->bhqk', q2b, k2).astype(jnp.float32) * scale
        p1 = jax.nn.softmax(jnp.where(causal, s1, -jnp.inf), axis=-1)
        p2 = jax.nn.softmax(jnp.where(causal, s2, -jnp.inf), axis=-1)
        w = p1 - lam * p2
        return jnp.einsum('bhqk,bhkd->bhqd', w.astype(v.dtype), v)

    out = lax.map(block, (to_query_blocks(q1), to_query_blocks(q2), starts))
    return from_query_blocks(out)


def hybrid_mixer(x, positions, w_in, sb_norm_gain, df_lambda, df_subln_gain, w_o, lambda_init):
    b, s, _ = x.shape
    proj = x @ w_in
    offsets = list(np.cumsum(IN_SPLITS)[:-1])
    sb_q, sb_k, sb_v, df_q, df_k, df_v = jnp.split(proj, offsets, axis=-1)

    def heads(t, h, d):
        return t.reshape(b, s, h, d).transpose(0, 2, 1, 3)

    sb_o = stick_breaking_attention(heads(sb_q, SB_HEADS, SB_HEAD_DIM),
                                    heads(sb_k, SB_HEADS, SB_HEAD_DIM),
                                    heads(sb_v, SB_HEADS, SB_HEAD_DIM))
    sb_o = rms_norm(sb_o, sb_norm_gain)

    dq = df_q.reshape(b, s, DIFF_HEADS, 2, DIFF_QK_DIM).transpose(3, 0, 2, 1, 4)
    dk = df_k.reshape(b, s, DIFF_HEADS, 2, DIFF_QK_DIM).transpose(3, 0, 2, 1, 4)
    q1, q2 = rope_partial(dq[0], positions), rope_partial(dq[1], positions)
    k1, k2 = rope_partial(dk[0], positions), rope_partial(dk[1], positions)
    lf = df_lambda.astype(jnp.float32)
    lam = (jnp.exp(jnp.sum(lf[0] * lf[1])) - jnp.exp(jnp.sum(lf[2] * lf[3]))
           + lambda_init)
    df_o = differential_attention(q1, k1, q2, k2, heads(df_v, DIFF_HEADS, DIFF_V_DIM), lam)
    df_o = rms_norm(df_o, df_subln_gain) * (1.0 - lambda_init)

    mixed = jnp.concatenate([sb_o.reshape(b, s, SB_WIDTH),
                             df_o.reshape(b, s, DIFF_WIDTH)], axis=-1)
    return mixed @ w_o


def memory_cross_attention(x, mem, w_mq, w_mkv, w_mo):
    b, s, _ = x.shape
    m = mem.shape[1]
    q = (x @ w_mq).reshape(b, s, MEM_HEADS, MEM_HEAD_DIM)
    kv = (mem @ w_mkv).reshape(b, m, 2, MEM_HEADS, MEM_HEAD_DIM)
    k, v = kv[:, :, 0], kv[:, :, 1]
    scores = jnp.einsum('bshd,bmhd->bhsm', q, k).astype(jnp.float32) / math.sqrt(MEM_HEAD_DIM)
    p = jax.nn.softmax(scores, axis=-1)
    o = jnp.einsum('bhsm,bmhd->bshd', p.astype(v.dtype), v).reshape(b, s, D_MODEL)
    return o @ w_mo


def peer_ffn(x, w_pq, sub_keys, expert_u, expert_v):
    b, s, d = x.shape
    q = (x @ w_pq).reshape(b, s, PEER_HEADS, 2, PEER_HALF)
    s1 = jnp.einsum('bshd,nd->bshn', q[..., 0, :], sub_keys[0]).astype(jnp.float32)
    s2 = jnp.einsum('bshd,nd->bshn', q[..., 1, :], sub_keys[1]).astype(jnp.float32)
    v1, i1 = lax.top_k(s1, PEER_TOPK)
    v2, i2 = lax.top_k(s2, PEER_TOPK)
    cand = (v1[..., :, None] + v2[..., None, :]).reshape(b, s, PEER_HEADS, PEER_TOPK * PEER_TOPK)
    cand_idx = (i1[..., :, None] * N_KEYS + i2[..., None, :]).reshape(
        b, s, PEER_HEADS, PEER_TOPK * PEER_TOPK)
    top, pos = lax.top_k(cand, PEER_TOPK)
    idx = jnp.take_along_axis(cand_idx, pos, axis=-1)
    g = jax.nn.softmax(top, axis=-1)

    n_blk = (b * s) // PEER_TOKEN_BLOCK
    xb = x.reshape(n_blk, PEER_TOKEN_BLOCK, d)
    ib = idx.reshape(n_blk, PEER_TOKEN_BLOCK, PEER_HEADS, PEER_TOPK)
    gb = g.reshape(n_blk, PEER_TOKEN_BLOCK, PEER_HEADS, PEER_TOPK)

    def block(args):
        xt, it, gt = args
        u = jnp.take(expert_u, it, axis=0)
        h = jnp.einsum('chkd,cd->chk', u, xt).astype(jnp.float32)
        a = gt * jax.nn.gelu(h, approximate=False)
        vv = jnp.take(expert_v, it, axis=0)
        return jnp.einsum('chk,chkd->cd', a.astype(x.dtype), vv)

    return lax.map(block, (xb, ib, gb)).reshape(b, s, d)


def setup_inputs(seed: int = 0) -> dict:
    key = jax.random.key(seed)
    ks = jax.random.split(key, 24)
    f32 = jnp.float32
    beta = DEEPNORM_BETA
    nrm = lambda k, shape: jax.random.normal(k, shape, dtype=f32)

    col_scale = jnp.concatenate([jnp.ones((2 * SB_WIDTH,), f32), beta * jnp.ones((SB_WIDTH,), f32),
                                 jnp.ones((2 * DIFF_Q_WIDTH,), f32), beta * jnp.ones((DIFF_WIDTH,), f32)])
    mkv_scale = jnp.concatenate([jnp.ones((D_MODEL,), f32), beta * jnp.ones((D_MODEL,), f32)])
    return {
        "x": nrm(ks[0], (BATCH, SEQ, D_MODEL)),
        "mem": nrm(ks[1], (BATCH, MEM_LEN, D_MODEL)),
        "w_in": nrm(ks[2], (DEPTH, D_MODEL, IN_WIDTH)) * (D_MODEL ** -0.5) * col_scale,
        "sb_norm_gain": 1.0 + 0.02 * nrm(ks[3], (DEPTH, SB_HEADS, SB_HEAD_DIM)),
        "df_lambda": 0.1 * nrm(ks[4], (DEPTH, 4, DIFF_QK_DIM)),
        "df_subln_gain": 1.0 + 0.02 * nrm(ks[5], (DEPTH, DIFF_V_DIM)),
        "w_o": nrm(ks[6], (DEPTH, MIX_WIDTH, D_MODEL)) * (MIX_WIDTH ** -0.5) * beta,
        "ln1_gain": 1.0 + 0.02 * nrm(ks[7], (DEPTH, D_MODEL)),
        "ln1_bias": 0.02 * nrm(ks[8], (DEPTH, D_MODEL)),
        "w_mq": nrm(ks[9], (DEPTH, D_MODEL, D_MODEL)) * (D_MODEL ** -0.5),
        "w_mkv": nrm(ks[10], (DEPTH, D_MODEL, 2 * D_MODEL)) * (D_MODEL ** -0.5) * mkv_scale,
        "w_mo": nrm(ks[11], (DEPTH, D_MODEL, D_MODEL)) * (D_MODEL ** -0.5) * beta,
        "ln2_gain": 1.0 + 0.02 * nrm(ks[12], (DEPTH, D_MODEL)),
        "ln2_bias": 0.02 * nrm(ks[13], (DEPTH, D_MODEL)),
        "w_pq": nrm(ks[14], (DEPTH, D_MODEL, PEER_HEADS * PEER_QUERY_DIM)) * (D_MODEL ** -0.5),
        "peer_sub_keys": nrm(ks[15], (DEPTH, 2, N_KEYS, PEER_HALF)) * (PEER_HALF ** -0.5),
        "peer_u": nrm(ks[16], (DEPTH, N_EXPERTS, D_MODEL)) * (D_MODEL ** -0.5),
        "peer_v": nrm(ks[17], (DEPTH, N_EXPERTS, D_MODEL)) * beta,
        "ln3_gain": 1.0 + 0.02 * nrm(ks[18], (DEPTH, D_MODEL)),
        "ln3_bias": 0.02 * nrm(ks[19], (DEPTH, D_MODEL)),
    }


def reference(x, mem, w_in, sb_norm_gain, df_lambda, df_subln_gain, w_o, ln1_gain, ln1_bias,
              w_mq, w_mkv, w_mo, ln2_gain, ln2_bias, w_pq, peer_sub_keys, peer_u, peer_v,
              ln3_gain, ln3_bias):
    positions = jnp.arange(x.shape[1])
    for l in range(DEPTH):
        lambda_init = 0.8 - 0.6 * math.exp(-0.3 * l)
        mix = hybrid_mixer(x, positions, w_in[l], sb_norm_gain[l], df_lambda[l],
                           df_subln_gain[l], w_o[l], lambda_init)
        x = layer_norm(DEEPNORM_ALPHA * x + mix, ln1_gain[l], ln1_bias[l])
        xa = memory_cross_attention(x, mem, w_mq[l], w_mkv[l], w_mo[l])
        x = layer_norm(DEEPNORM_ALPHA * x + xa, ln2_gain[l], ln2_bias[l])
        ff = peer_ffn(x, w_pq[l], peer_sub_keys[l], peer_u[l], peer_v[l])
        x = layer_norm(DEEPNORM_ALPHA * x + ff, ln3_gain[l], ln3_bias[l])
    return x
```

```python
import functools
import math

import jax
import jax.numpy as jnp
import numpy as np
from jax import lax
from jax.experimental import pallas as pl
from jax.experimental.pallas import tpu as pltpu

F32 = jnp.float32
BF16 = jnp.bfloat16

LANES = 128
VMEM_LIMIT_BYTES = 56 * 1024 * 1024

DEPTH = 1
SB_HEADS = 8
DIFF_HEADS = 8
HEAD_DIM = 128
DIFF_QK_DIM = 64
ROPE_DIM = 16
ROPE_THETA = 500000.0
MEM_HEADS = 4
PEER_HEADS = 8
N_KEYS = 128
PEER_TOPK = 16
LN_EPS = 1e-5
RMS_EPS = 1e-6
DEEPNORM_ALPHA = (2 * DEPTH) ** 0.25
NEG_INF = float("-inf")


def _dot(a, b):
    return jnp.dot(a, b, preferred_element_type=F32)


def _dot_nt(a, b):
    return lax.dot_general(a, b, (((1,), (1,)), ((), ())), preferred_element_type=F32)


def _params(*semantics):
    return pltpu.CompilerParams(dimension_semantics=semantics, vmem_limit_bytes=VMEM_LIMIT_BYTES)


def _matmul_kernel(a_ref, b_ref, o_ref):
    o_ref[...] = _dot(a_ref[...], b_ref[...]).astype(o_ref.dtype)


def _matmul(a, b, tm, tn, out_dtype=BF16):
    m, k = a.shape
    n = b.shape[1]
    return pl.pallas_call(
        _matmul_kernel,
        grid=(m // tm, n // tn),
        in_specs=[pl.BlockSpec((tm, k), lambda i, j: (i, 0)),
                  pl.BlockSpec((k, tn), lambda i, j: (0, j))],
        out_specs=pl.BlockSpec((tm, tn), lambda i, j: (i, j)),
        out_shape=jax.ShapeDtypeStruct((m, n), out_dtype),
        compiler_params=_params("parallel", "parallel"),
        name="matmul",
    )(a, b)


def _inproj_kernel(a_ref, b_ref, c_ref, s1_ref, s2_ref, o_ref, *, rope_lo, rope_hi):
    j = pl.program_id(1)
    acc = _dot(a_ref[...], b_ref[...])
    o_ref[...] = acc.astype(o_ref.dtype)

    @pl.when(jnp.logical_and(j >= rope_lo, j < rope_hi))
    def _():
        cos, sin_up, sin_dn = c_ref[...], s1_ref[...], s2_ref[...]
        for cc in range(acc.shape[1] // LANES):
            t = acc[:, cc * LANES:(cc + 1) * LANES]
            up = pltpu.roll(t, LANES - ROPE_DIM // 2, axis=1)
            dn = pltpu.roll(t, ROPE_DIM // 2, axis=1)
            o_ref[:, cc * LANES:(cc + 1) * LANES] = (t * cos + up * sin_up + dn * sin_dn).astype(o_ref.dtype)


def _rope_tables(seq):
    half = ROPE_DIM // 2
    inv_freq = jnp.power(ROPE_THETA, -jnp.arange(half, dtype=F32) * 2.0 / ROPE_DIM)
    ang = jnp.arange(seq).astype(F32)[:, None] * inv_freq[None, :]
    cos, sin = jnp.cos(ang), jnp.sin(ang)
    ones = jnp.ones((seq, DIFF_QK_DIM - ROPE_DIM), F32)
    zeros = jnp.zeros((seq, DIFF_QK_DIM - ROPE_DIM), F32)
    zh = jnp.zeros((seq, half), F32)
    c64 = jnp.concatenate([cos, cos, ones], axis=1)
    up64 = jnp.concatenate([-sin, zh, zeros], axis=1)
    dn64 = jnp.concatenate([zh, sin, zeros], axis=1)
    rep = LANES // DIFF_QK_DIM
    return jnp.tile(c64, (1, rep)), jnp.tile(up64, (1, rep)), jnp.tile(dn64, (1, rep))


def _inproj(xb, w_in, tm, tn, rope_cols):
    seq, k = xb.shape
    n = w_in.shape[1]
    cos, sin_up, sin_dn = _rope_tables(seq)
    tab = pl.BlockSpec((tm, LANES), lambda i, j: (i, 0))
    kern = functools.partial(_inproj_kernel, rope_lo=rope_cols[0] // tn, rope_hi=rope_cols[1] // tn)
    return pl.pallas_call(
        kern,
        grid=(seq // tm, n // tn),
        in_specs=[pl.BlockSpec((tm, k), lambda i, j: (i, 0)),
                  pl.BlockSpec((k, tn), lambda i, j: (0, j)),
                  tab, tab, tab],
        out_specs=pl.BlockSpec((tm, tn), lambda i, j: (i, j)),
        out_shape=jax.ShapeDtypeStruct((seq, n), BF16),
        compiler_params=_params("parallel", "parallel"),
        name="inproj_rope",
    )(xb, w_in, cos, sin_up, sin_dn)


def _sb_kernel(q_ref, k_ref, v_ref, g_ref, o_ref, *, tq, scale):
    h = pl.program_id(0)
    i = pl.program_id(1)
    q = q_ref[...]
    row = lax.broadcasted_iota(jnp.int32, (tq, tq), 0)
    col = lax.broadcasted_iota(jnp.int32, (tq, tq), 1)
    later = (row > col).astype(BF16)
    before = col < row

    def block(kb, acc, c, masked):
        start = pl.multiple_of(kb * tq, tq)
        k = k_ref[pl.ds(start, tq), :]
        v = v_ref[pl.ds(start, tq), :]
        z = _dot_nt(q, k) * scale
        sp = jnp.log(1.0 + jnp.exp(-jnp.abs(z)))
        log_beta = jnp.minimum(z, 0.0) - sp
        log_keep = -jnp.maximum(z, 0.0) - sp
        if masked:
            log_keep = jnp.where(before, log_keep, 0.0)
        hi = log_keep.astype(BF16)
        lo = (log_keep - hi.astype(F32)).astype(BF16)
        stick = _dot(hi, later) + _dot(lo, later)
        w = jnp.exp(log_beta + stick + c)
        if masked:
            w = jnp.where(before, w, 0.0)
        acc = acc + _dot(w.astype(BF16), v)
        c = c + jnp.sum(log_keep, axis=1, keepdims=True)
        return acc, c

    acc0 = jnp.zeros((tq, HEAD_DIM), F32)
    c0 = jnp.zeros((tq, 1), F32)
    acc, c = block(i, acc0, c0, True)
    acc, c = lax.fori_loop(0, i, lambda jj, carry: block(i - 1 - jj, carry[0], carry[1], False), (acc, c))
    g = g_ref[pl.ds(h, 1), :]
    ms = jnp.mean(acc * acc, axis=1, keepdims=True)
    o_ref[...] = (acc * lax.rsqrt(ms + RMS_EPS) * g).astype(o_ref.dtype)


def _sb_attention(proj, gain, tq):
    seq = proj.shape[0]
    kern = functools.partial(_sb_kernel, tq=tq, scale=1.0 / math.sqrt(HEAD_DIM))
    return pl.pallas_call(
        kern,
        grid=(SB_HEADS, seq // tq),
        in_specs=[pl.BlockSpec((tq, HEAD_DIM), lambda h, i: (i, h)),
                  pl.BlockSpec((seq, HEAD_DIM), lambda h, i: (0, SB_HEADS + h)),
                  pl.BlockSpec((seq, HEAD_DIM), lambda h, i: (0, 2 * SB_HEADS + h)),
                  pl.BlockSpec((SB_HEADS, HEAD_DIM), lambda h, i: (0, 0))],
        out_specs=pl.BlockSpec((tq, HEAD_DIM), lambda h, i: (i, h)),
        out_shape=jax.ShapeDtypeStruct((seq, SB_HEADS * HEAD_DIM), BF16),
        compiler_params=_params("parallel", "arbitrary"),
        name="stickbreak_attn",
    )(proj, proj, proj, gain)


def _diff_kernel(q_ref, k_ref, v_ref, lam_ref, g_ref, o_ref, *, tq, lambda_init):
    i = pl.program_id(1)
    q = q_ref[...]
    lane = lax.broadcasted_iota(jnp.int32, (tq, HEAD_DIM), 1)
    zero = jnp.zeros_like(q)
    qs = jnp.concatenate([jnp.where(lane < DIFF_QK_DIM, q, zero),
                          jnp.where(lane >= DIFF_QK_DIM, q, zero)], axis=0)
    row = lax.broadcasted_iota(jnp.int32, (2 * tq, tq), 0)
    col = lax.broadcasted_iota(jnp.int32, (2 * tq, tq), 1)
    causal = col <= jnp.where(row >= tq, row - tq, row)
    scale = 1.0 / math.sqrt(DIFF_QK_DIM)

    def block(kb, m, l, acc, masked):
        start = pl.multiple_of(kb * tq, tq)
        k = k_ref[pl.ds(start, tq), :]
        v = v_ref[pl.ds(start, tq), :]
        s = _dot_nt(qs, k) * scale
        if masked:
            s = jnp.where(causal, s, NEG_INF)
        m_new = jnp.maximum(m, jnp.max(s, axis=1, keepdims=True))
        alpha = jnp.exp(m - m_new)
        p = jnp.exp(s - m_new)
        l = alpha * l + jnp.sum(p, axis=1, keepdims=True)
        acc = alpha * acc + _dot(p.astype(BF16), v)
        return m_new, l, acc

    m0 = jnp.full((2 * tq, 1), NEG_INF, F32)
    l0 = jnp.zeros((2 * tq, 1), F32)
    acc0 = jnp.zeros((2 * tq, HEAD_DIM), F32)
    m, l, acc = block(i, m0, l0, acc0, True)
    m, l, acc = lax.fori_loop(0, i, lambda kb, carry: block(kb, *carry, False), (m, l, acc))

    lf = lam_ref[...]
    lam = (jnp.exp(jnp.sum(lf[0:1, :] * lf[1:2, :], axis=1, keepdims=True))
           - jnp.exp(jnp.sum(lf[2:3, :] * lf[3:4, :], axis=1, keepdims=True)) + lambda_init)
    o = acc / l
    d = o[:tq, :] - lam * o[tq:, :]
    ms = jnp.mean(d * d, axis=1, keepdims=True)
    o_ref[...] = (d * lax.rsqrt(ms + RMS_EPS) * g_ref[...] * (1.0 - lambda_init)).astype(o_ref.dtype)


def _diff_attention(proj, df_lambda, gain, tq, lambda_init):
    seq = proj.shape[0]
    qcol = 3 * SB_HEADS
    kcol = qcol + DIFF_HEADS
    vcol = kcol + DIFF_HEADS
    kern = functools.partial(_diff_kernel, tq=tq, lambda_init=lambda_init)
    return pl.pallas_call(
        kern,
        grid=(DIFF_HEADS, seq // tq),
        in_specs=[pl.BlockSpec((tq, HEAD_DIM), lambda h, i: (i, qcol + h)),
                  pl.BlockSpec((seq, HEAD_DIM), lambda h, i: (0, kcol + h)),
                  pl.BlockSpec((seq, HEAD_DIM), lambda h, i: (0, vcol + h)),
                  pl.BlockSpec((4, DIFF_QK_DIM), lambda h, i: (0, 0)),
                  pl.BlockSpec((1, HEAD_DIM), lambda h, i: (0, 0))],
        out_specs=pl.BlockSpec((tq, HEAD_DIM), lambda h, i: (i, h)),
        out_shape=jax.ShapeDtypeStruct((seq, DIFF_HEADS * HEAD_DIM), BF16),
        compiler_params=_params("parallel", "arbitrary"),
        name="diff_attn",
    )(proj, proj, proj, df_lambda, gain)


def _layer_norm(r, g, b):
    mu = jnp.mean(r, axis=1, keepdims=True)
    d = r - mu
    var = jnp.mean(d * d, axis=1, keepdims=True)
    return d * lax.rsqrt(var + LN_EPS) * g + b


def _proj_ln_kernel(*refs, n_parts):
    a_refs = refs[:n_parts]
    w_ref, x_ref, g_ref, b_ref, o_ref, ob_ref = refs[n_parts:]
    y = None
    off = 0
    for a_ref in a_refs:
        kk = a_ref.shape[1]
        part = _dot(a_ref[...], w_ref[off:off + kk, :])
        y = part if y is None else y + part
        off += kk
    out = _layer_norm(DEEPNORM_ALPHA * x_ref[...] + y, g_ref[...], b_ref[...])
    o_ref[...] = out
    ob_ref[...] = out.astype(BF16)


def _proj_ln(parts, w, x, gain, bias, tm):
    seq, d = x.shape
    kern = functools.partial(_proj_ln_kernel, n_parts=len(parts))
    row = lambda i: (i, 0)
    fixed = lambda i: (0, 0)
    return pl.pallas_call(
        kern,
        grid=(seq // tm,),
        in_specs=[pl.BlockSpec((tm, p.shape[1]), row) for p in parts] + [
            pl.BlockSpec(w.shape, fixed),
            pl.BlockSpec((tm, d), row),
            pl.BlockSpec((1, d), fixed),
            pl.BlockSpec((1, d), fixed)],
        out_specs=[pl.BlockSpec((tm, d), row), pl.BlockSpec((tm, d), row)],
        out_shape=[jax.ShapeDtypeStruct((seq, d), F32), jax.ShapeDtypeStruct((seq, d), BF16)],
        compiler_params=_params("parallel"),
        name="proj_residual_ln",
    )(*parts, w, x, gain, bias)


def _xattn_kernel(q_ref, kv_ref, o_ref, *, d_model, head_dim):
    scale = 1.0 / math.sqrt(head_dim)
    for hh in range(d_model // head_dim):
        lo = hh * head_dim
        q = q_ref[:, lo:lo + head_dim]
        k = kv_ref[:, lo:lo + head_dim]
        v = kv_ref[:, d_model + lo:d_model + lo + head_dim]
        s = _dot_nt(q, k) * scale
        m = jnp.max(s, axis=1, keepdims=True)
        p = jnp.exp(s - m)
        p = p / jnp.sum(p, axis=1, keepdims=True)
        o_ref[:, lo:lo + head_dim] = _dot(p.astype(BF16), v).astype(o_ref.dtype)


def _xattn(q, kv, tm):
    seq, d = q.shape
    kern = functools.partial(_xattn_kernel, d_model=d, head_dim=d // MEM_HEADS)
    return pl.pallas_call(
        kern,
        grid=(seq // tm,),
        in_specs=[pl.BlockSpec((tm, d), lambda i: (i, 0)),
                  pl.BlockSpec(kv.shape, lambda i: (0, 0))],
        out_specs=pl.BlockSpec((tm, d), lambda i: (i, 0)),
        out_shape=jax.ShapeDtypeStruct((seq, d), BF16),
        compiler_params=_params("parallel"),
        name="memory_xattn",
    )(q, kv)


def _top16(s, n_rows, vals_ref):
    iota = lax.broadcasted_iota(jnp.int32, s.shape, 0).astype(F32)

    def step(k, carry):
        work, rank = carry
        m = jnp.max(work, axis=0, keepdims=True)
        first = jnp.min(jnp.where(work == m, iota, float(n_rows)), axis=0, keepdims=True)
        sel = iota == first
        vals_ref[pl.ds(k, 1), :] = m
        return jnp.where(sel, NEG_INF, work), jnp.where(sel, k.astype(F32), rank)

    _, rank = lax.fori_loop(0, PEER_TOPK, step, (s, jnp.full(s.shape, float(PEER_TOPK), F32)))
    return rank


def _route_kernel(pq_ref, keys_ref, e2_ref, r2_ref, w1_ref, n1_ref, v1_ref, v2_ref, vc_ref):
    q = pq_ref[...]
    s1 = _dot_nt(keys_ref[0], q[:, :N_KEYS])
    s2 = _dot_nt(keys_ref[1], q[:, N_KEYS:])
    rank1 = _top16(s1, N_KEYS, v1_ref)
    rank2 = _top16(s2, N_KEYS, v2_ref)
    v2 = v2_ref[...]
    cand = jnp.concatenate([v1_ref[k1:k1 + 1, :] + v2 for k1 in range(PEER_TOPK)], axis=0)
    crank = _top16(cand, PEER_TOPK * PEER_TOPK, vc_ref)
    top = vc_ref[...]
    z = jnp.sum(jnp.exp(top - top[0:1, :]), axis=0, keepdims=True)
    chosen = jnp.where(crank < float(PEER_TOPK), 1.0, 0.0)
    n1 = jnp.zeros(s1.shape, F32)
    for k1 in range(PEER_TOPK):
        cnt = jnp.sum(chosen[k1 * PEER_TOPK:(k1 + 1) * PEER_TOPK, :], axis=0, keepdims=True)
        n1 = jnp.where(rank1 == float(k1), cnt, n1)
    e2_ref[0] = jnp.exp(s2 - v2[0:1, :])
    r2_ref[0] = rank2
    w1_ref[0] = jnp.exp(s1 - v1_ref[0:1, :]) / z
    n1_ref[0] = n1


def _peer_route(pq, keys, tt):
    seq = pq.shape[0]
    qd = 2 * N_KEYS
    tab = pl.BlockSpec((1, N_KEYS, tt), lambda i, h: (h, 0, i))
    shape = jax.ShapeDtypeStruct((PEER_HEADS, N_KEYS, seq), F32)
    return pl.pallas_call(
        _route_kernel,
        grid=(seq // tt, PEER_HEADS),
        in_specs=[pl.BlockSpec((tt, qd), lambda i, h: (i, h)),
                  pl.BlockSpec(keys.shape, lambda i, h: (0, 0, 0))],
        out_specs=[tab, tab, tab, tab],
        out_shape=[shape, shape, shape, shape],
        scratch_shapes=[pltpu.VMEM((PEER_TOPK, tt), F32),
                        pltpu.VMEM((PEER_TOPK, tt), F32),
                        pltpu.VMEM((PEER_TOPK, tt), F32)],
        compiler_params=_params("parallel", "parallel"),
        name="peer_route",
    )(pq, keys)


def _peer_kernel(xb_ref, u_ref, v_ref, e2_ref, r2_ref, w1_ref, n1_ref, x_ref, g_ref, b_ref,
                 o_ref, acc_ref, *, keys_per_step):
    cc = pl.program_id(1)

    @pl.when(cc == 0)
    def _():
        acc_ref[...] = jnp.zeros_like(acc_ref)

    hidden = _dot_nt(xb_ref[...], u_ref[...])
    act = 0.5 * hidden * (1.0 + lax.erf(hidden * math.sqrt(0.5)))
    gates = []
    for part in range(keys_per_step):
        c = cc * keys_per_step + part
        gt = None
        for h in range(PEER_HEADS):
            n1 = n1_ref[h, pl.ds(c, 1), :]
            w1 = w1_ref[h, pl.ds(c, 1), :]
            term = jnp.where(r2_ref[h] < n1, e2_ref[h] * w1, 0.0)
            gt = term if gt is None else gt + term
        gates.append(gt.T)
    gate = gates[0] if keys_per_step == 1 else jnp.concatenate(gates, axis=1)
    acc_ref[...] += _dot((act * gate).astype(BF16), v_ref[...])

    @pl.when(cc == pl.num_programs(1) - 1)
    def _():
        o_ref[...] = _layer_norm(DEEPNORM_ALPHA * x_ref[...] + acc_ref[...], g_ref[...], b_ref[...])


def _peer_dense(xb, x, u, v, tables, gain, bias, tt, keys_per_step):
    seq, d = x.shape
    te = keys_per_step * N_KEYS
    tab = pl.BlockSpec((PEER_HEADS, N_KEYS, tt), lambda i, c: (0, 0, i))
    row = lambda i, c: (i, 0)
    fixed = lambda i, c: (0, 0)
    kern = functools.partial(_peer_kernel, keys_per_step=keys_per_step)
    return pl.pallas_call(
        kern,
        grid=(seq // tt, u.shape[0] // te),
        in_specs=[pl.BlockSpec((tt, d), row),
                  pl.BlockSpec((te, d), lambda i, c: (c, 0)),
                  pl.BlockSpec((te, d), lambda i, c: (c, 0)),
                  tab, tab, tab, tab,
                  pl.BlockSpec((tt, d), row),
                  pl.BlockSpec((1, d), fixed),
                  pl.BlockSpec((1, d), fixed)],
        out_specs=pl.BlockSpec((tt, d), row),
        out_shape=jax.ShapeDtypeStruct((seq, d), F32),
        scratch_shapes=[pltpu.VMEM((tt, d), F32)],
        compiler_params=_params("parallel", "arbitrary"),
        name="peer_dense",
    )(xb, u, v, *tables, x, gain, bias)


def _tile(n, want):
    return min(n, want)


def kernel(x, mem, w_in, sb_norm_gain, df_lambda, df_subln_gain, w_o, ln1_gain, ln1_bias, w_mq, w_mkv, w_mo, ln2_gain, ln2_bias, w_pq, peer_sub_keys, peer_u, peer_v, ln3_gain, ln3_bias):
    b, seq, d = x.shape
    assert b == 1 and w_in.shape[0] == DEPTH
    x2d = x.reshape(seq, d)
    mem2d = mem.reshape(mem.shape[1], d).astype(BF16)
    for l in range(DEPTH):
        lambda_init = 0.8 - 0.6 * math.exp(-0.3 * l)
        row1 = lambda a: a.reshape(1, -1)
        proj = _inproj(x2d.astype(BF16), w_in[l].astype(BF16), _tile(seq, 512), 1024,
                       rope_cols=(3 * SB_HEADS * HEAD_DIM, 5 * SB_HEADS * HEAD_DIM))
        sb_o = _sb_attention(proj, sb_norm_gain[l], _tile(seq, 256))
        df_o = _diff_attention(proj, df_lambda[l], row1(df_subln_gain[l]), _tile(seq, 256), lambda_init)
        x1, x1b = _proj_ln([sb_o, df_o], w_o[l].astype(BF16), x2d, row1(ln1_gain[l]), row1(ln1_bias[l]),
                           _tile(seq, 256))
        q = _matmul(x1b, w_mq[l].astype(BF16), _tile(seq, 512), 1024)
        kv = _matmul(mem2d, w_mkv[l].astype(BF16), mem2d.shape[0], 1024)
        xa = _xattn(q, kv, _tile(seq, 256))
        x2, x2b = _proj_ln([xa], w_mo[l].astype(BF16), x1, row1(ln2_gain[l]), row1(ln2_bias[l]),
                           _tile(seq, 256))
        pq = _matmul(x2b, w_pq[l].astype(BF16), _tile(seq, 512), 1024)
        tables = _peer_route(pq, peer_sub_keys[l].astype(BF16), _tile(seq, 256))
        x2d = _peer_dense(x2b, x2, peer_u[l].astype(BF16), peer_v[l].astype(BF16), tables,
                          row1(ln3_gain[l]), row1(ln3_bias[l]), _tile(seq, 256), 2)
    return x2d.reshape(b, seq, d)
```

```python
import functools
import math

import jax
import jax.numpy as jnp
import numpy as np
from jax import lax
from jax.experimental import pallas as pl
from jax.experimental.pallas import tpu as pltpu

F32 = jnp.float32
BF16 = jnp.bfloat16

LANES = 128
VMEM_LIMIT_BYTES = 56 * 1024 * 1024

DEPTH = 1
SB_HEADS = 8
DIFF_HEADS = 8
HEAD_DIM = 128
DIFF_QK_DIM = 64
ROPE_DIM = 16
ROPE_THETA = 500000.0
MEM_HEADS = 4
PEER_HEADS = 8
N_KEYS = 128
PEER_TOPK = 16
LN_EPS = 1e-5
RMS_EPS = 1e-6
DEEPNORM_ALPHA = (2 * DEPTH) ** 0.25
NEG_INF = float("-inf")
SB_DEAD_LOG = -105.0
SUM_ROWS = 16


def _dot(a, b):
    return jnp.dot(a, b, preferred_element_type=F32)


def _dot_nt(a, b):
    return lax.dot_general(a, b, (((1,), (1,)), ((), ())), preferred_element_type=F32)


def _params(*semantics):
    return pltpu.CompilerParams(dimension_semantics=semantics, vmem_limit_bytes=VMEM_LIMIT_BYTES)


def _matmul_kernel(a_ref, b_ref, o_ref):
    o_ref[...] = _dot(a_ref[...], b_ref[...]).astype(o_ref.dtype)


def _matmul(a, b, tm, tn, out_dtype=BF16):
    m, k = a.shape
    n = b.shape[1]
    return pl.pallas_call(
        _matmul_kernel,
        grid=(m // tm, n // tn),
        in_specs=[pl.BlockSpec((tm, k), lambda i, j: (i, 0)),
                  pl.BlockSpec((k, tn), lambda i, j: (0, j))],
        out_specs=pl.BlockSpec((tm, tn), lambda i, j: (i, j)),
        out_shape=jax.ShapeDtypeStruct((m, n), out_dtype),
        compiler_params=_params("parallel", "parallel"),
        name="matmul",
    )(a, b)


def _inproj_kernel(a_ref, b_ref, c_ref, s1_ref, s2_ref, o_ref, *, rope_lo, rope_hi):
    j = pl.program_id(1)
    acc = _dot(a_ref[...], b_ref[...])
    o_ref[...] = acc.astype(o_ref.dtype)

    @pl.when(jnp.logical_and(j >= rope_lo, j < rope_hi))
    def _():
        cos, sin_up, sin_dn = c_ref[...], s1_ref[...], s2_ref[...]
        for cc in range(acc.shape[1] // LANES):
            t = acc[:, cc * LANES:(cc + 1) * LANES]
            up = pltpu.roll(t, LANES - ROPE_DIM // 2, axis=1)
            dn = pltpu.roll(t, ROPE_DIM // 2, axis=1)
            o_ref[:, cc * LANES:(cc + 1) * LANES] = (t * cos + up * sin_up + dn * sin_dn).astype(o_ref.dtype)


def _rope_tables(seq):
    half = ROPE_DIM // 2
    inv_freq = jnp.power(ROPE_THETA, -jnp.arange(half, dtype=F32) * 2.0 / ROPE_DIM)
    ang = jnp.arange(seq).astype(F32)[:, None] * inv_freq[None, :]
    cos, sin = jnp.cos(ang), jnp.sin(ang)
    ones = jnp.ones((seq, DIFF_QK_DIM - ROPE_DIM), F32)
    zeros = jnp.zeros((seq, DIFF_QK_DIM - ROPE_DIM), F32)
    zh = jnp.zeros((seq, half), F32)
    c64 = jnp.concatenate([cos, cos, ones], axis=1)
    up64 = jnp.concatenate([-sin, zh, zeros], axis=1)
    dn64 = jnp.concatenate([zh, sin, zeros], axis=1)
    rep = LANES // DIFF_QK_DIM
    return jnp.tile(c64, (1, rep)), jnp.tile(up64, (1, rep)), jnp.tile(dn64, (1, rep))


def _inproj(xb, w_in, tm, tn, rope_cols):
    seq, k = xb.shape
    n = w_in.shape[1]
    cos, sin_up, sin_dn = _rope_tables(seq)
    tab = pl.BlockSpec((tm, LANES), lambda i, j: (i, 0))
    kern = functools.partial(_inproj_kernel, rope_lo=rope_cols[0] // tn, rope_hi=rope_cols[1] // tn)
    return pl.pallas_call(
        kern,
        grid=(seq // tm, n // tn),
        in_specs=[pl.BlockSpec((tm, k), lambda i, j: (i, 0)),
                  pl.BlockSpec((k, tn), lambda i, j: (0, j)),
                  tab, tab, tab],
        out_specs=pl.BlockSpec((tm, tn), lambda i, j: (i, j)),
        out_shape=jax.ShapeDtypeStruct((seq, n), BF16),
        compiler_params=_params("parallel", "parallel"),
        name="inproj_rope",
    )(xb, w_in, cos, sin_up, sin_dn)


def _sb_kernel(q_ref, k_ref, v_ref, g_ref, o_ref, *, tq, scale):
    h = pl.program_id(0)
    i = pl.program_id(1)
    q = q_ref[...]
    row = lax.broadcasted_iota(jnp.int32, (tq, tq), 0)
    col = lax.broadcasted_iota(jnp.int32, (tq, tq), 1)
    later = (row > col).astype(BF16)
    before = col < row

    def block(kb, acc, c, masked):
        start = pl.multiple_of(kb * tq, tq)
        k = k_ref[pl.ds(start, tq), :]
        v = v_ref[pl.ds(start, tq), :]
        z = _dot_nt(q, k) * scale
        sp = jnp.log(1.0 + jnp.exp(-jnp.abs(z)))
        log_beta = jnp.minimum(z, 0.0) - sp
        log_keep = -jnp.maximum(z, 0.0) - sp
        if masked:
            log_keep = jnp.where(before, log_keep, 0.0)
        hi = log_keep.astype(BF16)
        lo = (log_keep - hi.astype(F32)).astype(BF16)
        stick = _dot(hi, later) + _dot(lo, later)
        w = jnp.exp(log_beta + stick + c)
        if masked:
            w = jnp.where(before, w, 0.0)
        acc = acc + _dot(w.astype(BF16), v)
        c = c + jnp.sum(log_keep, axis=1, keepdims=True)
        return acc, c

    acc0 = jnp.zeros((tq, HEAD_DIM), F32)
    c0 = jnp.zeros((tq, 1), F32)
    acc, c = block(i, acc0, c0, True)

    def live(carry):
        jj, _, _, cmax = carry
        return jnp.logical_and(jj < i, cmax > SB_DEAD_LOG)

    def older(carry):
        jj, acc, c, _ = carry
        acc, c = block(i - 1 - jj, acc, c, False)
        return jj + 1, acc, c, jnp.max(c)

    _, acc, c, _ = lax.while_loop(live, older, (jnp.int32(0), acc, c, jnp.max(c)))
    g = g_ref[pl.ds(h, 1), :]
    ms = jnp.mean(acc * acc, axis=1, keepdims=True)
    o_ref[...] = (acc * lax.rsqrt(ms + RMS_EPS) * g).astype(o_ref.dtype)


def _sb_attention(proj, gain, tq):
    seq = proj.shape[0]
    kern = functools.partial(_sb_kernel, tq=tq, scale=1.0 / math.sqrt(HEAD_DIM))
    return pl.pallas_call(
        kern,
        grid=(SB_HEADS, seq // tq),
        in_specs=[pl.BlockSpec((tq, HEAD_DIM), lambda h, i: (i, h)),
                  pl.BlockSpec((seq, HEAD_DIM), lambda h, i: (0, SB_HEADS + h)),
                  pl.BlockSpec((seq, HEAD_DIM), lambda h, i: (0, 2 * SB_HEADS + h)),
                  pl.BlockSpec((SB_HEADS, HEAD_DIM), lambda h, i: (0, 0))],
        out_specs=pl.BlockSpec((tq, HEAD_DIM), lambda h, i: (i, h)),
        out_shape=jax.ShapeDtypeStruct((seq, SB_HEADS * HEAD_DIM), BF16),
        compiler_params=_params("parallel", "arbitrary"),
        name="stickbreak_attn",
    )(proj, proj, proj, gain)


def _vt_proj_kernel(w_ref, x_ref, o_ref):
    o_ref[0] = _dot_nt(w_ref[...], x_ref[...]).astype(o_ref.dtype)


def _vt_proj(xb, w_vt, tk):
    seq, k = xb.shape
    n = w_vt.shape[0]
    return pl.pallas_call(
        _vt_proj_kernel,
        grid=(seq // tk,),
        in_specs=[pl.BlockSpec((n, k), lambda j: (0, 0)),
                  pl.BlockSpec((tk, k), lambda j: (j, 0))],
        out_specs=pl.BlockSpec((1, n, tk), lambda j: (j, 0, 0)),
        out_shape=jax.ShapeDtypeStruct((seq // tk, n, tk), BF16),
        compiler_params=_params("parallel"),
        name="vt_proj",
    )(w_vt, xb)


def _diff_kernel(q_ref, k_ref, vt_ref, lam_ref, g_ref, o_ref, *, tq, lambda_init):
    i = pl.program_id(1)
    q = q_ref[...] * jnp.asarray(1.0 / math.sqrt(DIFF_QK_DIM), BF16)
    lane = lax.broadcasted_iota(jnp.int32, (tq, HEAD_DIM), 1)
    zero = jnp.zeros_like(q)
    q1 = jnp.where(lane < DIFF_QK_DIM, q, zero)
    q2 = jnp.where(lane >= DIFF_QK_DIM, q, zero)
    key = lax.broadcasted_iota(jnp.int32, (tq, tq), 0)
    qry = lax.broadcasted_iota(jnp.int32, (tq, tq), 1)
    causal = key <= qry
    ones = jnp.ones((SUM_ROWS, tq), BF16)

    def scores(kb):
        k = k_ref[pl.ds(pl.multiple_of(kb * tq, tq), tq), :]
        return _dot_nt(k, q1), _dot_nt(k, q2)

    def update(kb, s, m, acc):
        m_new = jnp.maximum(m, jnp.max(s, axis=0, keepdims=True))
        alpha = jnp.exp(m - m_new)
        p = jnp.exp(s - m_new).astype(BF16)
        v_aug = jnp.concatenate([vt_ref[kb], ones], axis=0)
        return m_new, alpha * acc + _dot(v_aug, p)

    m0 = jnp.full((1, tq), NEG_INF, F32)
    acc0 = jnp.zeros((HEAD_DIM + SUM_ROWS, tq), F32)
    d1, d2 = scores(i)
    n1, n2 = scores(0)
    m1, a1 = update(i, jnp.where(causal, d1, NEG_INF), m0, acc0)
    m2, a2 = update(i, jnp.where(causal, d2, NEG_INF), m0, acc0)

    def body(t, carry):
        s1, s2, m1, a1, m2, a2 = carry
        nxt = jnp.minimum(t + 1, i - 1)
        n1, n2 = scores(nxt)
        m1, a1 = update(t, s1, m1, a1)
        m2, a2 = update(t, s2, m2, a2)
        return n1, n2, m1, a1, m2, a2

    _, _, m1, a1, m2, a2 = lax.fori_loop(0, i, body, (n1, n2, m1, a1, m2, a2))

    lf = lam_ref[...]
    lam = (jnp.exp(jnp.sum(lf[0:1, :] * lf[1:2, :], axis=1, keepdims=True))
           - jnp.exp(jnp.sum(lf[2:3, :] * lf[3:4, :], axis=1, keepdims=True)) + lambda_init)
    o1 = a1[:HEAD_DIM, :] / a1[HEAD_DIM:HEAD_DIM + 1, :]
    o2 = a2[:HEAD_DIM, :] / a2[HEAD_DIM:HEAD_DIM + 1, :]
    d = o1 - lam * o2
    ms = jnp.mean(d * d, axis=0, keepdims=True)
    dn = (d * lax.rsqrt(ms + RMS_EPS)).T
    o_ref[...] = (dn * g_ref[...] * (1.0 - lambda_init)).astype(o_ref.dtype)


def _diff_attention(proj, vt, df_lambda, gain, tq, lambda_init):
    seq = proj.shape[0]
    qcol = 3 * SB_HEADS
    kcol = qcol + DIFF_HEADS
    kern = functools.partial(_diff_kernel, tq=tq, lambda_init=lambda_init)
    return pl.pallas_call(
        kern,
        grid=(DIFF_HEADS, seq // tq),
        in_specs=[pl.BlockSpec((tq, HEAD_DIM), lambda h, i: (i, qcol + h)),
                  pl.BlockSpec((seq, HEAD_DIM), lambda h, i: (0, kcol + h)),
                  pl.BlockSpec((seq // tq, HEAD_DIM, tq), lambda h, i: (0, h, 0)),
                  pl.BlockSpec((4, DIFF_QK_DIM), lambda h, i: (0, 0)),
                  pl.BlockSpec((1, HEAD_DIM), lambda h, i: (0, 0))],
        out_specs=pl.BlockSpec((tq, HEAD_DIM), lambda h, i: (i, h)),
        out_shape=jax.ShapeDtypeStruct((seq, DIFF_HEADS * HEAD_DIM), BF16),
        compiler_params=_params("parallel", "arbitrary"),
        name="diff_attn",
    )(proj, proj, vt, df_lambda, gain)


def _layer_norm(r, g, b):
    mu = jnp.mean(r, axis=1, keepdims=True)
    d = r - mu
    var = jnp.mean(d * d, axis=1, keepdims=True)
    return d * lax.rsqrt(var + LN_EPS) * g + b


def _proj_ln_kernel(*refs, n_parts):
    a_refs = refs[:n_parts]
    w_ref, x_ref, g_ref, b_ref, o_ref, ob_ref = refs[n_parts:]
    y = None
    off = 0
    for a_ref in a_refs:
        kk = a_ref.shape[1]
        part = _dot(a_ref[...], w_ref[off:off + kk, :])
        y = part if y is None else y + part
        off += kk
    out = _layer_norm(DEEPNORM_ALPHA * x_ref[...] + y, g_ref[...], b_ref[...])
    o_ref[...] = out
    ob_ref[...] = out.astype(BF16)


def _proj_ln(parts, w, x, gain, bias, tm):
    seq, d = x.shape
    kern = functools.partial(_proj_ln_kernel, n_parts=len(parts))
    row = lambda i: (i, 0)
    fixed = lambda i: (0, 0)
    return pl.pallas_call(
        kern,
        grid=(seq // tm,),
        in_specs=[pl.BlockSpec((tm, p.shape[1]), row) for p in parts] + [
            pl.BlockSpec(w.shape, fixed),
            pl.BlockSpec((tm, d), row),
            pl.BlockSpec((1, d), fixed),
            pl.BlockSpec((1, d), fixed)],
        out_specs=[pl.BlockSpec((tm, d), row), pl.BlockSpec((tm, d), row)],
        out_shape=[jax.ShapeDtypeStruct((seq, d), F32), jax.ShapeDtypeStruct((seq, d), BF16)],
        compiler_params=_params("parallel"),
        name="proj_residual_ln",
    )(*parts, w, x, gain, bias)


def _xattn_kernel(q_ref, kv_ref, o_ref, *, d_model, head_dim):
    scale = 1.0 / math.sqrt(head_dim)
    for hh in range(d_model // head_dim):
        lo = hh * head_dim
        q = q_ref[:, lo:lo + head_dim]
        k = kv_ref[:, lo:lo + head_dim]
        v = kv_ref[:, d_model + lo:d_model + lo + head_dim]
        s = _dot_nt(q, k) * scale
        m = jnp.max(s, axis=1, keepdims=True)
        p = jnp.exp(s - m)
        p = p / jnp.sum(p, axis=1, keepdims=True)
        o_ref[:, lo:lo + head_dim] = _dot(p.astype(BF16), v).astype(o_ref.dtype)


def _xattn(q, kv, tm):
    seq, d = q.shape
    kern = functools.partial(_xattn_kernel, d_model=d, head_dim=d // MEM_HEADS)
    return pl.pallas_call(
        kern,
        grid=(seq // tm,),
        in_specs=[pl.BlockSpec((tm, d), lambda i: (i, 0)),
                  pl.BlockSpec(kv.shape, lambda i: (0, 0))],
        out_specs=pl.BlockSpec((tm, d), lambda i: (i, 0)),
        out_shape=jax.ShapeDtypeStruct((seq, d), BF16),
        compiler_params=_params("parallel"),
        name="memory_xattn",
    )(q, kv)


def _top16(s, n_rows, vals_ref):
    iota = lax.broadcasted_iota(jnp.int32, s.shape, 0).astype(F32)

    def step(k, carry):
        work, rank = carry
        m = jnp.max(work, axis=0, keepdims=True)
        first = jnp.min(jnp.where(work == m, iota, float(n_rows)), axis=0, keepdims=True)
        sel = iota == first
        vals_ref[pl.ds(k, 1), :] = m
        return jnp.where(sel, NEG_INF, work), jnp.where(sel, k.astype(F32), rank)

    _, rank = lax.fori_loop(0, PEER_TOPK, step, (s, jnp.full(s.shape, float(PEER_TOPK), F32)))
    return rank


def _route_kernel(pq_ref, keys_ref, e2_ref, r2_ref, w1_ref, n1_ref, v1_ref, v2_ref, vc_ref):
    q = pq_ref[...]
    s1 = _dot_nt(keys_ref[0], q[:, :N_KEYS])
    s2 = _dot_nt(keys_ref[1], q[:, N_KEYS:])
    rank1 = _top16(s1, N_KEYS, v1_ref)
    rank2 = _top16(s2, N_KEYS, v2_ref)
    v2 = v2_ref[...]
    cand = jnp.concatenate([v1_ref[k1:k1 + 1, :] + v2 for k1 in range(PEER_TOPK)], axis=0)
    crank = _top16(cand, PEER_TOPK * PEER_TOPK, vc_ref)
    top = vc_ref[...]
    z = jnp.sum(jnp.exp(top - top[0:1, :]), axis=0, keepdims=True)
    chosen = jnp.where(crank < float(PEER_TOPK), 1.0, 0.0)
    n1 = jnp.zeros(s1.shape, F32)
    for k1 in range(PEER_TOPK):
        cnt = jnp.sum(chosen[k1 * PEER_TOPK:(k1 + 1) * PEER_TOPK, :], axis=0, keepdims=True)
        n1 = jnp.where(rank1 == float(k1), cnt, n1)
    e2_ref[0] = jnp.exp(s2 - v2[0:1, :])
    r2_ref[0] = rank2
    w1_ref[0] = jnp.exp(s1 - v1_ref[0:1, :]) / z
    n1_ref[0] = n1


def _peer_route(pq, keys, tt):
    seq = pq.shape[0]
    qd = 2 * N_KEYS
    tab = pl.BlockSpec((1, N_KEYS, tt), lambda i, h: (h, 0, i))
    shape = jax.ShapeDtypeStruct((PEER_HEADS, N_KEYS, seq), F32)
    return pl.pallas_call(
        _route_kernel,
        grid=(seq // tt, PEER_HEADS),
        in_specs=[pl.BlockSpec((tt, qd), lambda i, h: (i, h)),
                  pl.BlockSpec(keys.shape, lambda i, h: (0, 0, 0))],
        out_specs=[tab, tab, tab, tab],
        out_shape=[shape, shape, shape, shape],
        scratch_shapes=[pltpu.VMEM((PEER_TOPK, tt), F32),
                        pltpu.VMEM((PEER_TOPK, tt), F32),
                        pltpu.VMEM((PEER_TOPK, tt), F32)],
        compiler_params=_params("parallel", "parallel"),
        name="peer_route",
    )(pq, keys)


def _peer_kernel(xb_ref, u_ref, v_ref, e2_ref, r2_ref, w1_ref, n1_ref, x_ref, g_ref, b_ref,
                 o_ref, acc_ref, *, keys_per_step):
    cc = pl.program_id(1)

    @pl.when(cc == 0)
    def _():
        acc_ref[...] = jnp.zeros_like(acc_ref)

    hidden = _dot_nt(xb_ref[...], u_ref[...])
    act = 0.5 * hidden * (1.0 + lax.erf(hidden * math.sqrt(0.5)))
    gates = []
    for part in range(keys_per_step):
        c = cc * keys_per_step + part
        gt = None
        for h in range(PEER_HEADS):
            n1 = n1_ref[h, pl.ds(c, 1), :]
            w1 = w1_ref[h, pl.ds(c, 1), :]
            term = jnp.where(r2_ref[h] < n1, e2_ref[h] * w1, 0.0)
            gt = term if gt is None else gt + term
        gates.append(gt.T)
    gate = gates[0] if keys_per_step == 1 else jnp.concatenate(gates, axis=1)
    acc_ref[...] += _dot((act * gate).astype(BF16), v_ref[...])

    @pl.when(cc == pl.num_programs(1) - 1)
    def _():
        o_ref[...] = _layer_norm(DEEPNORM_ALPHA * x_ref[...] + acc_ref[...], g_ref[...], b_ref[...])


def _peer_dense(xb, x, u, v, tables, gain, bias, tt, keys_per_step):
    seq, d = x.shape
    te = keys_per_step * N_KEYS
    tab = pl.BlockSpec((PEER_HEADS, N_KEYS, tt), lambda i, c: (0, 0, i))
    row = lambda i, c: (i, 0)
    fixed = lambda i, c: (0, 0)
    kern = functools.partial(_peer_kernel, keys_per_step=keys_per_step)
    return pl.pallas_call(
        kern,
        grid=(seq // tt, u.shape[0] // te),
        in_specs=[pl.BlockSpec((tt, d), row),
                  pl.BlockSpec((te, d), lambda i, c: (c, 0)),
                  pl.BlockSpec((te, d), lambda i, c: (c, 0)),
                  tab, tab, tab, tab,
                  pl.BlockSpec((tt, d), row),
                  pl.BlockSpec((1, d), fixed),
                  pl.BlockSpec((1, d), fixed)],
        out_specs=pl.BlockSpec((tt, d), row),
        out_shape=jax.ShapeDtypeStruct((seq, d), F32),
        scratch_shapes=[pltpu.VMEM((tt, d), F32)],
        compiler_params=_params("parallel", "arbitrary"),
        name="peer_dense",
    )(xb, u, v, *tables, x, gain, bias)


def _tile(n, want):
    return min(n, want)


def kernel(x, mem, w_in, sb_norm_gain, df_lambda, df_subln_gain, w_o, ln1_gain, ln1_bias, w_mq, w_mkv, w_mo, ln2_gain, ln2_bias, w_pq, peer_sub_keys, peer_u, peer_v, ln3_gain, ln3_bias):
    b, seq, d = x.shape
    assert b == 1 and w_in.shape[0] == DEPTH
    x2d = x.reshape(seq, d)
    mem2d = mem.reshape(mem.shape[1], d).astype(BF16)
    for l in range(DEPTH):
        lambda_init = 0.8 - 0.6 * math.exp(-0.3 * l)
        row1 = lambda a: a.reshape(1, -1)
        xb = x2d.astype(BF16)
        proj = _inproj(xb, w_in[l].astype(BF16), _tile(seq, 512), 1024,
                       rope_cols=(3 * SB_HEADS * HEAD_DIM, 5 * SB_HEADS * HEAD_DIM))
        sb_o = _sb_attention(proj, sb_norm_gain[l], _tile(seq, 256))
        tq = _tile(seq, 256)
        w_v = w_in[l][:, 5 * SB_HEADS * HEAD_DIM:]
        vt = _vt_proj(xb, w_v.T.astype(BF16), tq)
        df_o = _diff_attention(proj, vt, df_lambda[l], row1(df_subln_gain[l]), tq, lambda_init)
        x1, x1b = _proj_ln([sb_o, df_o], w_o[l].astype(BF16), x2d, row1(ln1_gain[l]), row1(ln1_bias[l]),
                           _tile(seq, 256))
        q = _matmul(x1b, w_mq[l].astype(BF16), _tile(seq, 512), 1024)
        kv = _matmul(mem2d, w_mkv[l].astype(BF16), mem2d.shape[0], 1024)
        xa = _xattn(q, kv, _tile(seq, 256))
        x2, x2b = _proj_ln([xa], w_mo[l].astype(BF16), x1, row1(ln2_gain[l]), row1(ln2_bias[l]),
                           _tile(seq, 256))
        pq = _matmul(x2b, w_pq[l].astype(BF16), _tile(seq, 512), 1024)
        tables = _peer_route(pq, peer_sub_keys[l].astype(BF16), _tile(seq, 256))
        x2d = _peer_dense(x2b, x2, peer_u[l].astype(BF16), peer_v[l].astype(BF16), tables,
                          row1(ln3_gain[l]), row1(ln3_bias[l]), _tile(seq, 256), 2)
    return x2d.reshape(b, seq, d)
```

```python
import functools
import math

import jax
import jax.numpy as jnp
import numpy as np
from jax import lax
from jax.experimental import pallas as pl
from jax.experimental.pallas import tpu as pltpu

F32 = jnp.float32
BF16 = jnp.bfloat16

LANES = 128
MXU_WIDTH = 256
VMEM_LIMIT_BYTES = 56 * 1024 * 1024

DEPTH = 1
SB_HEADS = 8
DIFF_HEADS = 8
HEAD_DIM = 128
DIFF_QK_DIM = 64
ROPE_DIM = 16
ROPE_THETA = 500000.0
MEM_HEADS = 4
PEER_HEADS = 8
N_KEYS = 128
PEER_TOPK = 16
LN_EPS = 1e-5
RMS_EPS = 1e-6
DEEPNORM_ALPHA = (2 * DEPTH) ** 0.25
NEG_INF = float("-inf")
SB_DEAD_LOG = -105.0
SUM_ROWS = 16


def _dot(a, b):
    return jnp.dot(a, b, preferred_element_type=F32)


def _dot_nt(a, b):
    return lax.dot_general(a, b, (((1,), (1,)), ((), ())), preferred_element_type=F32)


def _params(*semantics):
    return pltpu.CompilerParams(dimension_semantics=semantics, vmem_limit_bytes=VMEM_LIMIT_BYTES)


def _matmul_kernel(a_ref, b_ref, o_ref):
    o_ref[...] = _dot(a_ref[...], b_ref[...]).astype(o_ref.dtype)


def _matmul(a, b, tm, tn, out_dtype=BF16):
    m, k = a.shape
    n = b.shape[1]
    return pl.pallas_call(
        _matmul_kernel,
        grid=(m // tm, n // tn),
        in_specs=[pl.BlockSpec((tm, k), lambda i, j: (i, 0)),
                  pl.BlockSpec((k, tn), lambda i, j: (0, j))],
        out_specs=pl.BlockSpec((tm, tn), lambda i, j: (i, j)),
        out_shape=jax.ShapeDtypeStruct((m, n), out_dtype),
        compiler_params=_params("parallel", "parallel"),
        name="matmul",
    )(a, b)


def _inproj_kernel(a_ref, b_ref, c_ref, s1_ref, s2_ref, o_ref, *, rope_lo, rope_hi):
    j = pl.program_id(1)
    acc = _dot(a_ref[...], b_ref[...])
    o_ref[...] = acc.astype(o_ref.dtype)

    @pl.when(jnp.logical_and(j >= rope_lo, j < rope_hi))
    def _():
        cos, sin_up, sin_dn = c_ref[...], s1_ref[...], s2_ref[...]
        for cc in range(acc.shape[1] // LANES):
            t = acc[:, cc * LANES:(cc + 1) * LANES]
            up = pltpu.roll(t, LANES - ROPE_DIM // 2, axis=1)
            dn = pltpu.roll(t, ROPE_DIM // 2, axis=1)
            o_ref[:, cc * LANES:(cc + 1) * LANES] = (t * cos + up * sin_up + dn * sin_dn).astype(o_ref.dtype)


def _rope_tables(seq):
    half = ROPE_DIM // 2
    inv_freq = jnp.power(ROPE_THETA, -jnp.arange(half, dtype=F32) * 2.0 / ROPE_DIM)
    ang = jnp.arange(seq).astype(F32)[:, None] * inv_freq[None, :]
    cos, sin = jnp.cos(ang), jnp.sin(ang)
    ones = jnp.ones((seq, DIFF_QK_DIM - ROPE_DIM), F32)
    zeros = jnp.zeros((seq, DIFF_QK_DIM - ROPE_DIM), F32)
    zh = jnp.zeros((seq, half), F32)
    c64 = jnp.concatenate([cos, cos, ones], axis=1)
    up64 = jnp.concatenate([-sin, zh, zeros], axis=1)
    dn64 = jnp.concatenate([zh, sin, zeros], axis=1)
    rep = LANES // DIFF_QK_DIM
    return jnp.tile(c64, (1, rep)), jnp.tile(up64, (1, rep)), jnp.tile(dn64, (1, rep))


def _inproj(xb, w_in, tm, tn, rope_cols):
    seq, k = xb.shape
    n = w_in.shape[1]
    cos, sin_up, sin_dn = _rope_tables(seq)
    tab = pl.BlockSpec((tm, LANES), lambda i, j: (i, 0))
    kern = functools.partial(_inproj_kernel, rope_lo=rope_cols[0] // tn, rope_hi=rope_cols[1] // tn)
    return pl.pallas_call(
        kern,
        grid=(seq // tm, n // tn),
        in_specs=[pl.BlockSpec((tm, k), lambda i, j: (i, 0)),
                  pl.BlockSpec((k, tn), lambda i, j: (0, j)),
                  tab, tab, tab],
        out_specs=pl.BlockSpec((tm, tn), lambda i, j: (i, j)),
        out_shape=jax.ShapeDtypeStruct((seq, n), BF16),
        compiler_params=_params("parallel", "parallel"),
        name="inproj_rope",
    )(xb, w_in, cos, sin_up, sin_dn)


def _sb_kernel(q_ref, k_ref, v_ref, g_ref, o_ref, *, tq, scale):
    h = pl.program_id(0)
    i = pl.program_id(1)
    q = q_ref[...]
    row = lax.broadcasted_iota(jnp.int32, (tq, tq), 0)
    col = lax.broadcasted_iota(jnp.int32, (tq, tq), 1)
    later = (row > col).astype(BF16)
    before = col < row

    def block(kb, acc, c, masked):
        start = pl.multiple_of(kb * tq, tq)
        k = k_ref[pl.ds(start, tq), :]
        v = v_ref[pl.ds(start, tq), :]
        z = _dot_nt(q, k) * scale
        sp = jnp.log(1.0 + jnp.exp(-jnp.abs(z)))
        log_beta = jnp.minimum(z, 0.0) - sp
        log_keep = -jnp.maximum(z, 0.0) - sp
        if masked:
            log_keep = jnp.where(before, log_keep, 0.0)
        hi = log_keep.astype(BF16)
        lo = (log_keep - hi.astype(F32)).astype(BF16)
        stick = _dot(hi, later) + _dot(lo, later)
        w = jnp.exp(log_beta + stick + c)
        if masked:
            w = jnp.where(before, w, 0.0)
        acc = acc + _dot(w.astype(BF16), v)
        c = c + jnp.sum(log_keep, axis=1, keepdims=True)
        return acc, c

    acc0 = jnp.zeros((tq, HEAD_DIM), F32)
    c0 = jnp.zeros((tq, 1), F32)
    acc, c = block(i, acc0, c0, True)

    def live(carry):
        jj, _, _, cmax = carry
        return jnp.logical_and(jj < i, cmax > SB_DEAD_LOG)

    def older(carry):
        jj, acc, c, _ = carry
        acc, c = block(i - 1 - jj, acc, c, False)
        return jj + 1, acc, c, jnp.max(c)

    _, acc, c, _ = lax.while_loop(live, older, (jnp.int32(0), acc, c, jnp.max(c)))
    g = g_ref[pl.ds(h, 1), :]
    ms = jnp.mean(acc * acc, axis=1, keepdims=True)
    o_ref[...] = (acc * lax.rsqrt(ms + RMS_EPS) * g).astype(o_ref.dtype)


def _sb_attention(proj, gain, tq):
    seq = proj.shape[0]
    kern = functools.partial(_sb_kernel, tq=tq, scale=1.0 / math.sqrt(HEAD_DIM))
    return pl.pallas_call(
        kern,
        grid=(SB_HEADS, seq // tq),
        in_specs=[pl.BlockSpec((tq, HEAD_DIM), lambda h, i: (i, h)),
                  pl.BlockSpec((seq, HEAD_DIM), lambda h, i: (0, SB_HEADS + h)),
                  pl.BlockSpec((seq, HEAD_DIM), lambda h, i: (0, 2 * SB_HEADS + h)),
                  pl.BlockSpec((SB_HEADS, HEAD_DIM), lambda h, i: (0, 0))],
        out_specs=pl.BlockSpec((tq, HEAD_DIM), lambda h, i: (i, h)),
        out_shape=jax.ShapeDtypeStruct((seq, SB_HEADS * HEAD_DIM), BF16),
        compiler_params=_params("parallel", "arbitrary"),
        name="stickbreak_attn",
    )(proj, proj, proj, gain)


def _vt_proj_kernel(w_ref, x_ref, o_ref):
    o_ref[0] = _dot_nt(w_ref[...], x_ref[...]).astype(o_ref.dtype)


def _vt_proj(xb, w_vt, tk):
    seq, k = xb.shape
    n = w_vt.shape[0]
    return pl.pallas_call(
        _vt_proj_kernel,
        grid=(seq // tk,),
        in_specs=[pl.BlockSpec((n, k), lambda j: (0, 0)),
                  pl.BlockSpec((tk, k), lambda j: (j, 0))],
        out_specs=pl.BlockSpec((1, n, tk), lambda j: (j, 0, 0)),
        out_shape=jax.ShapeDtypeStruct((seq // tk, n, tk), BF16),
        compiler_params=_params("parallel"),
        name="vt_proj",
    )(w_vt, xb)


def _diff_kernel(q_ref, k_ref, vt_ref, lam_ref, g_ref, o_ref, *, tq, lambda_init):
    i = pl.program_id(1)
    q = q_ref[...] * jnp.asarray(1.0 / math.sqrt(DIFF_QK_DIM), BF16)
    lane = lax.broadcasted_iota(jnp.int32, (tq, HEAD_DIM), 1)
    zero = jnp.zeros_like(q)
    q1 = jnp.where(lane < DIFF_QK_DIM, q, zero)
    q2 = jnp.where(lane >= DIFF_QK_DIM, q, zero)
    key = lax.broadcasted_iota(jnp.int32, (tq, tq), 0)
    qry = lax.broadcasted_iota(jnp.int32, (tq, tq), 1)
    causal = key <= qry
    ones = jnp.ones((SUM_ROWS, tq), BF16)

    def scores(kb):
        k = k_ref[pl.ds(pl.multiple_of(kb * tq, tq), tq), :]
        return _dot_nt(k, q1), _dot_nt(k, q2)

    def update(kb, s, m, acc):
        m_new = jnp.maximum(m, jnp.max(s, axis=0, keepdims=True))
        alpha = jnp.exp(m - m_new)
        p = jnp.exp(s - m_new).astype(BF16)
        v_aug = jnp.concatenate([vt_ref[kb], ones], axis=0)
        return m_new, alpha * acc + _dot(v_aug, p)

    m0 = jnp.full((1, tq), NEG_INF, F32)
    acc0 = jnp.zeros((HEAD_DIM + SUM_ROWS, tq), F32)
    d1, d2 = scores(i)
    n1, n2 = scores(0)
    m1, a1 = update(i, jnp.where(causal, d1, NEG_INF), m0, acc0)
    m2, a2 = update(i, jnp.where(causal, d2, NEG_INF), m0, acc0)

    def body(t, carry):
        s1, s2, m1, a1, m2, a2 = carry
        nxt = jnp.minimum(t + 1, i - 1)
        n1, n2 = scores(nxt)
        m1, a1 = update(t, s1, m1, a1)
        m2, a2 = update(t, s2, m2, a2)
        return n1, n2, m1, a1, m2, a2

    _, _, m1, a1, m2, a2 = lax.fori_loop(0, i, body, (n1, n2, m1, a1, m2, a2))

    lf = lam_ref[...]
    lam = (jnp.exp(jnp.sum(lf[0:1, :] * lf[1:2, :], axis=1, keepdims=True))
           - jnp.exp(jnp.sum(lf[2:3, :] * lf[3:4, :], axis=1, keepdims=True)) + lambda_init)
    o1 = a1[:HEAD_DIM, :] / a1[HEAD_DIM:HEAD_DIM + 1, :]
    o2 = a2[:HEAD_DIM, :] / a2[HEAD_DIM:HEAD_DIM + 1, :]
    d = o1 - lam * o2
    ms = jnp.mean(d * d, axis=0, keepdims=True)
    dn = (d * lax.rsqrt(ms + RMS_EPS)).T
    o_ref[...] = (dn * g_ref[...] * (1.0 - lambda_init)).astype(o_ref.dtype)


def _diff_attention(proj, vt, df_lambda, gain, tq, lambda_init):
    seq = proj.shape[0]
    qcol = 3 * SB_HEADS
    kcol = qcol + DIFF_HEADS
    kern = functools.partial(_diff_kernel, tq=tq, lambda_init=lambda_init)
    return pl.pallas_call(
        kern,
        grid=(DIFF_HEADS, seq // tq),
        in_specs=[pl.BlockSpec((tq, HEAD_DIM), lambda h, i: (i, qcol + h)),
                  pl.BlockSpec((seq, HEAD_DIM), lambda h, i: (0, kcol + h)),
                  pl.BlockSpec((seq // tq, HEAD_DIM, tq), lambda h, i: (0, h, 0)),
                  pl.BlockSpec((4, DIFF_QK_DIM), lambda h, i: (0, 0)),
                  pl.BlockSpec((1, HEAD_DIM), lambda h, i: (0, 0))],
        out_specs=pl.BlockSpec((tq, HEAD_DIM), lambda h, i: (i, h)),
        out_shape=jax.ShapeDtypeStruct((seq, DIFF_HEADS * HEAD_DIM), BF16),
        compiler_params=_params("parallel", "arbitrary"),
        name="diff_attn",
    )(proj, proj, vt, df_lambda, gain)


def _layer_norm(r, g, b):
    mu = jnp.mean(r, axis=1, keepdims=True)
    d = r - mu
    var = jnp.mean(d * d, axis=1, keepdims=True)
    return d * lax.rsqrt(var + LN_EPS) * g + b


def _proj_ln_kernel(*refs, n_parts):
    a_refs = refs[:n_parts]
    w_ref, x_ref, g_ref, b_ref, o_ref, ob_ref = refs[n_parts:]
    y = None
    off = 0
    for a_ref in a_refs:
        kk = a_ref.shape[1]
        part = _dot(a_ref[...], w_ref[off:off + kk, :])
        y = part if y is None else y + part
        off += kk
    out = _layer_norm(DEEPNORM_ALPHA * x_ref[...] + y, g_ref[...], b_ref[...])
    o_ref[...] = out
    ob_ref[...] = out.astype(BF16)


def _proj_ln(parts, w, x, gain, bias, tm):
    seq, d = x.shape
    kern = functools.partial(_proj_ln_kernel, n_parts=len(parts))
    row = lambda i: (i, 0)
    fixed = lambda i: (0, 0)
    return pl.pallas_call(
        kern,
        grid=(seq // tm,),
        in_specs=[pl.BlockSpec((tm, p.shape[1]), row) for p in parts] + [
            pl.BlockSpec(w.shape, fixed),
            pl.BlockSpec((tm, d), row),
            pl.BlockSpec((1, d), fixed),
            pl.BlockSpec((1, d), fixed)],
        out_specs=[pl.BlockSpec((tm, d), row), pl.BlockSpec((tm, d), row)],
        out_shape=[jax.ShapeDtypeStruct((seq, d), F32), jax.ShapeDtypeStruct((seq, d), BF16)],
        compiler_params=_params("parallel"),
        name="proj_residual_ln",
    )(*parts, w, x, gain, bias)


def _xattn_kernel(q_ref, kv_ref, o_ref, *, d_model, head_dim):
    scale = 1.0 / math.sqrt(head_dim)
    for hh in range(d_model // head_dim):
        lo = hh * head_dim
        q = q_ref[:, lo:lo + head_dim]
        k = kv_ref[:, lo:lo + head_dim]
        v = kv_ref[:, d_model + lo:d_model + lo + head_dim]
        s = _dot_nt(q, k) * scale
        m = jnp.max(s, axis=1, keepdims=True)
        p = jnp.exp(s - m)
        p = p / jnp.sum(p, axis=1, keepdims=True)
        o_ref[:, lo:lo + head_dim] = _dot(p.astype(BF16), v).astype(o_ref.dtype)


def _xattn(q, kv, tm):
    seq, d = q.shape
    kern = functools.partial(_xattn_kernel, d_model=d, head_dim=d // MEM_HEADS)
    return pl.pallas_call(
        kern,
        grid=(seq // tm,),
        in_specs=[pl.BlockSpec((tm, d), lambda i: (i, 0)),
                  pl.BlockSpec(kv.shape, lambda i: (0, 0))],
        out_specs=pl.BlockSpec((tm, d), lambda i: (i, 0)),
        out_shape=jax.ShapeDtypeStruct((seq, d), BF16),
        compiler_params=_params("parallel"),
        name="memory_xattn",
    )(q, kv)


def _top16(s, n_rows, vals_ref):
    iota = lax.broadcasted_iota(jnp.int32, s.shape, 0).astype(F32)

    def step(k, carry):
        work, rank = carry
        m = jnp.max(work, axis=0, keepdims=True)
        first = jnp.min(jnp.where(work == m, iota, float(n_rows)), axis=0, keepdims=True)
        sel = iota == first
        vals_ref[pl.ds(k, 1), :] = m
        return jnp.where(sel, NEG_INF, work), jnp.where(sel, k.astype(F32), rank)

    _, rank = lax.fori_loop(0, PEER_TOPK, step, (s, jnp.full(s.shape, float(PEER_TOPK), F32)))
    return rank


CAND_KEEP = tuple(PEER_TOPK // (k1 + 1) for k1 in range(PEER_TOPK))
CAND_ROWS = -(-sum(CAND_KEEP) // 8) * 8


def _route_kernel(pq_ref, keys_ref, e2_ref, r2_ref, w1_ref, n1_ref, v1_ref, v2_ref, vc_ref, cand_ref):
    q = pq_ref[...]
    s1 = _dot_nt(keys_ref[0], q[:, :N_KEYS])
    s2 = _dot_nt(keys_ref[1], q[:, N_KEYS:])
    rank1 = _top16(s1, N_KEYS, v1_ref)
    rank2 = _top16(s2, N_KEYS, v2_ref)
    cand_ref[...] = jnp.full(cand_ref.shape, NEG_INF, F32)
    row = lax.broadcasted_iota(jnp.int32, cand_ref.shape, 0)
    seg = jnp.zeros(cand_ref.shape, F32)
    off = 0
    for k1, keep in enumerate(CAND_KEEP):
        cand_ref[off:off + keep, :] = v1_ref[k1:k1 + 1, :] + v2_ref[0:keep, :]
        off += keep
        seg = seg + jnp.where(row >= off, 1.0, 0.0)
    crank = _top16(cand_ref[...], CAND_ROWS, vc_ref)
    top = vc_ref[...]
    z = jnp.sum(jnp.exp(top - top[0:1, :]), axis=0, keepdims=True)
    chosen = jnp.where(crank < float(PEER_TOPK), 1.0, 0.0)
    n1 = jnp.zeros(s1.shape, F32)
    for k1 in range(PEER_TOPK):
        cnt = jnp.sum(jnp.where(seg == float(k1), chosen, 0.0), axis=0, keepdims=True)
        n1 = jnp.where(rank1 == float(k1), cnt, n1)
    v2 = v2_ref[...]
    e2_ref[0] = jnp.exp(s2 - v2[0:1, :]).astype(e2_ref.dtype)
    r2_ref[0] = rank2.astype(r2_ref.dtype)
    w1_ref[0] = jnp.exp(s1 - v1_ref[0:1, :]) / z
    n1_ref[0] = n1


def _peer_route(pq, keys, tt):
    seq = pq.shape[0]
    qd = 2 * N_KEYS
    tab = pl.BlockSpec((1, N_KEYS, tt), lambda i, h: (h, 0, i))
    shape = jax.ShapeDtypeStruct((PEER_HEADS, N_KEYS, seq), F32)
    packed = jax.ShapeDtypeStruct((PEER_HEADS, N_KEYS, seq), BF16)
    return pl.pallas_call(
        _route_kernel,
        grid=(seq // tt, PEER_HEADS),
        in_specs=[pl.BlockSpec((tt, qd), lambda i, h: (i, h)),
                  pl.BlockSpec(keys.shape, lambda i, h: (0, 0, 0))],
        out_specs=[tab, tab, tab, tab],
        out_shape=[packed, packed, shape, shape],
        scratch_shapes=[pltpu.VMEM((PEER_TOPK, tt), F32),
                        pltpu.VMEM((PEER_TOPK, tt), F32),
                        pltpu.VMEM((PEER_TOPK, tt), F32),
                        pltpu.VMEM((CAND_ROWS, tt), F32)],
        compiler_params=_params("parallel", "parallel"),
        name="peer_route",
    )(pq, keys)


def _peer_kernel(xb_ref, u_ref, v_ref, e2_ref, r2_ref, w1_ref, n1_ref, x_ref, g_ref, b_ref,
                 o_ref, acc_ref, a_ref, gate_ref, *, keys_per_step, n_chunks):
    cc = pl.program_id(1)
    cur, prev = cc % 2, (cc + 1) % 2

    @pl.when(cc == 0)
    def _():
        acc_ref[...] = jnp.zeros_like(acc_ref)
        a_ref[...] = jnp.zeros_like(a_ref)
        gate_ref[...] = jnp.zeros_like(gate_ref)

    gated = (a_ref[prev].astype(F32) * gate_ref[prev]).astype(BF16)
    acc_ref[...] += _dot(gated, v_ref[...])

    hidden = _dot(xb_ref[...], u_ref[...])
    a_ref[cur] = (0.5 * hidden * (1.0 + lax.erf(hidden * math.sqrt(0.5)))).astype(BF16)

    chunk = jnp.minimum(cc, n_chunks - 1)
    zero = jnp.zeros(r2_ref.shape[1:], BF16)
    for part in range(keys_per_step):
        c = chunk * keys_per_step + part
        gt = None
        for h in range(PEER_HEADS):
            n1 = n1_ref[h, pl.ds(c, 1), :].astype(BF16)
            w1 = w1_ref[h, pl.ds(c, 1), :].astype(BF16)
            term = jnp.where(r2_ref[h] < n1, e2_ref[h] * w1, zero)
            gt = term if gt is None else gt + term
        gate_ref[cur, :, part * N_KEYS:(part + 1) * N_KEYS] = gt.astype(F32).T

    @pl.when(cc == n_chunks)
    def _():
        o_ref[...] = _layer_norm(DEEPNORM_ALPHA * x_ref[...] + acc_ref[...], g_ref[...], b_ref[...])


def _peer_dense(xb, x, ut, v, tables, gain, bias, tt, keys_per_step):
    seq, d = x.shape
    te = keys_per_step * N_KEYS
    n_chunks = v.shape[0] // te
    tab = pl.BlockSpec((PEER_HEADS, N_KEYS, tt), lambda i, c: (0, 0, i))
    row = lambda i, c: (i, 0)
    fixed = lambda i, c: (0, 0)
    kern = functools.partial(_peer_kernel, keys_per_step=keys_per_step, n_chunks=n_chunks)
    return pl.pallas_call(
        kern,
        grid=(seq // tt, n_chunks + 1),
        in_specs=[pl.BlockSpec((tt, d), row),
                  pl.BlockSpec((d, te), lambda i, c: (0, jnp.minimum(c, n_chunks - 1))),
                  pl.BlockSpec((te, d), lambda i, c: (jnp.maximum(c - 1, 0), 0)),
                  tab, tab, tab, tab,
                  pl.BlockSpec((tt, d), row),
                  pl.BlockSpec((1, d), fixed),
                  pl.BlockSpec((1, d), fixed)],
        out_specs=pl.BlockSpec((tt, d), row),
        out_shape=jax.ShapeDtypeStruct((seq, d), F32),
        scratch_shapes=[pltpu.VMEM((tt, d), F32), pltpu.VMEM((2, tt, te), BF16),
                        pltpu.VMEM((2, tt, te), F32)],
        compiler_params=_params("parallel", "arbitrary"),
        name="peer_dense",
    )(xb, ut, v, *tables, x, gain, bias)


def _tile(n, want):
    return min(n, want)


def kernel(x, mem, w_in, sb_norm_gain, df_lambda, df_subln_gain, w_o, ln1_gain, ln1_bias, w_mq, w_mkv, w_mo, ln2_gain, ln2_bias, w_pq, peer_sub_keys, peer_u, peer_v, ln3_gain, ln3_bias):
    b, seq, d = x.shape
    assert b == 1 and w_in.shape[0] == DEPTH
    x2d = x.reshape(seq, d)
    mem2d = mem.reshape(mem.shape[1], d).astype(BF16)
    for l in range(DEPTH):
        lambda_init = 0.8 - 0.6 * math.exp(-0.3 * l)
        row1 = lambda a: a.reshape(1, -1)
        xb = x2d.astype(BF16)
        proj = _inproj(xb, w_in[l].astype(BF16), _tile(seq, 512), 1024,
                       rope_cols=(3 * SB_HEADS * HEAD_DIM, 5 * SB_HEADS * HEAD_DIM))
        sb_o = _sb_attention(proj, sb_norm_gain[l], _tile(seq, 256))
        tq = _tile(seq, 256)
        w_v = w_in[l][:, 5 * SB_HEADS * HEAD_DIM:]
        vt = _vt_proj(xb, w_v.T.astype(BF16), tq)
        df_o = _diff_attention(proj, vt, df_lambda[l], row1(df_subln_gain[l]), tq, lambda_init)
        x1, x1b = _proj_ln([sb_o, df_o], w_o[l].astype(BF16), x2d, row1(ln1_gain[l]), row1(ln1_bias[l]),
                           _tile(seq, 256))
        q = _matmul(x1b, w_mq[l].astype(BF16), _tile(seq, 512), 1024)
        kv = _matmul(mem2d, w_mkv[l].astype(BF16), mem2d.shape[0], 1024)
        xa = _xattn(q, kv, _tile(seq, 256))
        x2, x2b = _proj_ln([xa], w_mo[l].astype(BF16), x1, row1(ln2_gain[l]), row1(ln2_bias[l]),
                           _tile(seq, 256))
        pq = _matmul(x2b, w_pq[l].astype(BF16), _tile(seq, 512), 1024)
        tables = _peer_route(pq, peer_sub_keys[l].astype(BF16), _tile(seq, 256))
        x2d = _peer_dense(x2b, x2, peer_u[l].T.astype(BF16), peer_v[l].astype(BF16), tables,
                          row1(ln3_gain[l]), row1(ln3_bias[l]), _tile(seq, 256), 8)
    return x2d.reshape(b, seq, d)
```

```python
import functools
import math

import jax
import jax.numpy as jnp
import numpy as np
from jax import lax
from jax.experimental import pallas as pl
from jax.experimental.pallas import tpu as pltpu

F32 = jnp.float32
BF16 = jnp.bfloat16

LANES = 128
MXU_WIDTH = 256
VMEM_LIMIT_BYTES = 56 * 1024 * 1024

DEPTH = 1
SB_HEADS = 8
DIFF_HEADS = 8
HEAD_DIM = 128
DIFF_QK_DIM = 64
ROPE_DIM = 16
ROPE_THETA = 500000.0
MEM_HEADS = 4
PEER_HEADS = 8
N_KEYS = 128
PEER_TOPK = 16
LN_EPS = 1e-5
RMS_EPS = 1e-6
DEEPNORM_ALPHA = (2 * DEPTH) ** 0.25
NEG_INF = float("-inf")
SB_DEAD_LOG = -105.0
SUM_ROWS = 16


def _dot(a, b):
    return jnp.dot(a, b, preferred_element_type=F32)


def _dot_nt(a, b):
    return lax.dot_general(a, b, (((1,), (1,)), ((), ())), preferred_element_type=F32)


def _params(*semantics):
    return pltpu.CompilerParams(dimension_semantics=semantics, vmem_limit_bytes=VMEM_LIMIT_BYTES)


def _matmul_kernel(a_ref, b_ref, o_ref):
    o_ref[...] = _dot(a_ref[...], b_ref[...]).astype(o_ref.dtype)


def _matmul(a, b, tm, tn, out_dtype=BF16):
    m, k = a.shape
    n = b.shape[1]
    return pl.pallas_call(
        _matmul_kernel,
        grid=(m // tm, n // tn),
        in_specs=[pl.BlockSpec((tm, k), lambda i, j: (i, 0)),
                  pl.BlockSpec((k, tn), lambda i, j: (0, j))],
        out_specs=pl.BlockSpec((tm, tn), lambda i, j: (i, j)),
        out_shape=jax.ShapeDtypeStruct((m, n), out_dtype),
        compiler_params=_params("parallel", "parallel"),
        name="matmul",
    )(a, b)


def _inproj_kernel(a_ref, b_ref, c_ref, s1_ref, s2_ref, o_ref, *, rope_lo, rope_hi):
    j = pl.program_id(1)
    acc = _dot(a_ref[...], b_ref[...])
    o_ref[...] = acc.astype(o_ref.dtype)

    @pl.when(jnp.logical_and(j >= rope_lo, j < rope_hi))
    def _():
        cos, sin_up, sin_dn = c_ref[...], s1_ref[...], s2_ref[...]
        for cc in range(acc.shape[1] // LANES):
            t = acc[:, cc * LANES:(cc + 1) * LANES]
            up = pltpu.roll(t, LANES - ROPE_DIM // 2, axis=1)
            dn = pltpu.roll(t, ROPE_DIM // 2, axis=1)
            o_ref[:, cc * LANES:(cc + 1) * LANES] = (t * cos + up * sin_up + dn * sin_dn).astype(o_ref.dtype)


def _rope_tables(seq):
    half = ROPE_DIM // 2
    inv_freq = jnp.power(ROPE_THETA, -jnp.arange(half, dtype=F32) * 2.0 / ROPE_DIM)
    ang = jnp.arange(seq).astype(F32)[:, None] * inv_freq[None, :]
    cos, sin = jnp.cos(ang), jnp.sin(ang)
    ones = jnp.ones((seq, DIFF_QK_DIM - ROPE_DIM), F32)
    zeros = jnp.zeros((seq, DIFF_QK_DIM - ROPE_DIM), F32)
    zh = jnp.zeros((seq, half), F32)
    c64 = jnp.concatenate([cos, cos, ones], axis=1)
    up64 = jnp.concatenate([-sin, zh, zeros], axis=1)
    dn64 = jnp.concatenate([zh, sin, zeros], axis=1)
    rep = LANES // DIFF_QK_DIM
    return jnp.tile(c64, (1, rep)), jnp.tile(up64, (1, rep)), jnp.tile(dn64, (1, rep))


def _inproj(xb, w_in, tm, tn, rope_cols):
    seq, k = xb.shape
    n = w_in.shape[1]
    cos, sin_up, sin_dn = _rope_tables(seq)
    tab = pl.BlockSpec((tm, LANES), lambda i, j: (i, 0))
    kern = functools.partial(_inproj_kernel, rope_lo=rope_cols[0] // tn, rope_hi=rope_cols[1] // tn)
    return pl.pallas_call(
        kern,
        grid=(seq // tm, n // tn),
        in_specs=[pl.BlockSpec((tm, k), lambda i, j: (i, 0)),
                  pl.BlockSpec((k, tn), lambda i, j: (0, j)),
                  tab, tab, tab],
        out_specs=pl.BlockSpec((tm, tn), lambda i, j: (i, j)),
        out_shape=jax.ShapeDtypeStruct((seq, n), BF16),
        compiler_params=_params("parallel", "parallel"),
        name="inproj_rope",
    )(xb, w_in, cos, sin_up, sin_dn)


def _sb_kernel(q_ref, k_ref, v_ref, g_ref, o_ref, *, tq, scale):
    h = pl.program_id(0)
    i = pl.program_id(1)
    q = q_ref[...]
    row = lax.broadcasted_iota(jnp.int32, (tq, tq), 0)
    col = lax.broadcasted_iota(jnp.int32, (tq, tq), 1)
    later = (row > col).astype(BF16)
    before = col < row

    def block(kb, acc, c, masked):
        start = pl.multiple_of(kb * tq, tq)
        k = k_ref[pl.ds(start, tq), :]
        v = v_ref[pl.ds(start, tq), :]
        z = _dot_nt(q, k) * scale
        sp = jnp.log(1.0 + jnp.exp(-jnp.abs(z)))
        log_beta = jnp.minimum(z, 0.0) - sp
        log_keep = -jnp.maximum(z, 0.0) - sp
        if masked:
            log_keep = jnp.where(before, log_keep, 0.0)
        hi = log_keep.astype(BF16)
        lo = (log_keep - hi.astype(F32)).astype(BF16)
        stick = _dot(hi, later) + _dot(lo, later)
        w = jnp.exp(log_beta + stick + c)
        if masked:
            w = jnp.where(before, w, 0.0)
        acc = acc + _dot(w.astype(BF16), v)
        c = c + jnp.sum(log_keep, axis=1, keepdims=True)
        return acc, c

    acc0 = jnp.zeros((tq, HEAD_DIM), F32)
    c0 = jnp.zeros((tq, 1), F32)
    acc, c = block(i, acc0, c0, True)

    def live(carry):
        jj, _, _, cmax = carry
        return jnp.logical_and(jj < i, cmax > SB_DEAD_LOG)

    def older(carry):
        jj, acc, c, _ = carry
        acc, c = block(i - 1 - jj, acc, c, False)
        return jj + 1, acc, c, jnp.max(c)

    _, acc, c, _ = lax.while_loop(live, older, (jnp.int32(0), acc, c, jnp.max(c)))
    g = g_ref[pl.ds(h, 1), :]
    ms = jnp.mean(acc * acc, axis=1, keepdims=True)
    o_ref[...] = (acc * lax.rsqrt(ms + RMS_EPS) * g).astype(o_ref.dtype)


def _sb_attention(proj, gain, tq):
    seq = proj.shape[0]
    kern = functools.partial(_sb_kernel, tq=tq, scale=1.0 / math.sqrt(HEAD_DIM))
    return pl.pallas_call(
        kern,
        grid=(SB_HEADS, seq // tq),
        in_specs=[pl.BlockSpec((tq, HEAD_DIM), lambda h, i: (i, h)),
                  pl.BlockSpec((seq, HEAD_DIM), lambda h, i: (0, SB_HEADS + h)),
                  pl.BlockSpec((seq, HEAD_DIM), lambda h, i: (0, 2 * SB_HEADS + h)),
                  pl.BlockSpec((SB_HEADS, HEAD_DIM), lambda h, i: (0, 0))],
        out_specs=pl.BlockSpec((tq, HEAD_DIM), lambda h, i: (i, h)),
        out_shape=jax.ShapeDtypeStruct((seq, SB_HEADS * HEAD_DIM), BF16),
        compiler_params=_params("parallel", "arbitrary"),
        name="stickbreak_attn",
    )(proj, proj, proj, gain)


def _vt_proj_kernel(w_ref, x_ref, o_ref):
    o_ref[0] = _dot_nt(w_ref[...], x_ref[...]).astype(o_ref.dtype)


def _vt_proj(xb, w_vt, tk):
    seq, k = xb.shape
    n = w_vt.shape[0]
    return pl.pallas_call(
        _vt_proj_kernel,
        grid=(seq // tk,),
        in_specs=[pl.BlockSpec((n, k), lambda j: (0, 0)),
                  pl.BlockSpec((tk, k), lambda j: (j, 0))],
        out_specs=pl.BlockSpec((1, n, tk), lambda j: (j, 0, 0)),
        out_shape=jax.ShapeDtypeStruct((seq // tk, n, tk), BF16),
        compiler_params=_params("parallel"),
        name="vt_proj",
    )(w_vt, xb)


def _diff_kernel(q_ref, k_ref, vt_ref, lam_ref, g_ref, o_ref, s0_ref, s1_ref, p0_ref, p1_ref, acc_ref,
                 *, tq, lambda_init):
    i = pl.program_id(1)
    s_refs, p_refs = (s0_ref, s1_ref), (p0_ref, p1_ref)
    q = q_ref[...] * jnp.asarray(1.0 / math.sqrt(DIFF_QK_DIM), BF16)
    lane = lax.broadcasted_iota(jnp.int32, (tq, HEAD_DIM), 1)
    zero = jnp.zeros_like(q)
    qmaps = (jnp.where(lane < DIFF_QK_DIM, q, zero),
             jnp.where(lane >= DIFF_QK_DIM, q, zero))
    ones = jnp.ones((SUM_ROWS, tq), BF16)

    def scores(kb, mp):
        k = k_ref[pl.ds(pl.multiple_of(kb * tq, tq), tq), :]
        return _dot_nt(k, qmaps[mp])

    def softmax_step(s, m, slot, mp):
        m_new = jnp.maximum(m, jnp.max(s, axis=0, keepdims=True))
        p_refs[slot][mp] = jnp.exp(s - m_new).astype(BF16)
        return m_new, jnp.exp(m - m_new)

    def accumulate(kb, alpha, slot, mp):
        v_aug = jnp.concatenate([vt_ref[kb], ones], axis=0)
        acc_ref[mp] = alpha * acc_ref[mp] + _dot(v_aug, p_refs[slot][mp])

    key = lax.broadcasted_iota(jnp.int32, (tq, tq), 0)
    qry = lax.broadcasted_iota(jnp.int32, (tq, tq), 1)
    causal = key <= qry
    m0 = jnp.full((1, tq), NEG_INF, F32)
    acc_ref[...] = jnp.zeros_like(acc_ref)
    stats = []
    for mp in range(2):
        s_refs[1][mp] = scores(0, mp)
        stats.extend(softmax_step(jnp.where(causal, scores(i, mp), NEG_INF), m0, 0, mp))
    stats = tuple(stats)

    def step(t, carry, cur):
        nxt = 1 - cur
        vb = jnp.where(t == 0, i, t - 1)
        out = []
        for mp in range(2):
            m, alpha = carry[2 * mp], carry[2 * mp + 1]
            s_refs[cur][mp] = scores(jnp.minimum(t + 1, i - 1), mp)
            accumulate(vb, alpha, cur, mp)
            out.extend(softmax_step(s_refs[nxt][mp], m, nxt, mp))
        return tuple(out)

    stats = lax.fori_loop(0, i // 2, lambda u, c: step(2 * u + 1, step(2 * u, c, 0), 1), stats)
    odd = i % 2 == 1
    stats = lax.cond(odd, lambda c: step(i - 1, c, 0), lambda c: c, stats)
    last = jnp.maximum(i - 1, 0)
    for slot in range(2):
        @pl.when((i % 2) == slot)
        def _():
            for mp in range(2):
                accumulate(last, stats[2 * mp + 1], slot, mp)

    lf = lam_ref[...]
    lam = (jnp.exp(jnp.sum(lf[0:1, :] * lf[1:2, :], axis=1, keepdims=True))
           - jnp.exp(jnp.sum(lf[2:3, :] * lf[3:4, :], axis=1, keepdims=True)) + lambda_init)
    o1 = acc_ref[0, :HEAD_DIM, :] / acc_ref[0, HEAD_DIM:HEAD_DIM + 1, :]
    o2 = acc_ref[1, :HEAD_DIM, :] / acc_ref[1, HEAD_DIM:HEAD_DIM + 1, :]
    d = o1 - lam * o2
    ms = jnp.mean(d * d, axis=0, keepdims=True)
    dn = (d * lax.rsqrt(ms + RMS_EPS)).T
    o_ref[...] = (dn * g_ref[...] * (1.0 - lambda_init)).astype(o_ref.dtype)


def _diff_attention(proj, vt, df_lambda, gain, tq, lambda_init):
    seq = proj.shape[0]
    qcol = 3 * SB_HEADS
    kcol = qcol + DIFF_HEADS
    kern = functools.partial(_diff_kernel, tq=tq, lambda_init=lambda_init)
    return pl.pallas_call(
        kern,
        grid=(DIFF_HEADS, seq // tq),
        in_specs=[pl.BlockSpec((tq, HEAD_DIM), lambda h, i: (i, qcol + h)),
                  pl.BlockSpec((seq, HEAD_DIM), lambda h, i: (0, kcol + h)),
                  pl.BlockSpec((seq // tq, HEAD_DIM, tq), lambda h, i: (0, h, 0)),
                  pl.BlockSpec((4, DIFF_QK_DIM), lambda h, i: (0, 0)),
                  pl.BlockSpec((1, HEAD_DIM), lambda h, i: (0, 0))],
        out_specs=pl.BlockSpec((tq, HEAD_DIM), lambda h, i: (i, h)),
        out_shape=jax.ShapeDtypeStruct((seq, DIFF_HEADS * HEAD_DIM), BF16),
        scratch_shapes=[pltpu.VMEM((2, tq, tq), F32), pltpu.VMEM((2, tq, tq), F32),
                        pltpu.VMEM((2, tq, tq), BF16), pltpu.VMEM((2, tq, tq), BF16),
                        pltpu.VMEM((2, HEAD_DIM + SUM_ROWS, tq), F32)],
        compiler_params=_params("parallel", "arbitrary"),
        name="diff_attn",
    )(proj, proj, vt, df_lambda, gain)


def _layer_norm(r, g, b):
    mu = jnp.mean(r, axis=1, keepdims=True)
    d = r - mu
    var = jnp.mean(d * d, axis=1, keepdims=True)
    return d * lax.rsqrt(var + LN_EPS) * g + b


def _proj_ln_kernel(*refs, n_parts):
    a_refs = refs[:n_parts]
    w_ref, x_ref, g_ref, b_ref, o_ref, ob_ref = refs[n_parts:]
    y = None
    off = 0
    for a_ref in a_refs:
        kk = a_ref.shape[1]
        part = _dot(a_ref[...], w_ref[off:off + kk, :])
        y = part if y is None else y + part
        off += kk
    out = _layer_norm(DEEPNORM_ALPHA * x_ref[...] + y, g_ref[...], b_ref[...])
    o_ref[...] = out
    ob_ref[...] = out.astype(BF16)


def _proj_ln(parts, w, x, gain, bias, tm):
    seq, d = x.shape
    kern = functools.partial(_proj_ln_kernel, n_parts=len(parts))
    row = lambda i: (i, 0)
    fixed = lambda i: (0, 0)
    return pl.pallas_call(
        kern,
        grid=(seq // tm,),
        in_specs=[pl.BlockSpec((tm, p.shape[1]), row) for p in parts] + [
            pl.BlockSpec(w.shape, fixed),
            pl.BlockSpec((tm, d), row),
            pl.BlockSpec((1, d), fixed),
            pl.BlockSpec((1, d), fixed)],
        out_specs=[pl.BlockSpec((tm, d), row), pl.BlockSpec((tm, d), row)],
        out_shape=[jax.ShapeDtypeStruct((seq, d), F32), jax.ShapeDtypeStruct((seq, d), BF16)],
        compiler_params=_params("parallel"),
        name="proj_residual_ln",
    )(*parts, w, x, gain, bias)


def _xattn_kernel(q_ref, kv_ref, o_ref, *, d_model, head_dim):
    scale = 1.0 / math.sqrt(head_dim)
    for hh in range(d_model // head_dim):
        lo = hh * head_dim
        q = q_ref[:, lo:lo + head_dim]
        k = kv_ref[:, lo:lo + head_dim]
        v = kv_ref[:, d_model + lo:d_model + lo + head_dim]
        s = _dot_nt(q, k) * scale
        m = jnp.max(s, axis=1, keepdims=True)
        p = jnp.exp(s - m)
        p = p / jnp.sum(p, axis=1, keepdims=True)
        o_ref[:, lo:lo + head_dim] = _dot(p.astype(BF16), v).astype(o_ref.dtype)


def _xattn(q, kv, tm):
    seq, d = q.shape
    kern = functools.partial(_xattn_kernel, d_model=d, head_dim=d // MEM_HEADS)
    return pl.pallas_call(
        kern,
        grid=(seq // tm,),
        in_specs=[pl.BlockSpec((tm, d), lambda i: (i, 0)),
                  pl.BlockSpec(kv.shape, lambda i: (0, 0))],
        out_specs=pl.BlockSpec((tm, d), lambda i: (i, 0)),
        out_shape=jax.ShapeDtypeStruct((seq, d), BF16),
        compiler_params=_params("parallel"),
        name="memory_xattn",
    )(q, kv)


def _top16(s, n_rows, vals_ref):
    iota = lax.broadcasted_iota(jnp.int32, s.shape, 0).astype(F32)

    def step(k, carry):
        work, rank = carry
        m = jnp.max(work, axis=0, keepdims=True)
        first = jnp.min(jnp.where(work == m, iota, float(n_rows)), axis=0, keepdims=True)
        sel = iota == first
        vals_ref[pl.ds(k, 1), :] = m
        return jnp.where(sel, NEG_INF, work), jnp.where(sel, lax.convert_element_type(k, F32), rank)

    _, rank = lax.fori_loop(0, PEER_TOPK, step, (s, jnp.full(s.shape, float(PEER_TOPK), F32)))
    return rank


CAND_KEEP = tuple(PEER_TOPK // (k1 + 1) for k1 in range(PEER_TOPK))
CAND_ROWS = -(-sum(CAND_KEEP) // 8) * 8


def _route_kernel(pq_ref, keys_ref, e2_ref, r2_ref, w1_ref, n1_ref, v1_ref, v2_ref, vc_ref, cand_ref):
    q = pq_ref[...]
    s1 = _dot_nt(keys_ref[0], q[:, :N_KEYS])
    s2 = _dot_nt(keys_ref[1], q[:, N_KEYS:])
    rank1 = _top16(s1, N_KEYS, v1_ref)
    rank2 = _top16(s2, N_KEYS, v2_ref)
    cand_ref[...] = jnp.full(cand_ref.shape, NEG_INF, F32)
    row = lax.broadcasted_iota(jnp.int32, cand_ref.shape, 0)
    seg = jnp.zeros(cand_ref.shape, F32)
    off = 0
    for k1, keep in enumerate(CAND_KEEP):
        cand_ref[off:off + keep, :] = v1_ref[k1:k1 + 1, :] + v2_ref[0:keep, :]
        off += keep
        seg = seg + jnp.where(row >= off, 1.0, 0.0)
    crank = _top16(cand_ref[...], CAND_ROWS, vc_ref)
    top = vc_ref[...]
    z = jnp.sum(jnp.exp(top - top[0:1, :]), axis=0, keepdims=True)
    chosen = jnp.where(crank < float(PEER_TOPK), 1.0, 0.0)
    n1 = jnp.zeros(s1.shape, F32)
    for k1 in range(PEER_TOPK):
        cnt = jnp.sum(jnp.where(seg == float(k1), chosen, 0.0), axis=0, keepdims=True)
        n1 = jnp.where(rank1 == float(k1), cnt, n1)
    v2 = v2_ref[...]
    e2_ref[0] = jnp.exp(s2 - v2[0:1, :]).astype(e2_ref.dtype)
    r2_ref[0] = rank2.astype(r2_ref.dtype)
    w1_ref[0] = jnp.exp(s1 - v1_ref[0:1, :]) / z
    n1_ref[0] = n1


def _peer_route(pq, keys, tt):
    seq = pq.shape[0]
    qd = 2 * N_KEYS
    tab = pl.BlockSpec((1, N_KEYS, tt), lambda i, h: (h, 0, i))
    shape = jax.ShapeDtypeStruct((PEER_HEADS, N_KEYS, seq), F32)
    packed = jax.ShapeDtypeStruct((PEER_HEADS, N_KEYS, seq), BF16)
    return pl.pallas_call(
        _route_kernel,
        grid=(seq // tt, PEER_HEADS),
        in_specs=[pl.BlockSpec((tt, qd), lambda i, h: (i, h)),
                  pl.BlockSpec(keys.shape, lambda i, h: (0, 0, 0))],
        out_specs=[tab, tab, tab, tab],
        out_shape=[packed, packed, shape, shape],
        scratch_shapes=[pltpu.VMEM((PEER_TOPK, tt), F32),
                        pltpu.VMEM((PEER_TOPK, tt), F32),
                        pltpu.VMEM((PEER_TOPK, tt), F32),
                        pltpu.VMEM((CAND_ROWS, tt), F32)],
        compiler_params=_params("parallel", "parallel"),
        name="peer_route",
    )(pq, keys)


def _peer_kernel(xb_ref, u_ref, v_ref, e2_ref, r2_ref, w1_ref, n1_ref, x_ref, g_ref, b_ref,
                 o_ref, acc_ref, a_ref, gate_ref, *, keys_per_step, n_chunks):
    cc = pl.program_id(1)
    cur, prev = cc % 2, (cc + 1) % 2

    @pl.when(cc == 0)
    def _():
        acc_ref[...] = jnp.zeros_like(acc_ref)
        a_ref[...] = jnp.zeros_like(a_ref)
        gate_ref[...] = jnp.zeros_like(gate_ref)

    gated = (a_ref[prev].astype(F32) * gate_ref[prev]).astype(BF16)
    acc_ref[...] += _dot(gated, v_ref[...])

    hidden = _dot(xb_ref[...], u_ref[...])
    a_ref[cur] = (0.5 * hidden * (1.0 + lax.erf(hidden * math.sqrt(0.5)))).astype(BF16)

    chunk = jnp.minimum(cc, n_chunks - 1)
    zero = jnp.zeros(r2_ref.shape[1:], BF16)
    for part in range(keys_per_step):
        c = chunk * keys_per_step + part
        gt = None
        for h in range(PEER_HEADS):
            n1 = n1_ref[h, pl.ds(c, 1), :].astype(BF16)
            w1 = w1_ref[h, pl.ds(c, 1), :].astype(BF16)
            term = jnp.where(r2_ref[h] < n1, e2_ref[h] * w1, zero)
            gt = term if gt is None else gt + term
        gate_ref[cur, :, part * N_KEYS:(part + 1) * N_KEYS] = gt.astype(F32).T

    @pl.when(cc == n_chunks)
    def _():
        o_ref[...] = _layer_norm(DEEPNORM_ALPHA * x_ref[...] + acc_ref[...], g_ref[...], b_ref[...])


def _peer_dense(xb, x, ut, v, tables, gain, bias, tt, keys_per_step):
    seq, d = x.shape
    te = keys_per_step * N_KEYS
    n_chunks = v.shape[0] // te
    tab = pl.BlockSpec((PEER_HEADS, N_KEYS, tt), lambda i, c: (0, 0, i))
    row = lambda i, c: (i, 0)
    fixed = lambda i, c: (0, 0)
    kern = functools.partial(_peer_kernel, keys_per_step=keys_per_step, n_chunks=n_chunks)
    return pl.pallas_call(
        kern,
        grid=(seq // tt, n_chunks + 1),
        in_specs=[pl.BlockSpec((tt, d), row),
                  pl.BlockSpec((d, te), lambda i, c: (0, jnp.minimum(c, n_chunks - 1))),
                  pl.BlockSpec((te, d), lambda i, c: (jnp.maximum(c - 1, 0), 0)),
                  tab, tab, tab, tab,
                  pl.BlockSpec((tt, d), row),
                  pl.BlockSpec((1, d), fixed),
                  pl.BlockSpec((1, d), fixed)],
        out_specs=pl.BlockSpec((tt, d), row),
        out_shape=jax.ShapeDtypeStruct((seq, d), F32),
        scratch_shapes=[pltpu.VMEM((tt, d), F32), pltpu.VMEM((2, tt, te), BF16),
                        pltpu.VMEM((2, tt, te), F32)],
        compiler_params=_params("parallel", "arbitrary"),
        name="peer_dense",
    )(xb, ut, v, *tables, x, gain, bias)


def _tile(n, want):
    return min(n, want)


def kernel(x, mem, w_in, sb_norm_gain, df_lambda, df_subln_gain, w_o, ln1_gain, ln1_bias, w_mq, w_mkv, w_mo, ln2_gain, ln2_bias, w_pq, peer_sub_keys, peer_u, peer_v, ln3_gain, ln3_bias):
    b, seq, d = x.shape
    assert b == 1 and w_in.shape[0] == DEPTH
    x2d = x.reshape(seq, d)
    mem2d = mem.reshape(mem.shape[1], d).astype(BF16)
    for l in range(DEPTH):
        lambda_init = 0.8 - 0.6 * math.exp(-0.3 * l)
        row1 = lambda a: a.reshape(1, -1)
        xb = x2d.astype(BF16)
        proj = _inproj(xb, w_in[l].astype(BF16), _tile(seq, 512), 1024,
                       rope_cols=(3 * SB_HEADS * HEAD_DIM, 5 * SB_HEADS * HEAD_DIM))
        sb_o = _sb_attention(proj, sb_norm_gain[l], _tile(seq, 256))
        tq = _tile(seq, 256)
        w_v = w_in[l][:, 5 * SB_HEADS * HEAD_DIM:]
        vt = _vt_proj(xb, w_v.T.astype(BF16), tq)
        df_o = _diff_attention(proj, vt, df_lambda[l], row1(df_subln_gain[l]), tq, lambda_init)
        x1, x1b = _proj_ln([sb_o, df_o], w_o[l].astype(BF16), x2d, row1(ln1_gain[l]), row1(ln1_bias[l]),
                           _tile(seq, 256))
        q = _matmul(x1b, w_mq[l].astype(BF16), _tile(seq, 512), 1024)
        kv = _matmul(mem2d, w_mkv[l].astype(BF16), mem2d.shape[0], 1024)
        xa = _xattn(q, kv, _tile(seq, 256))
        x2, x2b = _proj_ln([xa], w_mo[l].astype(BF16), x1, row1(ln2_gain[l]), row1(ln2_bias[l]),
                           _tile(seq, 256))
        pq = _matmul(x2b, w_pq[l].astype(BF16), _tile(seq, 512), 1024)
        tables = _peer_route(pq, peer_sub_keys[l].astype(BF16), _tile(seq, 256))
        x2d = _peer_dense(x2b, x2, peer_u[l].T.astype(BF16), peer_v[l].astype(BF16), tables,
                          row1(ln3_gain[l]), row1(ln3_bias[l]), _tile(seq, 256), 8)
    return x2d.reshape(b, seq, d)
```

```python
import functools
import math

import jax
import jax.numpy as jnp
import numpy as np
from jax import lax
from jax.experimental import pallas as pl
from jax.experimental.pallas import tpu as pltpu

F32 = jnp.float32
BF16 = jnp.bfloat16

LANES = 128
MXU_WIDTH = 256
VMEM_LIMIT_BYTES = 56 * 1024 * 1024

DEPTH = 1
SB_HEADS = 8
DIFF_HEADS = 8
HEAD_DIM = 128
DIFF_QK_DIM = 64
ROPE_DIM = 16
ROPE_THETA = 500000.0
MEM_HEADS = 4
PEER_HEADS = 8
N_KEYS = 128
PEER_TOPK = 16
LN_EPS = 1e-5
RMS_EPS = 1e-6
DEEPNORM_ALPHA = (2 * DEPTH) ** 0.25
NEG_INF = float("-inf")
SB_DEAD_LOG = -105.0
SUM_ROWS = 16


def _dot(a, b):
    return jnp.dot(a, b, preferred_element_type=F32)


def _dot_nt(a, b):
    return lax.dot_general(a, b, (((1,), (1,)), ((), ())), preferred_element_type=F32)


def _params(*semantics):
    return pltpu.CompilerParams(dimension_semantics=semantics, vmem_limit_bytes=VMEM_LIMIT_BYTES)


def _matmul_kernel(a_ref, b_ref, o_ref):
    o_ref[...] = _dot(a_ref[...], b_ref[...]).astype(o_ref.dtype)


def _matmul(a, b, tm, tn, out_dtype=BF16):
    m, k = a.shape
    n = b.shape[1]
    return pl.pallas_call(
        _matmul_kernel,
        grid=(m // tm, n // tn),
        in_specs=[pl.BlockSpec((tm, k), lambda i, j: (i, 0)),
                  pl.BlockSpec((k, tn), lambda i, j: (0, j))],
        out_specs=pl.BlockSpec((tm, tn), lambda i, j: (i, j)),
        out_shape=jax.ShapeDtypeStruct((m, n), out_dtype),
        compiler_params=_params("parallel", "parallel"),
        name="matmul",
    )(a, b)


def _inproj_kernel(a_ref, b_ref, c_ref, s1_ref, s2_ref, o_ref, *, rope_lo, rope_hi):
    j = pl.program_id(1)
    acc = _dot(a_ref[...], b_ref[...])
    o_ref[...] = acc.astype(o_ref.dtype)

    @pl.when(jnp.logical_and(j >= rope_lo, j < rope_hi))
    def _():
        cos, sin_up, sin_dn = c_ref[...], s1_ref[...], s2_ref[...]
        for cc in range(acc.shape[1] // LANES):
            t = acc[:, cc * LANES:(cc + 1) * LANES]
            up = pltpu.roll(t, LANES - ROPE_DIM // 2, axis=1)
            dn = pltpu.roll(t, ROPE_DIM // 2, axis=1)
            o_ref[:, cc * LANES:(cc + 1) * LANES] = (t * cos + up * sin_up + dn * sin_dn).astype(o_ref.dtype)


def _rope_tables(seq):
    half = ROPE_DIM // 2
    inv_freq = jnp.power(ROPE_THETA, -jnp.arange(half, dtype=F32) * 2.0 / ROPE_DIM)
    ang = jnp.arange(seq).astype(F32)[:, None] * inv_freq[None, :]
    cos, sin = jnp.cos(ang), jnp.sin(ang)
    ones = jnp.ones((seq, DIFF_QK_DIM - ROPE_DIM), F32)
    zeros = jnp.zeros((seq, DIFF_QK_DIM - ROPE_DIM), F32)
    zh = jnp.zeros((seq, half), F32)
    c64 = jnp.concatenate([cos, cos, ones], axis=1)
    up64 = jnp.concatenate([-sin, zh, zeros], axis=1)
    dn64 = jnp.concatenate([zh, sin, zeros], axis=1)
    rep = LANES // DIFF_QK_DIM
    return jnp.tile(c64, (1, rep)), jnp.tile(up64, (1, rep)), jnp.tile(dn64, (1, rep))


def _inproj(xb, w_in, tm, tn, rope_cols):
    seq, k = xb.shape
    n = w_in.shape[1]
    cos, sin_up, sin_dn = _rope_tables(seq)
    tab = pl.BlockSpec((tm, LANES), lambda i, j: (i, 0))
    kern = functools.partial(_inproj_kernel, rope_lo=rope_cols[0] // tn, rope_hi=rope_cols[1] // tn)
    return pl.pallas_call(
        kern,
        grid=(seq // tm, n // tn),
        in_specs=[pl.BlockSpec((tm, k), lambda i, j: (i, 0)),
                  pl.BlockSpec((k, tn), lambda i, j: (0, j)),
                  tab, tab, tab],
        out_specs=pl.BlockSpec((tm, tn), lambda i, j: (i, j)),
        out_shape=jax.ShapeDtypeStruct((seq, n), BF16),
        compiler_params=_params("parallel", "parallel"),
        name="inproj_rope",
    )(xb, w_in, cos, sin_up, sin_dn)


def _sb_kernel(q_ref, k_ref, v_ref, g_ref, o_ref, *, tq, scale):
    h = pl.program_id(0)
    i = pl.program_id(1)
    q = q_ref[...]
    row = lax.broadcasted_iota(jnp.int32, (tq, tq), 0)
    col = lax.broadcasted_iota(jnp.int32, (tq, tq), 1)
    later = (row > col).astype(BF16)
    before = col < row

    def block(kb, acc, c, masked):
        start = pl.multiple_of(kb * tq, tq)
        k = k_ref[pl.ds(start, tq), :]
        v = v_ref[pl.ds(start, tq), :]
        z = _dot_nt(q, k) * scale
        sp = jnp.log(1.0 + jnp.exp(-jnp.abs(z)))
        log_beta = jnp.minimum(z, 0.0) - sp
        log_keep = -jnp.maximum(z, 0.0) - sp
        if masked:
            log_keep = jnp.where(before, log_keep, 0.0)
        hi = log_keep.astype(BF16)
        lo = (log_keep - hi.astype(F32)).astype(BF16)
        stick = _dot(hi, later) + _dot(lo, later)
        w = jnp.exp(log_beta + stick + c)
        if masked:
            w = jnp.where(before, w, 0.0)
        acc = acc + _dot(w.astype(BF16), v)
        c = c + jnp.sum(log_keep, axis=1, keepdims=True)
        return acc, c

    acc0 = jnp.zeros((tq, HEAD_DIM), F32)
    c0 = jnp.zeros((tq, 1), F32)
    acc, c = block(i, acc0, c0, True)

    def live(carry):
        jj, _, _, cmax = carry
        return jnp.logical_and(jj < i, cmax > SB_DEAD_LOG)

    def older(carry):
        jj, acc, c, _ = carry
        acc, c = block(i - 1 - jj, acc, c, False)
        return jj + 1, acc, c, jnp.max(c)

    _, acc, c, _ = lax.while_loop(live, older, (jnp.int32(0), acc, c, jnp.max(c)))
    g = g_ref[pl.ds(h, 1), :]
    ms = jnp.mean(acc * acc, axis=1, keepdims=True)
    o_ref[...] = (acc * lax.rsqrt(ms + RMS_EPS) * g).astype(o_ref.dtype)


def _sb_attention(proj, gain, tq):
    seq = proj.shape[0]
    kern = functools.partial(_sb_kernel, tq=tq, scale=1.0 / math.sqrt(HEAD_DIM))
    return pl.pallas_call(
        kern,
        grid=(SB_HEADS, seq // tq),
        in_specs=[pl.BlockSpec((tq, HEAD_DIM), lambda h, i: (i, h)),
                  pl.BlockSpec((seq, HEAD_DIM), lambda h, i: (0, SB_HEADS + h)),
                  pl.BlockSpec((seq, HEAD_DIM), lambda h, i: (0, 2 * SB_HEADS + h)),
                  pl.BlockSpec((SB_HEADS, HEAD_DIM), lambda h, i: (0, 0))],
        out_specs=pl.BlockSpec((tq, HEAD_DIM), lambda h, i: (i, h)),
        out_shape=jax.ShapeDtypeStruct((seq, SB_HEADS * HEAD_DIM), BF16),
        compiler_params=_params("parallel", "arbitrary"),
        name="stickbreak_attn",
    )(proj, proj, proj, gain)


def _vt_proj_kernel(w_ref, x_ref, o_ref):
    o_ref[0] = _dot_nt(w_ref[...], x_ref[...]).astype(o_ref.dtype)


def _vt_proj(xb, w_vt, tk):
    seq, k = xb.shape
    n = w_vt.shape[0]
    return pl.pallas_call(
        _vt_proj_kernel,
        grid=(seq // tk,),
        in_specs=[pl.BlockSpec((n, k), lambda j: (0, 0)),
                  pl.BlockSpec((tk, k), lambda j: (j, 0))],
        out_specs=pl.BlockSpec((1, n, tk), lambda j: (j, 0, 0)),
        out_shape=jax.ShapeDtypeStruct((seq // tk, n, tk), BF16),
        compiler_params=_params("parallel"),
        name="vt_proj",
    )(w_vt, xb)


def _diff_kernel(q_ref, k_ref, vt_ref, lam_ref, g_ref, o_ref, s0_ref, s1_ref, p0_ref, p1_ref, acc_ref,
                 *, tq, lambda_init):
    i = pl.program_id(1)
    s_refs, p_refs = (s0_ref, s1_ref), (p0_ref, p1_ref)
    q = q_ref[...] * jnp.asarray(1.0 / math.sqrt(DIFF_QK_DIM), BF16)
    lane = lax.broadcasted_iota(jnp.int32, (tq, HEAD_DIM), 1)
    zero = jnp.zeros_like(q)
    qmaps = (jnp.where(lane < DIFF_QK_DIM, q, zero),
             jnp.where(lane >= DIFF_QK_DIM, q, zero))
    ones = jnp.ones((SUM_ROWS, tq), BF16)

    def scores(kb, mp):
        k = k_ref[pl.ds(pl.multiple_of(kb * tq, tq), tq), :]
        return _dot_nt(k, qmaps[mp])

    def softmax_step(s, m, slot, mp):
        m_new = jnp.maximum(m, jnp.max(s, axis=0, keepdims=True))
        p_refs[slot][mp] = jnp.exp(s - m_new).astype(BF16)
        return m_new, jnp.exp(m - m_new)

    def accumulate(kb, alpha, slot, mp):
        v_aug = jnp.concatenate([vt_ref[kb], ones], axis=0)
        acc_ref[mp] = alpha * acc_ref[mp] + _dot(v_aug, p_refs[slot][mp])

    key = lax.broadcasted_iota(jnp.int32, (tq, tq), 0)
    qry = lax.broadcasted_iota(jnp.int32, (tq, tq), 1)
    causal = key <= qry
    m0 = jnp.full((1, tq), NEG_INF, F32)
    acc_ref[...] = jnp.zeros_like(acc_ref)
    stats = []
    for mp in range(2):
        s_refs[1][mp] = scores(0, mp)
        stats.extend(softmax_step(jnp.where(causal, scores(i, mp), NEG_INF), m0, 0, mp))
    stats = tuple(stats)

    def step(t, carry, cur):
        nxt = 1 - cur
        vb = jnp.where(t == 0, i, t - 1)
        out = []
        for mp in range(2):
            m, alpha = carry[2 * mp], carry[2 * mp + 1]
            s_refs[cur][mp] = scores(jnp.minimum(t + 1, i - 1), mp)
            accumulate(vb, alpha, cur, mp)
            out.extend(softmax_step(s_refs[nxt][mp], m, nxt, mp))
        return tuple(out)

    stats = lax.fori_loop(0, i // 2, lambda u, c: step(2 * u + 1, step(2 * u, c, 0), 1), stats)
    odd = i % 2 == 1
    stats = lax.cond(odd, lambda c: step(i - 1, c, 0), lambda c: c, stats)
    last = jnp.maximum(i - 1, 0)
    for slot in range(2):
        @pl.when((i % 2) == slot)
        def _():
            for mp in range(2):
                accumulate(last, stats[2 * mp + 1], slot, mp)

    lf = lam_ref[...]
    lam = (jnp.exp(jnp.sum(lf[0:1, :] * lf[1:2, :], axis=1, keepdims=True))
           - jnp.exp(jnp.sum(lf[2:3, :] * lf[3:4, :], axis=1, keepdims=True)) + lambda_init)
    o1 = acc_ref[0, :HEAD_DIM, :] / acc_ref[0, HEAD_DIM:HEAD_DIM + 1, :]
    o2 = acc_ref[1, :HEAD_DIM, :] / acc_ref[1, HEAD_DIM:HEAD_DIM + 1, :]
    d = o1 - lam * o2
    ms = jnp.mean(d * d, axis=0, keepdims=True)
    dn = (d * lax.rsqrt(ms + RMS_EPS)).T
    o_ref[...] = (dn * g_ref[...] * (1.0 - lambda_init)).astype(o_ref.dtype)


def _diff_attention(proj, vt, df_lambda, gain, tq, lambda_init):
    seq = proj.shape[0]
    qcol = 3 * SB_HEADS
    kcol = qcol + DIFF_HEADS
    kern = functools.partial(_diff_kernel, tq=tq, lambda_init=lambda_init)
    return pl.pallas_call(
        kern,
        grid=(DIFF_HEADS, seq // tq),
        in_specs=[pl.BlockSpec((tq, HEAD_DIM), lambda h, i: (i, qcol + h)),
                  pl.BlockSpec((seq, HEAD_DIM), lambda h, i: (0, kcol + h)),
                  pl.BlockSpec((seq // tq, HEAD_DIM, tq), lambda h, i: (0, h, 0)),
                  pl.BlockSpec((4, DIFF_QK_DIM), lambda h, i: (0, 0)),
                  pl.BlockSpec((1, HEAD_DIM), lambda h, i: (0, 0))],
        out_specs=pl.BlockSpec((tq, HEAD_DIM), lambda h, i: (i, h)),
        out_shape=jax.ShapeDtypeStruct((seq, DIFF_HEADS * HEAD_DIM), BF16),
        scratch_shapes=[pltpu.VMEM((2, tq, tq), F32), pltpu.VMEM((2, tq, tq), F32),
                        pltpu.VMEM((2, tq, tq), BF16), pltpu.VMEM((2, tq, tq), BF16),
                        pltpu.VMEM((2, HEAD_DIM + SUM_ROWS, tq), F32)],
        compiler_params=_params("parallel", "arbitrary"),
        name="diff_attn",
    )(proj, proj, vt, df_lambda, gain)


def _layer_norm(r, g, b):
    mu = jnp.mean(r, axis=1, keepdims=True)
    d = r - mu
    var = jnp.mean(d * d, axis=1, keepdims=True)
    return d * lax.rsqrt(var + LN_EPS) * g + b


def _proj_ln_kernel(*refs, n_parts):
    a_refs = refs[:n_parts]
    w_ref, x_ref, g_ref, b_ref, o_ref, ob_ref = refs[n_parts:]
    y = None
    off = 0
    for a_ref in a_refs:
        kk = a_ref.shape[1]
        part = _dot(a_ref[...], w_ref[off:off + kk, :])
        y = part if y is None else y + part
        off += kk
    out = _layer_norm(DEEPNORM_ALPHA * x_ref[...] + y, g_ref[...], b_ref[...])
    o_ref[...] = out
    ob_ref[...] = out.astype(BF16)


def _proj_ln(parts, w, x, gain, bias, tm):
    seq, d = x.shape
    kern = functools.partial(_proj_ln_kernel, n_parts=len(parts))
    row = lambda i: (i, 0)
    fixed = lambda i: (0, 0)
    return pl.pallas_call(
        kern,
        grid=(seq // tm,),
        in_specs=[pl.BlockSpec((tm, p.shape[1]), row) for p in parts] + [
            pl.BlockSpec(w.shape, fixed),
            pl.BlockSpec((tm, d), row),
            pl.BlockSpec((1, d), fixed),
            pl.BlockSpec((1, d), fixed)],
        out_specs=[pl.BlockSpec((tm, d), row), pl.BlockSpec((tm, d), row)],
        out_shape=[jax.ShapeDtypeStruct((seq, d), F32), jax.ShapeDtypeStruct((seq, d), BF16)],
        compiler_params=_params("parallel"),
        name="proj_residual_ln",
    )(*parts, w, x, gain, bias)


def _xattn_kernel(q_ref, kv_ref, o_ref, *, d_model, head_dim):
    scale = 1.0 / math.sqrt(head_dim)
    for hh in range(d_model // head_dim):
        lo = hh * head_dim
        q = q_ref[:, lo:lo + head_dim]
        k = kv_ref[:, lo:lo + head_dim]
        v = kv_ref[:, d_model + lo:d_model + lo + head_dim]
        s = _dot_nt(q, k) * scale
        m = jnp.max(s, axis=1, keepdims=True)
        p = jnp.exp(s - m)
        p = p / jnp.sum(p, axis=1, keepdims=True)
        o_ref[:, lo:lo + head_dim] = _dot(p.astype(BF16), v).astype(o_ref.dtype)


def _xattn(q, kv, tm):
    seq, d = q.shape
    kern = functools.partial(_xattn_kernel, d_model=d, head_dim=d // MEM_HEADS)
    return pl.pallas_call(
        kern,
        grid=(seq // tm,),
        in_specs=[pl.BlockSpec((tm, d), lambda i: (i, 0)),
                  pl.BlockSpec(kv.shape, lambda i: (0, 0))],
        out_specs=pl.BlockSpec((tm, d), lambda i: (i, 0)),
        out_shape=jax.ShapeDtypeStruct((seq, d), BF16),
        compiler_params=_params("parallel"),
        name="memory_xattn",
    )(q, kv)


def _top16(s, vals_ref):
    unranked = jnp.full(s.shape, float(PEER_TOPK), F32)

    def step(k, carry, exact_ties):
        work, rank = carry
        m = jnp.max(work, axis=0, keepdims=True)
        sel = work == m
        if exact_ties:
            iota = lax.broadcasted_iota(jnp.int32, s.shape, 0).astype(F32)
            sel = iota == jnp.min(jnp.where(sel, iota, float(s.shape[0])), axis=0, keepdims=True)
        vals_ref[pl.ds(k, 1), :] = m
        return jnp.where(sel, NEG_INF, work), jnp.where(sel, lax.convert_element_type(k, F32), rank)

    def run(exact_ties):
        return lax.fori_loop(0, PEER_TOPK, functools.partial(step, exact_ties=exact_ties), (s, unranked))[1]

    rank = run(False)
    ranked = jnp.sum(jnp.where(rank < float(PEER_TOPK), 1.0, 0.0), axis=0, keepdims=True)
    return lax.cond(jnp.max(ranked) > float(PEER_TOPK), lambda _: run(True), lambda r: r, rank)


CAND_KEEP = tuple(PEER_TOPK // (k1 + 1) for k1 in range(PEER_TOPK))
CAND_ROWS = -(-sum(CAND_KEEP) // 8) * 8


def _route_kernel(pq_ref, keys_ref, e2_ref, r2_ref, w1_ref, n1_ref, v1_ref, v2_ref, vc_ref, cand_ref):
    q = pq_ref[...]
    s1 = _dot_nt(keys_ref[0], q[:, :N_KEYS])
    s2 = _dot_nt(keys_ref[1], q[:, N_KEYS:])
    rank1 = _top16(s1, v1_ref)
    rank2 = _top16(s2, v2_ref)
    cand_ref[...] = jnp.full(cand_ref.shape, NEG_INF, F32)
    row = lax.broadcasted_iota(jnp.int32, cand_ref.shape, 0)
    seg = jnp.zeros(cand_ref.shape, F32)
    off = 0
    for k1, keep in enumerate(CAND_KEEP):
        cand_ref[off:off + keep, :] = v1_ref[k1:k1 + 1, :] + v2_ref[0:keep, :]
        off += keep
        seg = seg + jnp.where(row >= off, 1.0, 0.0)
    crank = _top16(cand_ref[...], vc_ref)
    top = vc_ref[...]
    z = jnp.sum(jnp.exp(top - top[0:1, :]), axis=0, keepdims=True)
    chosen = jnp.where(crank < float(PEER_TOPK), 1.0, 0.0)
    n1 = jnp.zeros(s1.shape, F32)
    for k1 in range(PEER_TOPK):
        cnt = jnp.sum(jnp.where(seg == float(k1), chosen, 0.0), axis=0, keepdims=True)
        n1 = jnp.where(rank1 == float(k1), cnt, n1)
    v2 = v2_ref[...]
    e2_ref[0] = jnp.exp(s2 - v2[0:1, :]).astype(e2_ref.dtype)
    r2_ref[0] = rank2.astype(r2_ref.dtype)
    w1_ref[0] = jnp.exp(s1 - v1_ref[0:1, :]) / z
    n1_ref[0] = n1


def _peer_route(pq, keys, tt):
    seq = pq.shape[0]
    qd = 2 * N_KEYS
    tab = pl.BlockSpec((1, N_KEYS, tt), lambda i, h: (h, 0, i))
    shape = jax.ShapeDtypeStruct((PEER_HEADS, N_KEYS, seq), F32)
    packed = jax.ShapeDtypeStruct((PEER_HEADS, N_KEYS, seq), BF16)
    return pl.pallas_call(
        _route_kernel,
        grid=(seq // tt, PEER_HEADS),
        in_specs=[pl.BlockSpec((tt, qd), lambda i, h: (i, h)),
                  pl.BlockSpec(keys.shape, lambda i, h: (0, 0, 0))],
        out_specs=[tab, tab, tab, tab],
        out_shape=[packed, packed, shape, shape],
        scratch_shapes=[pltpu.VMEM((PEER_TOPK, tt), F32),
                        pltpu.VMEM((PEER_TOPK, tt), F32),
                        pltpu.VMEM((PEER_TOPK, tt), F32),
                        pltpu.VMEM((CAND_ROWS, tt), F32)],
        compiler_params=_params("parallel", "parallel"),
        name="peer_route",
    )(pq, keys)


def _peer_kernel(xb_ref, u_ref, v_ref, e2_ref, r2_ref, w1_ref, n1_ref, x_ref, g_ref, b_ref,
                 o_ref, a_ref, gate_ref, *, keys_per_step, n_chunks):
    cc = pl.program_id(1)
    cur, prev = cc % 2, (cc + 1) % 2

    @pl.when(cc == 0)
    def _():
        o_ref[...] = jnp.zeros_like(o_ref)
        a_ref[...] = jnp.zeros_like(a_ref)
        gate_ref[...] = jnp.zeros_like(gate_ref)

    gated = (a_ref[prev].astype(F32) * gate_ref[prev]).astype(BF16)
    o_ref[...] += _dot(gated, v_ref[...])

    hidden = _dot(xb_ref[...], u_ref[...])
    a_ref[cur] = (0.5 * hidden * (1.0 + lax.erf(hidden * math.sqrt(0.5)))).astype(BF16)

    chunk = jnp.minimum(cc, n_chunks - 1)
    zero = jnp.zeros(r2_ref.shape[1:], BF16)
    for part in range(keys_per_step):
        c = chunk * keys_per_step + part
        gt = None
        for h in range(PEER_HEADS):
            n1 = n1_ref[h, pl.ds(c, 1), :].astype(BF16)
            w1 = w1_ref[h, pl.ds(c, 1), :].astype(BF16)
            term = jnp.where(r2_ref[h] < n1, e2_ref[h] * w1, zero)
            gt = term if gt is None else gt + term
        gate_ref[cur, :, part * N_KEYS:(part + 1) * N_KEYS] = gt.astype(F32).T

    @pl.when(cc == n_chunks)
    def _():
        o_ref[...] = _layer_norm(DEEPNORM_ALPHA * x_ref[...] + o_ref[...], g_ref[...], b_ref[...])


def _peer_dense(xb, x, ut, v, tables, gain, bias, tt, keys_per_step):
    seq, d = x.shape
    te = keys_per_step * N_KEYS
    n_chunks = v.shape[0] // te
    tab = pl.BlockSpec((PEER_HEADS, N_KEYS, tt), lambda i, c: (0, 0, i))
    row = lambda i, c: (i, 0)
    fixed = lambda i, c: (0, 0)
    kern = functools.partial(_peer_kernel, keys_per_step=keys_per_step, n_chunks=n_chunks)
    return pl.pallas_call(
        kern,
        grid=(seq // tt, n_chunks + 1),
        in_specs=[pl.BlockSpec((tt, d), row),
                  pl.BlockSpec((d, te), lambda i, c: (0, jnp.minimum(c, n_chunks - 1))),
                  pl.BlockSpec((te, d), lambda i, c: (jnp.maximum(c - 1, 0), 0)),
                  tab, tab, tab, tab,
                  pl.BlockSpec((tt, d), row),
                  pl.BlockSpec((1, d), fixed),
                  pl.BlockSpec((1, d), fixed)],
        out_specs=pl.BlockSpec((tt, d), row),
        out_shape=jax.ShapeDtypeStruct((seq, d), F32),
        scratch_shapes=[pltpu.VMEM((2, tt, te), BF16), pltpu.VMEM((2, tt, te), F32)],
        compiler_params=_params("parallel", "arbitrary"),
        name="peer_dense",
    )(xb, ut, v, *tables, x, gain, bias)


def _tile(n, want):
    return min(n, want)


def kernel(x, mem, w_in, sb_norm_gain, df_lambda, df_subln_gain, w_o, ln1_gain, ln1_bias, w_mq, w_mkv, w_mo, ln2_gain, ln2_bias, w_pq, peer_sub_keys, peer_u, peer_v, ln3_gain, ln3_bias):
    b, seq, d = x.shape
    assert b == 1 and w_in.shape[0] == DEPTH
    x2d = x.reshape(seq, d)
    mem2d = mem.reshape(mem.shape[1], d).astype(BF16)
    for l in range(DEPTH):
        lambda_init = 0.8 - 0.6 * math.exp(-0.3 * l)
        row1 = lambda a: a.reshape(1, -1)
        xb = x2d.astype(BF16)
        proj = _inproj(xb, w_in[l].astype(BF16), _tile(seq, 512), 1024,
                       rope_cols=(3 * SB_HEADS * HEAD_DIM, 5 * SB_HEADS * HEAD_DIM))
        sb_o = _sb_attention(proj, sb_norm_gain[l], _tile(seq, 256))
        tq = _tile(seq, 256)
        w_v = w_in[l][:, 5 * SB_HEADS * HEAD_DIM:]
        vt = _vt_proj(xb, w_v.T.astype(BF16), tq)
        df_o = _diff_attention(proj, vt, df_lambda[l], row1(df_subln_gain[l]), tq, lambda_init)
        x1, x1b = _proj_ln([sb_o, df_o], w_o[l].astype(BF16), x2d, row1(ln1_gain[l]), row1(ln1_bias[l]),
                           _tile(seq, 256))
        q = _matmul(x1b, w_mq[l].astype(BF16), _tile(seq, 512), 1024)
        kv = _matmul(mem2d, w_mkv[l].astype(BF16), mem2d.shape[0], 1024)
        xa = _xattn(q, kv, _tile(seq, 256))
        x2, x2b = _proj_ln([xa], w_mo[l].astype(BF16), x1, row1(ln2_gain[l]), row1(ln2_bias[l]),
                           _tile(seq, 256))
        pq = _matmul(x2b, w_pq[l].astype(BF16), _tile(seq, 512), 1024)
        tables = _peer_route(pq, peer_sub_keys[l].astype(BF16), _tile(seq, 256))
        x2d = _peer_dense(x2b, x2, peer_u[l].T.astype(BF16), peer_v[l].astype(BF16), tables,
                          row1(ln3_gain[l]), row1(ln3_bias[l]), _tile(seq, 512), 4)
    return x2d.reshape(b, seq, d)
```

```python
import functools
import math

import jax
import jax.numpy as jnp
import numpy as np
from jax import lax
from jax.experimental import pallas as pl
from jax.experimental.pallas import tpu as pltpu

F32 = jnp.float32
BF16 = jnp.bfloat16

LANES = 128
BF16_SUBLANES = 16
MXU_WIDTH = 256
VMEM_LIMIT_BYTES = 56 * 1024 * 1024

DEPTH = 1
SB_HEADS = 8
DIFF_HEADS = 8
HEAD_DIM = 128
DIFF_QK_DIM = 64
ROPE_DIM = 16
ROPE_THETA = 500000.0
MEM_HEADS = 4
PEER_HEADS = 8
N_KEYS = 128
PEER_TOPK = 16
LN_EPS = 1e-5
RMS_EPS = 1e-6
DEEPNORM_ALPHA = (2 * DEPTH) ** 0.25
NEG_INF = float("-inf")
SB_DEAD_LOG = -105.0
SUM_ROWS = 16


def _dot(a, b):
    return jnp.dot(a, b, preferred_element_type=F32)


def _dot_nt(a, b):
    return lax.dot_general(a, b, (((1,), (1,)), ((), ())), preferred_element_type=F32)


def _params(*semantics):
    return pltpu.CompilerParams(dimension_semantics=semantics, vmem_limit_bytes=VMEM_LIMIT_BYTES)


def _matmul_kernel(a_ref, b_ref, o_ref):
    o_ref[...] = _dot(a_ref[...], b_ref[...]).astype(o_ref.dtype)


def _matmul(a, b, tm, tn, out_dtype=BF16):
    m, k = a.shape
    n = b.shape[1]
    return pl.pallas_call(
        _matmul_kernel,
        grid=(m // tm, n // tn),
        in_specs=[pl.BlockSpec((tm, k), lambda i, j: (i, 0)),
                  pl.BlockSpec((k, tn), lambda i, j: (0, j))],
        out_specs=pl.BlockSpec((tm, tn), lambda i, j: (i, j)),
        out_shape=jax.ShapeDtypeStruct((m, n), out_dtype),
        compiler_params=_params("parallel", "parallel"),
        name="matmul",
    )(a, b)


def _inproj_kernel(a_ref, b_ref, c_ref, s1_ref, s2_ref, o_ref, *, rope_lo, rope_hi):
    j = pl.program_id(1)
    acc = _dot(a_ref[...], b_ref[...])
    o_ref[...] = acc.astype(o_ref.dtype)

    @pl.when(jnp.logical_and(j >= rope_lo, j < rope_hi))
    def _():
        cos, sin_up, sin_dn = c_ref[...], s1_ref[...], s2_ref[...]
        for cc in range(acc.shape[1] // LANES):
            t = acc[:, cc * LANES:(cc + 1) * LANES]
            up = pltpu.roll(t, LANES - ROPE_DIM // 2, axis=1)
            dn = pltpu.roll(t, ROPE_DIM // 2, axis=1)
            o_ref[:, cc * LANES:(cc + 1) * LANES] = (t * cos + up * sin_up + dn * sin_dn).astype(o_ref.dtype)


def _rope_tables(seq):
    half = ROPE_DIM // 2
    inv_freq = jnp.power(ROPE_THETA, -jnp.arange(half, dtype=F32) * 2.0 / ROPE_DIM)
    ang = jnp.arange(seq).astype(F32)[:, None] * inv_freq[None, :]
    cos, sin = jnp.cos(ang), jnp.sin(ang)
    ones = jnp.ones((seq, DIFF_QK_DIM - ROPE_DIM), F32)
    zeros = jnp.zeros((seq, DIFF_QK_DIM - ROPE_DIM), F32)
    zh = jnp.zeros((seq, half), F32)
    c64 = jnp.concatenate([cos, cos, ones], axis=1)
    up64 = jnp.concatenate([-sin, zh, zeros], axis=1)
    dn64 = jnp.concatenate([zh, sin, zeros], axis=1)
    rep = LANES // DIFF_QK_DIM
    return jnp.tile(c64, (1, rep)), jnp.tile(up64, (1, rep)), jnp.tile(dn64, (1, rep))


def _inproj(xb, w_in, tm, tn, rope_cols):
    seq, k = xb.shape
    n = w_in.shape[1]
    cos, sin_up, sin_dn = _rope_tables(seq)
    tab = pl.BlockSpec((tm, LANES), lambda i, j: (i, 0))
    kern = functools.partial(_inproj_kernel, rope_lo=rope_cols[0] // tn, rope_hi=rope_cols[1] // tn)
    return pl.pallas_call(
        kern,
        grid=(seq // tm, n // tn),
        in_specs=[pl.BlockSpec((tm, k), lambda i, j: (i, 0)),
                  pl.BlockSpec((k, tn), lambda i, j: (0, j)),
                  tab, tab, tab],
        out_specs=pl.BlockSpec((tm, tn), lambda i, j: (i, j)),
        out_shape=jax.ShapeDtypeStruct((seq, n), BF16),
        compiler_params=_params("parallel", "parallel"),
        name="inproj_rope",
    )(xb, w_in, cos, sin_up, sin_dn)


def _sb_kernel(q_ref, k_ref, v_ref, g_ref, o_ref, *, tq, scale):
    h = pl.program_id(0)
    i = pl.program_id(1)
    q = q_ref[...]
    row = lax.broadcasted_iota(jnp.int32, (tq, tq), 0)
    col = lax.broadcasted_iota(jnp.int32, (tq, tq), 1)
    later = (row > col).astype(BF16)
    before = col < row

    def block_terms(kb, masked):
        start = pl.multiple_of(kb * tq, tq)
        k = k_ref[pl.ds(start, tq), :]
        z = _dot_nt(q, k) * scale
        sp = jnp.log(1.0 + jnp.exp(-jnp.abs(z)))
        log_beta = jnp.minimum(z, 0.0) - sp
        log_keep = -jnp.maximum(z, 0.0) - sp
        if masked:
            log_keep = jnp.where(before, log_keep, 0.0)
        hi = log_keep.astype(BF16)
        lo = (log_keep - hi.astype(F32)).astype(BF16)
        stick = _dot(hi, later) + _dot(lo, later)
        return log_beta + stick, jnp.sum(log_keep, axis=1, keepdims=True), v_ref[pl.ds(start, tq), :]

    def block_apply(terms, acc, c, masked, valid=None):
        log_w, keep_sum, v = terms
        w = jnp.exp(log_w + c)
        if masked:
            w = jnp.where(before, w, 0.0)
        if valid is not None:
            w = jnp.where(valid, w, 0.0)
            keep_sum = jnp.where(valid, keep_sum, 0.0)
        return acc + _dot(w.astype(BF16), v), c + keep_sum

    acc0 = jnp.zeros((tq, HEAD_DIM), F32)
    c0 = jnp.zeros((tq, 1), F32)
    diag = block_terms(i, True)
    prev = block_terms(jnp.maximum(i - 1, 0), False)
    acc, c = block_apply(diag, acc0, c0, True)
    acc, c = block_apply(prev, acc, c, False, valid=i >= 1)

    def live(carry):
        jj, _, _, cmax = carry
        return jnp.logical_and(jj < i, cmax > SB_DEAD_LOG)

    def older(carry):
        jj, acc, c, _ = carry
        acc, c = block_apply(block_terms(i - 1 - jj, False), acc, c, False)
        return jj + 1, acc, c, jnp.max(c)

    _, acc, c, _ = lax.while_loop(live, older, (jnp.int32(1), acc, c, jnp.max(c)))
    g = g_ref[pl.ds(h, 1), :]
    ms = jnp.mean(acc * acc, axis=1, keepdims=True)
    o_ref[...] = (acc * lax.rsqrt(ms + RMS_EPS) * g).astype(o_ref.dtype)


def _sb_attention(proj, gain, tq):
    seq = proj.shape[0]
    kern = functools.partial(_sb_kernel, tq=tq, scale=1.0 / math.sqrt(HEAD_DIM))
    return pl.pallas_call(
        kern,
        grid=(SB_HEADS, seq // tq),
        in_specs=[pl.BlockSpec((tq, HEAD_DIM), lambda h, i: (i, h)),
                  pl.BlockSpec((seq, HEAD_DIM), lambda h, i: (0, SB_HEADS + h)),
                  pl.BlockSpec((seq, HEAD_DIM), lambda h, i: (0, 2 * SB_HEADS + h)),
                  pl.BlockSpec((SB_HEADS, HEAD_DIM), lambda h, i: (0, 0))],
        out_specs=pl.BlockSpec((tq, HEAD_DIM), lambda h, i: (i, h)),
        out_shape=jax.ShapeDtypeStruct((seq, SB_HEADS * HEAD_DIM), BF16),
        compiler_params=_params("parallel", "arbitrary"),
        name="stickbreak_attn",
    )(proj, proj, proj, gain)


def _vt_proj_kernel(w_ref, x_ref, o_ref):
    o_ref[0] = _dot_nt(w_ref[...], x_ref[...]).astype(o_ref.dtype)


def _vt_proj(xb, w_vt, tk):
    seq, k = xb.shape
    n = w_vt.shape[0]
    return pl.pallas_call(
        _vt_proj_kernel,
        grid=(seq // tk,),
        in_specs=[pl.BlockSpec((n, k), lambda j: (0, 0)),
                  pl.BlockSpec((tk, k), lambda j: (j, 0))],
        out_specs=pl.BlockSpec((1, n, tk), lambda j: (j, 0, 0)),
        out_shape=jax.ShapeDtypeStruct((seq // tk, n, tk), BF16),
        compiler_params=_params("parallel"),
        name="vt_proj",
    )(w_vt, xb)


def _diff_kernel(q_ref, k_ref, vt_ref, lam_ref, g_ref, o_ref, s0_ref, s1_ref, p0_ref, p1_ref, acc_ref,
                 *, tq, lambda_init):
    i = pl.program_id(1)
    s_refs, p_refs = (s0_ref, s1_ref), (p0_ref, p1_ref)
    q = q_ref[...] * jnp.asarray(1.0 / math.sqrt(DIFF_QK_DIM), BF16)
    lane = lax.broadcasted_iota(jnp.int32, (tq, HEAD_DIM), 1)
    zero = jnp.zeros_like(q)
    qmaps = (jnp.where(lane < DIFF_QK_DIM, q, zero),
             jnp.where(lane >= DIFF_QK_DIM, q, zero))
    ones = jnp.ones((SUM_ROWS, tq), BF16)

    def scores(kb, mp):
        k = k_ref[pl.ds(pl.multiple_of(kb * tq, tq), tq), :]
        return _dot_nt(k, qmaps[mp])

    def softmax_step(s, m, slot, mp):
        m_new = jnp.maximum(m, jnp.max(s, axis=0, keepdims=True))
        p_refs[slot][mp] = jnp.exp(s - m_new).astype(BF16)
        return m_new, jnp.exp(m - m_new)

    def accumulate(kb, alpha, slot, mp):
        v_aug = jnp.concatenate([vt_ref[kb], ones], axis=0)
        acc_ref[mp] = alpha * acc_ref[mp] + _dot(v_aug, p_refs[slot][mp])

    key = lax.broadcasted_iota(jnp.int32, (tq, tq), 0)
    qry = lax.broadcasted_iota(jnp.int32, (tq, tq), 1)
    causal = key <= qry
    m0 = jnp.full((1, tq), NEG_INF, F32)
    acc_ref[...] = jnp.zeros_like(acc_ref)
    stats = []
    for mp in range(2):
        s_refs[1][mp] = scores(0, mp)
        stats.extend(softmax_step(jnp.where(causal, scores(i, mp), NEG_INF), m0, 0, mp))
    stats = tuple(stats)

    def step(t, carry, cur):
        nxt = 1 - cur
        vb = jnp.where(t == 0, i, t - 1)
        out = []
        for mp in range(2):
            m, alpha = carry[2 * mp], carry[2 * mp + 1]
            s_refs[cur][mp] = scores(jnp.minimum(t + 1, i - 1), mp)
            accumulate(vb, alpha, cur, mp)
            out.extend(softmax_step(s_refs[nxt][mp], m, nxt, mp))
        return tuple(out)

    stats = lax.fori_loop(0, i // 2, lambda u, c: step(2 * u + 1, step(2 * u, c, 0), 1), stats)
    odd = i % 2 == 1
    stats = lax.cond(odd, lambda c: step(i - 1, c, 0), lambda c: c, stats)
    last = jnp.maximum(i - 1, 0)
    for slot in range(2):
        @pl.when((i % 2) == slot)
        def _():
            for mp in range(2):
                accumulate(last, stats[2 * mp + 1], slot, mp)

    lf = lam_ref[...]
    lam = (jnp.exp(jnp.sum(lf[0:1, :] * lf[1:2, :], axis=1, keepdims=True))
           - jnp.exp(jnp.sum(lf[2:3, :] * lf[3:4, :], axis=1, keepdims=True)) + lambda_init)
    o1 = acc_ref[0, :HEAD_DIM, :] / acc_ref[0, HEAD_DIM:HEAD_DIM + 1, :]
    o2 = acc_ref[1, :HEAD_DIM, :] / acc_ref[1, HEAD_DIM:HEAD_DIM + 1, :]
    d = o1 - lam * o2
    ms = jnp.mean(d * d, axis=0, keepdims=True)
    dn = (d * lax.rsqrt(ms + RMS_EPS)).T
    o_ref[...] = (dn * g_ref[...] * (1.0 - lambda_init)).astype(o_ref.dtype)


def _diff_attention(proj, vt, df_lambda, gain, tq, lambda_init):
    seq = proj.shape[0]
    qcol = 3 * SB_HEADS
    kcol = qcol + DIFF_HEADS
    kern = functools.partial(_diff_kernel, tq=tq, lambda_init=lambda_init)
    return pl.pallas_call(
        kern,
        grid=(DIFF_HEADS, seq // tq),
        in_specs=[pl.BlockSpec((tq, HEAD_DIM), lambda h, i: (i, qcol + h)),
                  pl.BlockSpec((seq, HEAD_DIM), lambda h, i: (0, kcol + h)),
                  pl.BlockSpec((seq // tq, HEAD_DIM, tq), lambda h, i: (0, h, 0)),
                  pl.BlockSpec((4, DIFF_QK_DIM), lambda h, i: (0, 0)),
                  pl.BlockSpec((1, HEAD_DIM), lambda h, i: (0, 0))],
        out_specs=pl.BlockSpec((tq, HEAD_DIM), lambda h, i: (i, h)),
        out_shape=jax.ShapeDtypeStruct((seq, DIFF_HEADS * HEAD_DIM), BF16),
        scratch_shapes=[pltpu.VMEM((2, tq, tq), F32), pltpu.VMEM((2, tq, tq), F32),
                        pltpu.VMEM((2, tq, tq), BF16), pltpu.VMEM((2, tq, tq), BF16),
                        pltpu.VMEM((2, HEAD_DIM + SUM_ROWS, tq), F32)],
        compiler_params=_params("parallel", "arbitrary"),
        name="diff_attn",
    )(proj, proj, vt, df_lambda, gain)


def _layer_norm(r, g, b):
    mu = jnp.mean(r, axis=1, keepdims=True)
    d = r - mu
    var = jnp.mean(d * d, axis=1, keepdims=True)
    return d * lax.rsqrt(var + LN_EPS) * g + b


def _proj_ln_kernel(*refs, n_parts):
    a_refs = refs[:n_parts]
    w_ref, x_ref, g_ref, b_ref, o_ref, ob_ref = refs[n_parts:]
    y = None
    off = 0
    for a_ref in a_refs:
        kk = a_ref.shape[1]
        part = _dot(a_ref[...], w_ref[off:off + kk, :])
        y = part if y is None else y + part
        off += kk
    out = _layer_norm(DEEPNORM_ALPHA * x_ref[...] + y, g_ref[...], b_ref[...])
    o_ref[...] = out
    ob_ref[...] = out.astype(BF16)


def _proj_ln(parts, w, x, gain, bias, tm):
    seq, d = x.shape
    kern = functools.partial(_proj_ln_kernel, n_parts=len(parts))
    row = lambda i: (i, 0)
    fixed = lambda i: (0, 0)
    return pl.pallas_call(
        kern,
        grid=(seq // tm,),
        in_specs=[pl.BlockSpec((tm, p.shape[1]), row) for p in parts] + [
            pl.BlockSpec(w.shape, fixed),
            pl.BlockSpec((tm, d), row),
            pl.BlockSpec((1, d), fixed),
            pl.BlockSpec((1, d), fixed)],
        out_specs=[pl.BlockSpec((tm, d), row), pl.BlockSpec((tm, d), row)],
        out_shape=[jax.ShapeDtypeStruct((seq, d), F32), jax.ShapeDtypeStruct((seq, d), BF16)],
        compiler_params=_params("parallel"),
        name="proj_residual_ln",
    )(*parts, w, x, gain, bias)


def _xattn_kernel(q_ref, kv_ref, o_ref, *, d_model, head_dim):
    scale = 1.0 / math.sqrt(head_dim)
    for hh in range(d_model // head_dim):
        lo = hh * head_dim
        q = q_ref[:, lo:lo + head_dim]
        k = kv_ref[:, lo:lo + head_dim]
        v = kv_ref[:, d_model + lo:d_model + lo + head_dim]
        s = _dot_nt(q, k) * scale
        m = jnp.max(s, axis=1, keepdims=True)
        p = jnp.exp(s - m)
        p = p / jnp.sum(p, axis=1, keepdims=True)
        o_ref[:, lo:lo + head_dim] = _dot(p.astype(BF16), v).astype(o_ref.dtype)


def _xattn(q, kv, tm):
    seq, d = q.shape
    kern = functools.partial(_xattn_kernel, d_model=d, head_dim=d // MEM_HEADS)
    return pl.pallas_call(
        kern,
        grid=(seq // tm,),
        in_specs=[pl.BlockSpec((tm, d), lambda i: (i, 0)),
                  pl.BlockSpec(kv.shape, lambda i: (0, 0))],
        out_specs=pl.BlockSpec((tm, d), lambda i: (i, 0)),
        out_shape=jax.ShapeDtypeStruct((seq, d), BF16),
        compiler_params=_params("parallel"),
        name="memory_xattn",
    )(q, kv)


def _top16(s, vals_ref):
    unranked = jnp.full(s.shape, float(PEER_TOPK), F32)

    def step(k, carry, exact_ties):
        work, rank = carry
        m = jnp.max(work, axis=0, keepdims=True)
        sel = work == m
        if exact_ties:
            iota = lax.broadcasted_iota(jnp.int32, s.shape, 0).astype(F32)
            sel = iota == jnp.min(jnp.where(sel, iota, float(s.shape[0])), axis=0, keepdims=True)
        vals_ref[pl.ds(k, 1), :] = m
        return jnp.where(sel, NEG_INF, work), jnp.where(sel, lax.convert_element_type(k, F32), rank)

    def run(exact_ties):
        return lax.fori_loop(0, PEER_TOPK, functools.partial(step, exact_ties=exact_ties), (s, unranked))[1]

    rank = run(False)
    ranked = jnp.sum(jnp.where(rank < float(PEER_TOPK), 1.0, 0.0), axis=0, keepdims=True)
    return lax.cond(jnp.max(ranked) > float(PEER_TOPK), lambda _: run(True), lambda r: r, rank)


CAND_KEEP = tuple(PEER_TOPK // (k1 + 1) for k1 in range(PEER_TOPK))
CAND_ROWS = -(-sum(CAND_KEEP) // 8) * 8


def _route_kernel(pq_ref, keys_ref, e2_ref, r2_ref, w1_ref, n1_ref, v1_ref, v2_ref, vc_ref, cand_ref):
    q = pq_ref[...]
    s1 = _dot_nt(keys_ref[0], q[:, :N_KEYS])
    s2 = _dot_nt(keys_ref[1], q[:, N_KEYS:])
    rank1 = _top16(s1, v1_ref)
    rank2 = _top16(s2, v2_ref)
    cand_ref[...] = jnp.full(cand_ref.shape, NEG_INF, F32)
    row = lax.broadcasted_iota(jnp.int32, cand_ref.shape, 0)
    seg = jnp.zeros(cand_ref.shape, F32)
    off = 0
    for k1, keep in enumerate(CAND_KEEP):
        cand_ref[off:off + keep, :] = v1_ref[k1:k1 + 1, :] + v2_ref[0:keep, :]
        off += keep
        seg = seg + jnp.where(row >= off, 1.0, 0.0)
    crank = _top16(cand_ref[...], vc_ref)
    top = vc_ref[...]
    z = jnp.sum(jnp.exp(top - top[0:1, :]), axis=0, keepdims=True)
    chosen = jnp.where(crank < float(PEER_TOPK), 1.0, 0.0)
    n1 = jnp.zeros(s1.shape, F32)
    for k1 in range(PEER_TOPK):
        cnt = jnp.sum(jnp.where(seg == float(k1), chosen, 0.0), axis=0, keepdims=True)
        n1 = jnp.where(rank1 == float(k1), cnt, n1)
    v2 = v2_ref[...]
    e2_ref[0] = jnp.exp(s2 - v2[0:1, :]).astype(e2_ref.dtype)
    r2_ref[0] = rank2.astype(r2_ref.dtype)
    w1_ref[0] = jnp.exp(s1 - v1_ref[0:1, :]) / z
    n1_ref[0] = n1


def _peer_route(pq, keys, tt):
    seq = pq.shape[0]
    qd = 2 * N_KEYS
    tab = pl.BlockSpec((1, N_KEYS, tt), lambda i, h: (h, 0, i))
    shape = jax.ShapeDtypeStruct((PEER_HEADS, N_KEYS, seq), F32)
    packed = jax.ShapeDtypeStruct((PEER_HEADS, N_KEYS, seq), BF16)
    return pl.pallas_call(
        _route_kernel,
        grid=(seq // tt, PEER_HEADS),
        in_specs=[pl.BlockSpec((tt, qd), lambda i, h: (i, h)),
                  pl.BlockSpec(keys.shape, lambda i, h: (0, 0, 0))],
        out_specs=[tab, tab, tab, tab],
        out_shape=[packed, packed, shape, shape],
        scratch_shapes=[pltpu.VMEM((PEER_TOPK, tt), F32),
                        pltpu.VMEM((PEER_TOPK, tt), F32),
                        pltpu.VMEM((PEER_TOPK, tt), F32),
                        pltpu.VMEM((CAND_ROWS, tt), F32)],
        compiler_params=_params("parallel", "parallel"),
        name="peer_route",
    )(pq, keys)


def _peer_kernel(xb_ref, u_ref, v_ref, e2_ref, r2_ref, w1_ref, n1_ref, x_ref, g_ref, b_ref,
                 o_ref, a_ref, gate_ref, *, keys_per_step, n_chunks):
    cc = pl.program_id(1)
    cur, prev = cc % 2, (cc + 1) % 2

    @pl.when(cc == 0)
    def _():
        o_ref[...] = jnp.zeros_like(o_ref)
        a_ref[...] = jnp.zeros_like(a_ref)
        gate_ref[...] = jnp.zeros_like(gate_ref)

    gated = (a_ref[prev].astype(F32) * gate_ref[prev]).astype(BF16)
    o_ref[...] += _dot(gated, v_ref[...])

    hidden = _dot(xb_ref[...], u_ref[...])
    a_ref[cur] = (0.5 * hidden * (1.0 + lax.erf(hidden * math.sqrt(0.5)))).astype(BF16)

    chunk = jnp.minimum(cc, n_chunks - 1)
    zero = jnp.zeros(r2_ref.shape[1:], BF16)

    def rows(row):
        tile = jnp.broadcast_to(row, (BF16_SUBLANES, row.shape[1])).astype(BF16)
        return jnp.tile(tile, (N_KEYS // BF16_SUBLANES, 1))

    for part in range(keys_per_step):
        c = chunk * keys_per_step + part
        gt = None
        for h in range(PEER_HEADS):
            n1 = rows(n1_ref[h, pl.ds(c, 1), :])
            w1 = rows(w1_ref[h, pl.ds(c, 1), :])
            term = jnp.where(r2_ref[h] < n1, e2_ref[h] * w1, zero)
            gt = term if gt is None else gt + term
        gate_ref[cur, :, part * N_KEYS:(part + 1) * N_KEYS] = gt.astype(F32).T

    @pl.when(cc == n_chunks)
    def _():
        o_ref[...] = _layer_norm(DEEPNORM_ALPHA * x_ref[...] + o_ref[...], g_ref[...], b_ref[...])


def _peer_dense(xb, x, ut, v, tables, gain, bias, tt, keys_per_step):
    seq, d = x.shape
    te = keys_per_step * N_KEYS
    n_chunks = v.shape[0] // te
    tab = pl.BlockSpec((PEER_HEADS, N_KEYS, tt), lambda i, c: (0, 0, i))
    row = lambda i, c: (i, 0)
    fixed = lambda i, c: (0, 0)
    kern = functools.partial(_peer_kernel, keys_per_step=keys_per_step, n_chunks=n_chunks)
    return pl.pallas_call(
        kern,
        grid=(seq // tt, n_chunks + 1),
        in_specs=[pl.BlockSpec((tt, d), row),
                  pl.BlockSpec((d, te), lambda i, c: (0, jnp.minimum(c, n_chunks - 1))),
                  pl.BlockSpec((te, d), lambda i, c: (jnp.maximum(c - 1, 0), 0)),
                  tab, tab, tab, tab,
                  pl.BlockSpec((tt, d), row),
                  pl.BlockSpec((1, d), fixed),
                  pl.BlockSpec((1, d), fixed)],
        out_specs=pl.BlockSpec((tt, d), row),
        out_shape=jax.ShapeDtypeStruct((seq, d), F32),
        scratch_shapes=[pltpu.VMEM((2, tt, te), BF16), pltpu.VMEM((2, tt, te), F32)],
        compiler_params=_params("parallel", "arbitrary"),
        name="peer_dense",
    )(xb, ut, v, *tables, x, gain, bias)


def _tile(n, want):
    return min(n, want)


def kernel(x, mem, w_in, sb_norm_gain, df_lambda, df_subln_gain, w_o, ln1_gain, ln1_bias, w_mq, w_mkv, w_mo, ln2_gain, ln2_bias, w_pq, peer_sub_keys, peer_u, peer_v, ln3_gain, ln3_bias):
    b, seq, d = x.shape
    assert b == 1 and w_in.shape[0] == DEPTH
    x2d = x.reshape(seq, d)
    mem2d = mem.reshape(mem.shape[1], d).astype(BF16)
    for l in range(DEPTH):
        lambda_init = 0.8 - 0.6 * math.exp(-0.3 * l)
        row1 = lambda a: a.reshape(1, -1)
        xb = x2d.astype(BF16)
        proj = _inproj(xb, w_in[l].astype(BF16), _tile(seq, 512), 1024,
                       rope_cols=(3 * SB_HEADS * HEAD_DIM, 5 * SB_HEADS * HEAD_DIM))
        sb_o = _sb_attention(proj, sb_norm_gain[l], _tile(seq, 256))
        tq = _tile(seq, 256)
        w_v = w_in[l][:, 5 * SB_HEADS * HEAD_DIM:]
        vt = _vt_proj(xb, w_v.T.astype(BF16), tq)
        df_o = _diff_attention(proj, vt, df_lambda[l], row1(df_subln_gain[l]), tq, lambda_init)
        x1, x1b = _proj_ln([sb_o, df_o], w_o[l].astype(BF16), x2d, row1(ln1_gain[l]), row1(ln1_bias[l]),
                           _tile(seq, 256))
        q = _matmul(x1b, w_mq[l].astype(BF16), _tile(seq, 512), 1024)
        kv = _matmul(mem2d, w_mkv[l].astype(BF16), mem2d.shape[0], 1024)
        xa = _xattn(q, kv, _tile(seq, 256))
        x2, x2b = _proj_ln([xa], w_mo[l].astype(BF16), x1, row1(ln2_gain[l]), row1(ln2_bias[l]),
                           _tile(seq, 256))
        pq = _matmul(x2b, w_pq[l].astype(BF16), _tile(seq, 512), 1024)
        tables = _peer_route(pq, peer_sub_keys[l].astype(BF16), _tile(seq, 256))
        x2d = _peer_dense(x2b, x2, peer_u[l].T.astype(BF16), peer_v[l].astype(BF16), tables,
                          row1(ln3_gain[l]), row1(ln3_bias[l]), _tile(seq, 512), 4)
    return x2d.reshape(b, seq, d)
```

```python
import functools
import math

import jax
import jax.numpy as jnp
import numpy as np
from jax import lax
from jax.experimental import pallas as pl
from jax.experimental.pallas import tpu as pltpu

F32 = jnp.float32
BF16 = jnp.bfloat16

LANES = 128
BF16_SUBLANES = 16
MXU_WIDTH = 256
VMEM_LIMIT_BYTES = 56 * 1024 * 1024

DEPTH = 1
SB_HEADS = 8
DIFF_HEADS = 8
HEAD_DIM = 128
DIFF_QK_DIM = 64
ROPE_DIM = 16
ROPE_THETA = 500000.0
MEM_HEADS = 4
PEER_HEADS = 8
N_KEYS = 128
PEER_TOPK = 16
LN_EPS = 1e-5
RMS_EPS = 1e-6
DEEPNORM_ALPHA = (2 * DEPTH) ** 0.25
NEG_INF = float("-inf")
SB_DEAD_LOG = -105.0
SUM_ROWS = 16


def _dot(a, b):
    return jnp.dot(a, b, preferred_element_type=F32)


def _dot_nt(a, b):
    return lax.dot_general(a, b, (((1,), (1,)), ((), ())), preferred_element_type=F32)


def _params(*semantics):
    return pltpu.CompilerParams(dimension_semantics=semantics, vmem_limit_bytes=VMEM_LIMIT_BYTES)


def _matmul_kernel(a_ref, b_ref, o_ref):
    o_ref[...] = _dot(a_ref[...], b_ref[...]).astype(o_ref.dtype)


def _matmul(a, b, tm, tn, out_dtype=BF16):
    m, k = a.shape
    n = b.shape[1]
    return pl.pallas_call(
        _matmul_kernel,
        grid=(m // tm, n // tn),
        in_specs=[pl.BlockSpec((tm, k), lambda i, j: (i, 0)),
                  pl.BlockSpec((k, tn), lambda i, j: (0, j))],
        out_specs=pl.BlockSpec((tm, tn), lambda i, j: (i, j)),
        out_shape=jax.ShapeDtypeStruct((m, n), out_dtype),
        compiler_params=_params("parallel", "parallel"),
        name="matmul",
    )(a, b)


def _inproj_kernel(a_ref, b_ref, c_ref, s1_ref, s2_ref, o_ref, vt_ref, *, rope_lo, rope_hi, vt_col):
    j = pl.program_id(1)
    acc = _dot(a_ref[...].astype(BF16), b_ref[...])
    o_ref[...] = acc.astype(o_ref.dtype)

    @pl.when(j == vt_col)
    def _():
        tk = vt_ref.shape[2]
        for kb in range(vt_ref.shape[0]):
            vt_ref[kb] = acc[kb * tk:(kb + 1) * tk, :].T.astype(vt_ref.dtype)

    @pl.when(jnp.logical_and(j >= rope_lo, j < rope_hi))
    def _():
        cos, sin_up, sin_dn = c_ref[...], s1_ref[...], s2_ref[...]
        for cc in range(acc.shape[1] // LANES):
            t = acc[:, cc * LANES:(cc + 1) * LANES]
            up = pltpu.roll(t, LANES - ROPE_DIM // 2, axis=1)
            dn = pltpu.roll(t, ROPE_DIM // 2, axis=1)
            o_ref[:, cc * LANES:(cc + 1) * LANES] = (t * cos + up * sin_up + dn * sin_dn).astype(o_ref.dtype)


def _rope_tables(seq):
    half = ROPE_DIM // 2
    inv_freq = jnp.power(ROPE_THETA, -jnp.arange(half, dtype=F32) * 2.0 / ROPE_DIM)
    ang = jnp.arange(seq).astype(F32)[:, None] * inv_freq[None, :]
    cos, sin = jnp.cos(ang), jnp.sin(ang)
    ones = jnp.ones((seq, DIFF_QK_DIM - ROPE_DIM), F32)
    zeros = jnp.zeros((seq, DIFF_QK_DIM - ROPE_DIM), F32)
    zh = jnp.zeros((seq, half), F32)
    c64 = jnp.concatenate([cos, cos, ones], axis=1)
    up64 = jnp.concatenate([-sin, zh, zeros], axis=1)
    dn64 = jnp.concatenate([zh, sin, zeros], axis=1)
    rep = LANES // DIFF_QK_DIM
    return jnp.tile(c64, (1, rep)), jnp.tile(up64, (1, rep)), jnp.tile(dn64, (1, rep))


def _inproj(x, w_in, tm, tn, tk, rope_cols, vt_cols):
    seq, k = x.shape
    n = w_in.shape[1]
    assert vt_cols == (n - tn, n)
    cos, sin_up, sin_dn = _rope_tables(seq)
    tab = pl.BlockSpec((tm, LANES), lambda i, j: (i, 0))
    kern = functools.partial(_inproj_kernel, rope_lo=rope_cols[0] // tn, rope_hi=rope_cols[1] // tn,
                             vt_col=vt_cols[0] // tn)
    return pl.pallas_call(
        kern,
        grid=(seq // tm, n // tn),
        in_specs=[pl.BlockSpec((tm, k), lambda i, j: (i, 0)),
                  pl.BlockSpec((k, tn), lambda i, j: (0, j)),
                  tab, tab, tab],
        out_specs=[pl.BlockSpec((tm, tn), lambda i, j: (i, j)),
                   pl.BlockSpec((tm // tk, tn, tk), lambda i, j: (i, 0, 0))],
        out_shape=[jax.ShapeDtypeStruct((seq, n), BF16),
                   jax.ShapeDtypeStruct((seq // tk, tn, tk), BF16)],
        compiler_params=_params("parallel", "arbitrary"),
        name="inproj_rope",
    )(x, w_in, cos, sin_up, sin_dn)


def _sb_kernel(q_ref, k_ref, v_ref, g_ref, o_ref, *, tq, scale):
    h = pl.program_id(0)
    i = pl.program_id(1)
    q = q_ref[...]
    row = lax.broadcasted_iota(jnp.int32, (tq, tq), 0)
    col = lax.broadcasted_iota(jnp.int32, (tq, tq), 1)
    later = (row > col).astype(BF16)
    before = col < row

    def block_terms(kb, masked):
        start = pl.multiple_of(kb * tq, tq)
        k = k_ref[pl.ds(start, tq), :]
        z = _dot_nt(q, k) * scale
        sp = jnp.log(1.0 + jnp.exp(-jnp.abs(z)))
        log_beta = jnp.minimum(z, 0.0) - sp
        log_keep = -jnp.maximum(z, 0.0) - sp
        if masked:
            log_keep = jnp.where(before, log_keep, 0.0)
        hi = log_keep.astype(BF16)
        lo = (log_keep - hi.astype(F32)).astype(BF16)
        stick = _dot(hi, later) + _dot(lo, later)
        return log_beta + stick, jnp.sum(log_keep, axis=1, keepdims=True), v_ref[pl.ds(start, tq), :]

    def block_apply(terms, acc, c, masked, valid=None):
        log_w, keep_sum, v = terms
        w = jnp.exp(log_w + c)
        if masked:
            w = jnp.where(before, w, 0.0)
        if valid is not None:
            w = jnp.where(valid, w, 0.0)
            keep_sum = jnp.where(valid, keep_sum, 0.0)
        return acc + _dot(w.astype(BF16), v), c + keep_sum

    acc0 = jnp.zeros((tq, HEAD_DIM), F32)
    c0 = jnp.zeros((tq, 1), F32)
    diag = block_terms(i, True)
    prev = block_terms(jnp.maximum(i - 1, 0), False)
    acc, c = block_apply(diag, acc0, c0, True)
    acc, c = block_apply(prev, acc, c, False, valid=i >= 1)

    def live(carry):
        jj, _, _, cmax = carry
        return jnp.logical_and(jj < i, cmax > SB_DEAD_LOG)

    def older(carry):
        jj, acc, c, _ = carry
        acc, c = block_apply(block_terms(i - 1 - jj, False), acc, c, False)
        return jj + 1, acc, c, jnp.max(c)

    _, acc, c, _ = lax.while_loop(live, older, (jnp.int32(1), acc, c, jnp.max(c)))
    g = g_ref[pl.ds(h, 1), :]
    ms = jnp.mean(acc * acc, axis=1, keepdims=True)
    o_ref[...] = (acc * lax.rsqrt(ms + RMS_EPS) * g).astype(o_ref.dtype)


def _sb_attention(proj, gain, tq):
    seq = proj.shape[0]
    kern = functools.partial(_sb_kernel, tq=tq, scale=1.0 / math.sqrt(HEAD_DIM))
    return pl.pallas_call(
        kern,
        grid=(SB_HEADS, seq // tq),
        in_specs=[pl.BlockSpec((tq, HEAD_DIM), lambda h, i: (i, h)),
                  pl.BlockSpec((seq, HEAD_DIM), lambda h, i: (0, SB_HEADS + h)),
                  pl.BlockSpec((seq, HEAD_DIM), lambda h, i: (0, 2 * SB_HEADS + h)),
                  pl.BlockSpec((SB_HEADS, HEAD_DIM), lambda h, i: (0, 0))],
        out_specs=pl.BlockSpec((tq, HEAD_DIM), lambda h, i: (i, h)),
        out_shape=jax.ShapeDtypeStruct((seq, SB_HEADS * HEAD_DIM), BF16),
        compiler_params=_params("parallel", "arbitrary"),
        name="stickbreak_attn",
    )(proj, proj, proj, gain)


def _diff_kernel(q_ref, k_ref, vt_ref, lam_ref, g_ref, o_ref, s0_ref, s1_ref, p0_ref, p1_ref, acc_ref,
                 *, tq, lambda_init):
    i = pl.program_id(1)
    s_refs, p_refs = (s0_ref, s1_ref), (p0_ref, p1_ref)
    q = q_ref[...] * jnp.asarray(1.0 / math.sqrt(DIFF_QK_DIM), BF16)
    lane = lax.broadcasted_iota(jnp.int32, (tq, HEAD_DIM), 1)
    zero = jnp.zeros_like(q)
    qmaps = (jnp.where(lane < DIFF_QK_DIM, q, zero),
             jnp.where(lane >= DIFF_QK_DIM, q, zero))
    ones = jnp.ones((SUM_ROWS, tq), BF16)

    def scores(kb, mp):
        k = k_ref[pl.ds(pl.multiple_of(kb * tq, tq), tq), :]
        return _dot_nt(k, qmaps[mp])

    def softmax_step(s, m, slot, mp):
        m_new = jnp.maximum(m, jnp.max(s, axis=0, keepdims=True))
        p_refs[slot][mp] = jnp.exp(s - m_new).astype(BF16)
        return m_new, jnp.exp(m - m_new)

    def accumulate(kb, alpha, slot, mp):
        v_aug = jnp.concatenate([vt_ref[kb], ones], axis=0)
        acc_ref[mp] = alpha * acc_ref[mp] + _dot(v_aug, p_refs[slot][mp])

    key = lax.broadcasted_iota(jnp.int32, (tq, tq), 0)
    qry = lax.broadcasted_iota(jnp.int32, (tq, tq), 1)
    causal = key <= qry
    m0 = jnp.full((1, tq), NEG_INF, F32)
    acc_ref[...] = jnp.zeros_like(acc_ref)
    stats = []
    for mp in range(2):
        s_refs[1][mp] = scores(0, mp)
        stats.extend(softmax_step(jnp.where(causal, scores(i, mp), NEG_INF), m0, 0, mp))
    stats = tuple(stats)

    def step(t, carry, cur):
        nxt = 1 - cur
        vb = jnp.where(t == 0, i, t - 1)
        out = []
        for mp in range(2):
            m, alpha = carry[2 * mp], carry[2 * mp + 1]
            s_refs[cur][mp] = scores(jnp.minimum(t + 1, i - 1), mp)
            accumulate(vb, alpha, cur, mp)
            out.extend(softmax_step(s_refs[nxt][mp], m, nxt, mp))
        return tuple(out)

    stats = lax.fori_loop(0, i // 2, lambda u, c: step(2 * u + 1, step(2 * u, c, 0), 1), stats)
    odd = i % 2 == 1
    stats = lax.cond(odd, lambda c: step(i - 1, c, 0), lambda c: c, stats)
    last = jnp.maximum(i - 1, 0)
    for slot in range(2):
        @pl.when((i % 2) == slot)
        def _():
            for mp in range(2):
                accumulate(last, stats[2 * mp + 1], slot, mp)

    lf = lam_ref[...]
    lam = (jnp.exp(jnp.sum(lf[0:1, :] * lf[1:2, :], axis=1, keepdims=True))
           - jnp.exp(jnp.sum(lf[2:3, :] * lf[3:4, :], axis=1, keepdims=True)) + lambda_init)
    o1 = acc_ref[0, :HEAD_DIM, :] / acc_ref[0, HEAD_DIM:HEAD_DIM + 1, :]
    o2 = acc_ref[1, :HEAD_DIM, :] / acc_ref[1, HEAD_DIM:HEAD_DIM + 1, :]
    d = o1 - lam * o2
    ms = jnp.mean(d * d, axis=0, keepdims=True)
    dn = (d * lax.rsqrt(ms + RMS_EPS)).T
    o_ref[...] = (dn * g_ref[...] * (1.0 - lambda_init)).astype(o_ref.dtype)


def _diff_attention(proj, vt, df_lambda, gain, tq, lambda_init):
    seq = proj.shape[0]
    qcol = 3 * SB_HEADS
    kcol = qcol + DIFF_HEADS
    kern = functools.partial(_diff_kernel, tq=tq, lambda_init=lambda_init)
    return pl.pallas_call(
        kern,
        grid=(DIFF_HEADS, seq // tq),
        in_specs=[pl.BlockSpec((tq, HEAD_DIM), lambda h, i: (i, qcol + h)),
                  pl.BlockSpec((seq, HEAD_DIM), lambda h, i: (0, kcol + h)),
                  pl.BlockSpec((seq // tq, HEAD_DIM, tq), lambda h, i: (0, h, 0)),
                  pl.BlockSpec((4, DIFF_QK_DIM), lambda h, i: (0, 0)),
                  pl.BlockSpec((1, HEAD_DIM), lambda h, i: (0, 0))],
        out_specs=pl.BlockSpec((tq, HEAD_DIM), lambda h, i: (i, h)),
        out_shape=jax.ShapeDtypeStruct((seq, DIFF_HEADS * HEAD_DIM), BF16),
        scratch_shapes=[pltpu.VMEM((2, tq, tq), F32), pltpu.VMEM((2, tq, tq), F32),
                        pltpu.VMEM((2, tq, tq), BF16), pltpu.VMEM((2, tq, tq), BF16),
                        pltpu.VMEM((2, HEAD_DIM + SUM_ROWS, tq), F32)],
        compiler_params=_params("parallel", "arbitrary"),
        name="diff_attn",
    )(proj, proj, vt, df_lambda, gain)


def _layer_norm(r, g, b):
    mu = jnp.mean(r, axis=1, keepdims=True)
    d = r - mu
    var = jnp.mean(d * d, axis=1, keepdims=True)
    return d * lax.rsqrt(var + LN_EPS) * g + b


def _proj_ln_kernel(*refs, n_parts):
    a_refs = refs[:n_parts]
    w_ref, x_ref, g_ref, b_ref, o_ref, ob_ref = refs[n_parts:]
    y = None
    off = 0
    for a_ref in a_refs:
        kk = a_ref.shape[1]
        part = _dot(a_ref[...], w_ref[off:off + kk, :])
        y = part if y is None else y + part
        off += kk
    out = _layer_norm(DEEPNORM_ALPHA * x_ref[...] + y, g_ref[...], b_ref[...])
    o_ref[...] = out
    ob_ref[...] = out.astype(BF16)


def _proj_ln(parts, w, x, gain, bias, tm):
    seq, d = x.shape
    kern = functools.partial(_proj_ln_kernel, n_parts=len(parts))
    row = lambda i: (i, 0)
    fixed = lambda i: (0, 0)
    return pl.pallas_call(
        kern,
        grid=(seq // tm,),
        in_specs=[pl.BlockSpec((tm, p.shape[1]), row) for p in parts] + [
            pl.BlockSpec(w.shape, fixed),
            pl.BlockSpec((tm, d), row),
            pl.BlockSpec((1, d), fixed),
            pl.BlockSpec((1, d), fixed)],
        out_specs=[pl.BlockSpec((tm, d), row), pl.BlockSpec((tm, d), row)],
        out_shape=[jax.ShapeDtypeStruct((seq, d), F32), jax.ShapeDtypeStruct((seq, d), BF16)],
        compiler_params=_params("parallel"),
        name="proj_residual_ln",
    )(*parts, w, x, gain, bias)


def _xattn_kernel(q_ref, kv_ref, o_ref, *, d_model, head_dim):
    scale = 1.0 / math.sqrt(head_dim)
    for hh in range(d_model // head_dim):
        lo = hh * head_dim
        q = q_ref[:, lo:lo + head_dim]
        k = kv_ref[:, lo:lo + head_dim]
        v = kv_ref[:, d_model + lo:d_model + lo + head_dim]
        s = _dot_nt(q, k) * scale
        m = jnp.max(s, axis=1, keepdims=True)
        p = jnp.exp(s - m)
        p = p / jnp.sum(p, axis=1, keepdims=True)
        o_ref[:, lo:lo + head_dim] = _dot(p.astype(BF16), v).astype(o_ref.dtype)


def _xattn(q, kv, tm):
    seq, d = q.shape
    kern = functools.partial(_xattn_kernel, d_model=d, head_dim=d // MEM_HEADS)
    return pl.pallas_call(
        kern,
        grid=(seq // tm,),
        in_specs=[pl.BlockSpec((tm, d), lambda i: (i, 0)),
                  pl.BlockSpec(kv.shape, lambda i: (0, 0))],
        out_specs=pl.BlockSpec((tm, d), lambda i: (i, 0)),
        out_shape=jax.ShapeDtypeStruct((seq, d), BF16),
        compiler_params=_params("parallel"),
        name="memory_xattn",
    )(q, kv)


def _top16(s, vals_ref, exact_ties):
    def step(k, carry):
        work, rank = carry
        m = jnp.max(work, axis=0, keepdims=True)
        sel = work == m
        if exact_ties:
            iota = lax.broadcasted_iota(jnp.int32, s.shape, 0).astype(F32)
            sel = iota == jnp.min(jnp.where(sel, iota, float(s.shape[0])), axis=0, keepdims=True)
        vals_ref[pl.ds(k, 1), :] = m
        return jnp.where(sel, NEG_INF, work), jnp.where(sel, lax.convert_element_type(k, F32), rank)

    _, rank = lax.fori_loop(0, PEER_TOPK, step, (s, jnp.full(s.shape, float(PEER_TOPK), F32)))
    tie_free = float(sum(range(PEER_TOPK)) + PEER_TOPK * (s.shape[0] - PEER_TOPK))
    return rank, tie_free - jnp.sum(rank, axis=0, keepdims=True)


CAND_KEEP = tuple(PEER_TOPK // (k1 + 1) for k1 in range(PEER_TOPK))
CAND_ROWS = -(-sum(CAND_KEEP) // 8) * 8


def _route_kernel(pq_ref, keys_ref, e2_ref, r2_ref, w1_ref, n1_ref, v1_ref, v2_ref, vc_ref, cand_ref):
    q = pq_ref[...]
    s1 = _dot_nt(keys_ref[0], q[:, :N_KEYS])
    s2 = _dot_nt(keys_ref[1], q[:, N_KEYS:])

    def route(exact_ties):
        rank1, deficit1 = _top16(s1, v1_ref, exact_ties)
        rank2, deficit2 = _top16(s2, v2_ref, exact_ties)
        cand_ref[...] = jnp.full(cand_ref.shape, NEG_INF, F32)
        row = lax.broadcasted_iota(jnp.int32, cand_ref.shape, 0)
        seg = jnp.zeros(cand_ref.shape, F32)
        off = 0
        for k1, keep in enumerate(CAND_KEEP):
            cand_ref[off:off + keep, :] = v1_ref[k1:k1 + 1, :] + v2_ref[0:keep, :]
            off += keep
            seg = seg + jnp.where(row >= off, 1.0, 0.0)
        crank, deficit3 = _top16(cand_ref[...], vc_ref, exact_ties)
        top = vc_ref[...]
        z = jnp.sum(jnp.exp(top - top[0:1, :]), axis=0, keepdims=True)
        chosen = jnp.where(crank < float(PEER_TOPK), 1.0, 0.0)
        rank1_packed = rank1.astype(BF16)
        n1 = jnp.zeros(s1.shape, BF16)
        for k1 in range(PEER_TOPK):
            cnt = jnp.sum(jnp.where(seg == float(k1), chosen, 0.0), axis=0, keepdims=True)
            cnt = jnp.tile(jnp.broadcast_to(cnt, (BF16_SUBLANES, cnt.shape[1])).astype(BF16),
                           (N_KEYS // BF16_SUBLANES, 1))
            n1 = jnp.where(rank1_packed == jnp.asarray(k1, BF16), cnt, n1)
        e2 = jnp.exp(s2 - v2_ref[0:1, :]).astype(e2_ref.dtype)
        w1 = jnp.exp(s1 - v1_ref[0:1, :]) / z
        tied = jnp.max(jnp.maximum(jnp.maximum(deficit1, deficit2), deficit3)) > 0.0
        return (e2, rank2.astype(r2_ref.dtype), w1, n1.astype(F32)), tied

    tables, tied = route(False)
    tables = lax.cond(tied, lambda _: route(True)[0], lambda t: t, tables)
    e2_ref[0], r2_ref[0], w1_ref[0], n1_ref[0] = tables


def _peer_route(pq, keys, tt):
    seq = pq.shape[0]
    qd = 2 * N_KEYS
    tab = pl.BlockSpec((1, N_KEYS, tt), lambda i, h: (h, 0, i))
    shape = jax.ShapeDtypeStruct((PEER_HEADS, N_KEYS, seq), F32)
    packed = jax.ShapeDtypeStruct((PEER_HEADS, N_KEYS, seq), BF16)
    return pl.pallas_call(
        _route_kernel,
        grid=(seq // tt, PEER_HEADS),
        in_specs=[pl.BlockSpec((tt, qd), lambda i, h: (i, h)),
                  pl.BlockSpec(keys.shape, lambda i, h: (0, 0, 0))],
        out_specs=[tab, tab, tab, tab],
        out_shape=[packed, packed, shape, shape],
        scratch_shapes=[pltpu.VMEM((PEER_TOPK, tt), F32),
                        pltpu.VMEM((PEER_TOPK, tt), F32),
                        pltpu.VMEM((PEER_TOPK, tt), F32),
                        pltpu.VMEM((CAND_ROWS, tt), F32)],
        compiler_params=_params("parallel", "parallel"),
        name="peer_route",
    )(pq, keys)


def _peer_kernel(xb_ref, u_ref, v_ref, e2_ref, r2_ref, w1_ref, n1_ref, x_ref, g_ref, b_ref,
                 o_ref, a_ref, gate_ref, *, keys_per_step, n_chunks):
    cc = pl.program_id(1)
    cur, prev = cc % 2, (cc + 1) % 2

    @pl.when(cc == 0)
    def _():
        o_ref[...] = jnp.zeros_like(o_ref)
        a_ref[...] = jnp.zeros_like(a_ref)
        gate_ref[...] = jnp.zeros_like(gate_ref)

    gated = (a_ref[prev].astype(F32) * gate_ref[prev]).astype(BF16)
    o_ref[...] += _dot(gated, v_ref[...])

    hidden = _dot(xb_ref[...], u_ref[...])
    a_ref[cur] = (0.5 * hidden * (1.0 + lax.erf(hidden * math.sqrt(0.5)))).astype(BF16)

    chunk = jnp.minimum(cc, n_chunks - 1)
    zero = jnp.zeros(r2_ref.shape[1:], BF16)

    def rows(row):
        tile = jnp.broadcast_to(row, (BF16_SUBLANES, row.shape[1])).astype(BF16)
        return jnp.tile(tile, (N_KEYS // BF16_SUBLANES, 1))

    for part in range(keys_per_step):
        c = chunk * keys_per_step + part
        gt = None
        for h in range(PEER_HEADS):
            n1 = rows(n1_ref[h, pl.ds(c, 1), :])
            w1 = rows(w1_ref[h, pl.ds(c, 1), :])
            term = jnp.where(r2_ref[h] < n1, e2_ref[h] * w1, zero)
            gt = term if gt is None else gt + term
        gate_ref[cur, :, part * N_KEYS:(part + 1) * N_KEYS] = gt.astype(F32).T

    @pl.when(cc == n_chunks)
    def _():
        o_ref[...] = _layer_norm(DEEPNORM_ALPHA * x_ref[...] + o_ref[...], g_ref[...], b_ref[...])


def _peer_dense(xb, x, ut, v, tables, gain, bias, tt, keys_per_step):
    seq, d = x.shape
    te = keys_per_step * N_KEYS
    n_chunks = v.shape[0] // te
    tab = pl.BlockSpec((PEER_HEADS, N_KEYS, tt), lambda i, c: (0, 0, i))
    row = lambda i, c: (i, 0)
    fixed = lambda i, c: (0, 0)
    kern = functools.partial(_peer_kernel, keys_per_step=keys_per_step, n_chunks=n_chunks)
    return pl.pallas_call(
        kern,
        grid=(seq // tt, n_chunks + 1),
        in_specs=[pl.BlockSpec((tt, d), row),
                  pl.BlockSpec((d, te), lambda i, c: (0, jnp.minimum(c, n_chunks - 1))),
                  pl.BlockSpec((te, d), lambda i, c: (jnp.maximum(c - 1, 0), 0)),
                  tab, tab, tab, tab,
                  pl.BlockSpec((tt, d), row),
                  pl.BlockSpec((1, d), fixed),
                  pl.BlockSpec((1, d), fixed)],
        out_specs=pl.BlockSpec((tt, d), row),
        out_shape=jax.ShapeDtypeStruct((seq, d), F32),
        scratch_shapes=[pltpu.VMEM((2, tt, te), BF16), pltpu.VMEM((2, tt, te), F32)],
        compiler_params=_params("parallel", "arbitrary"),
        name="peer_dense",
    )(xb, ut, v, *tables, x, gain, bias)


def _tile(n, want):
    return min(n, want)


def kernel(x, mem, w_in, sb_norm_gain, df_lambda, df_subln_gain, w_o, ln1_gain, ln1_bias, w_mq, w_mkv, w_mo, ln2_gain, ln2_bias, w_pq, peer_sub_keys, peer_u, peer_v, ln3_gain, ln3_bias):
    b, seq, d = x.shape
    assert b == 1 and w_in.shape[0] == DEPTH
    x2d = x.reshape(seq, d)
    mem2d = mem.reshape(mem.shape[1], d).astype(BF16)
    for l in range(DEPTH):
        lambda_init = 0.8 - 0.6 * math.exp(-0.3 * l)
        row1 = lambda a: a.reshape(1, -1)
        tq = _tile(seq, 256)
        width = SB_HEADS * HEAD_DIM
        proj, vt = _inproj(x2d, w_in[l].astype(BF16), _tile(seq, 512), width, tq,
                           rope_cols=(3 * width, 5 * width), vt_cols=(5 * width, 6 * width))
        sb_o = _sb_attention(proj, sb_norm_gain[l], tq)
        df_o = _diff_attention(proj, vt, df_lambda[l], row1(df_subln_gain[l]), tq, lambda_init)
        x1, x1b = _proj_ln([sb_o, df_o], w_o[l].astype(BF16), x2d, row1(ln1_gain[l]), row1(ln1_bias[l]),
                           _tile(seq, 256))
        q = _matmul(x1b, w_mq[l].astype(BF16), _tile(seq, 512), 1024)
        kv = _matmul(mem2d, w_mkv[l].astype(BF16), mem2d.shape[0], 1024)
        xa = _xattn(q, kv, _tile(seq, 256))
        x2, x2b = _proj_ln([xa], w_mo[l].astype(BF16), x1, row1(ln2_gain[l]), row1(ln2_bias[l]),
                           _tile(seq, 256))
        pq = _matmul(x2b, w_pq[l].astype(BF16), _tile(seq, 512), 1024)
        tables = _peer_route(pq, peer_sub_keys[l].astype(BF16), _tile(seq, 256))
        x2d = _peer_dense(x2b, x2, peer_u[l].T.astype(BF16), peer_v[l].astype(BF16), tables,
                          row1(ln3_gain[l]), row1(ln3_bias[l]), _tile(seq, 512), 4)
    return x2d.reshape(b, seq, d)
```

```python
import functools
import math

import jax
import jax.numpy as jnp
from jax import lax
from jax.experimental import pallas as pl
from jax.experimental.pallas import tpu as pltpu

F32 = jnp.float32
BF16 = jnp.bfloat16

LANES = 128
BF16_SUBLANES = 16
MXU_WIDTH = 256
VMEM_LIMIT_BYTES = 56 * 1024 * 1024

ATTN_BLOCK = 256
PROJ_ROWS = 512
PROJ_COLS = 1024
LN_ROWS = 256
ROUTE_TOKENS = 256
PEER_TOKENS = 512
PEER_KEYS_PER_STEP = 4

DEPTH = 1
SB_HEADS = 8
DIFF_HEADS = 8
HEAD_DIM = 128
DIFF_QK_DIM = 64
ROPE_DIM = 16
ROPE_THETA = 500000.0
MEM_HEADS = 4
PEER_HEADS = 8
N_KEYS = 128
PEER_TOPK = 16
LN_EPS = 1e-5
RMS_EPS = 1e-6
DEEPNORM_ALPHA = (2 * DEPTH) ** 0.25
NEG_INF = float("-inf")
SB_DEAD_LOG = -105.0
SUM_ROWS = 16


def _dot(a, b):
    return jnp.dot(a, b, preferred_element_type=F32)


def _dot_nt(a, b):
    return lax.dot_general(a, b, (((1,), (1,)), ((), ())), preferred_element_type=F32)


def _params(*semantics):
    return pltpu.CompilerParams(dimension_semantics=semantics, vmem_limit_bytes=VMEM_LIMIT_BYTES)


def _matmul_kernel(a_ref, b_ref, o_ref):
    o_ref[...] = _dot(a_ref[...], b_ref[...]).astype(o_ref.dtype)


def _matmul(a, b, tm, tn, out_dtype=BF16):
    m, k = a.shape
    n = b.shape[1]
    return pl.pallas_call(
        _matmul_kernel,
        grid=(m // tm, n // tn),
        in_specs=[pl.BlockSpec((tm, k), lambda i, j: (i, 0)),
                  pl.BlockSpec((k, tn), lambda i, j: (0, j))],
        out_specs=pl.BlockSpec((tm, tn), lambda i, j: (i, j)),
        out_shape=jax.ShapeDtypeStruct((m, n), out_dtype),
        compiler_params=_params("parallel", "parallel"),
        name="matmul",
    )(a, b)


def _inproj_kernel(a_ref, b_ref, c_ref, s1_ref, s2_ref, o_ref, vt_ref, *, rope_lo, rope_hi, vt_col):
    j = pl.program_id(1)
    acc = _dot(a_ref[...].astype(BF16), b_ref[...])
    o_ref[...] = acc.astype(o_ref.dtype)

    @pl.when(j == vt_col)
    def _():
        tk = vt_ref.shape[2]
        for kb in range(vt_ref.shape[0]):
            vt_ref[kb] = acc[kb * tk:(kb + 1) * tk, :].T.astype(vt_ref.dtype)

    @pl.when(jnp.logical_and(j >= rope_lo, j < rope_hi))
    def _():
        cos, sin_up, sin_dn = c_ref[...], s1_ref[...], s2_ref[...]
        for cc in range(acc.shape[1] // LANES):
            t = acc[:, cc * LANES:(cc + 1) * LANES]
            up = pltpu.roll(t, LANES - ROPE_DIM // 2, axis=1)
            dn = pltpu.roll(t, ROPE_DIM // 2, axis=1)
            o_ref[:, cc * LANES:(cc + 1) * LANES] = (t * cos + up * sin_up + dn * sin_dn).astype(o_ref.dtype)


def _rope_tables(seq):
    half = ROPE_DIM // 2
    inv_freq = jnp.power(ROPE_THETA, -jnp.arange(half, dtype=F32) * 2.0 / ROPE_DIM)
    ang = jnp.arange(seq).astype(F32)[:, None] * inv_freq[None, :]
    cos, sin = jnp.cos(ang), jnp.sin(ang)
    ones = jnp.ones((seq, DIFF_QK_DIM - ROPE_DIM), F32)
    zeros = jnp.zeros((seq, DIFF_QK_DIM - ROPE_DIM), F32)
    zh = jnp.zeros((seq, half), F32)
    c64 = jnp.concatenate([cos, cos, ones], axis=1)
    up64 = jnp.concatenate([-sin, zh, zeros], axis=1)
    dn64 = jnp.concatenate([zh, sin, zeros], axis=1)
    rep = LANES // DIFF_QK_DIM
    return jnp.tile(c64, (1, rep)), jnp.tile(up64, (1, rep)), jnp.tile(dn64, (1, rep))


def _inproj(x, w_in, tm, tn, tk, rope_cols, vt_cols):
    seq, k = x.shape
    n = w_in.shape[1]
    assert vt_cols == (n - tn, n)
    cos, sin_up, sin_dn = _rope_tables(seq)
    tab = pl.BlockSpec((tm, LANES), lambda i, j: (i, 0))
    kern = functools.partial(_inproj_kernel, rope_lo=rope_cols[0] // tn, rope_hi=rope_cols[1] // tn,
                             vt_col=vt_cols[0] // tn)
    return pl.pallas_call(
        kern,
        grid=(seq // tm, n // tn),
        in_specs=[pl.BlockSpec((tm, k), lambda i, j: (i, 0)),
                  pl.BlockSpec((k, tn), lambda i, j: (0, j)),
                  tab, tab, tab],
        out_specs=[pl.BlockSpec((tm, tn), lambda i, j: (i, j)),
                   pl.BlockSpec((tm // tk, tn, tk), lambda i, j: (i, 0, 0))],
        out_shape=[jax.ShapeDtypeStruct((seq, n), BF16),
                   jax.ShapeDtypeStruct((seq // tk, tn, tk), BF16)],
        compiler_params=_params("parallel", "arbitrary"),
        name="inproj_rope",
    )(x, w_in, cos, sin_up, sin_dn)


def _sb_kernel(qa_ref, ka_ref, va_ref, qb_ref, kb_ref, vb_ref, g_ref, oa_ref, ob_ref, *, tq, scale, head_stride):
    h = pl.program_id(0)
    i = pl.program_id(1)
    row = lax.broadcasted_iota(jnp.int32, (tq, tq), 0)
    col = lax.broadcasted_iota(jnp.int32, (tq, tq), 1)
    later = (row > col).astype(BF16)
    before = col < row

    def block_terms(q, k_ref, v_ref, kb, masked):
        start = pl.multiple_of(kb * tq, tq)
        k = k_ref[pl.ds(start, tq), :]
        z = _dot_nt(q, k) * scale
        sp = jnp.log(1.0 + jnp.exp(-jnp.abs(z)))
        log_beta = jnp.minimum(z, 0.0) - sp
        log_keep = -jnp.maximum(z, 0.0) - sp
        if masked:
            log_keep = jnp.where(before, log_keep, 0.0)
        hi = log_keep.astype(BF16)
        lo = (log_keep - hi.astype(F32)).astype(BF16)
        stick = _dot(hi, later) + _dot(lo, later)
        return log_beta + stick, jnp.sum(log_keep, axis=1, keepdims=True), v_ref[pl.ds(start, tq), :]

    def block_apply(terms, acc, c, masked, valid=None):
        log_w, keep_sum, v = terms
        w = jnp.exp(log_w + c)
        if masked:
            w = jnp.where(before, w, 0.0)
        if valid is not None:
            w = jnp.where(valid, w, 0.0)
            keep_sum = jnp.where(valid, keep_sum, 0.0)
        return acc + _dot(w.astype(BF16), v), c + keep_sum

    acc0 = jnp.zeros((tq, HEAD_DIM), F32)
    c0 = jnp.zeros((tq, 1), F32)
    heads = ((qa_ref[...], ka_ref, va_ref), (qb_ref[...], kb_ref, vb_ref))
    first = [(block_terms(q, k_ref, v_ref, i, True), block_terms(q, k_ref, v_ref, jnp.maximum(i - 1, 0), False))
             for q, k_ref, v_ref in heads]
    state = []
    for diag, prev in first:
        acc, c = block_apply(diag, acc0, c0, True)
        state.append(block_apply(prev, acc, c, False, valid=i >= 1))

    def live(carry):
        jj, _, _, cmax = carry
        return jnp.logical_and(jj < i, cmax > SB_DEAD_LOG)

    for (q, k_ref, v_ref), (acc, c), o_ref, head in zip(heads, state, (oa_ref, ob_ref), (h, h + head_stride)):
        def older(carry, q=q, k_ref=k_ref, v_ref=v_ref):
            jj, acc, c, _ = carry
            acc, c = block_apply(block_terms(q, k_ref, v_ref, i - 1 - jj, False), acc, c, False)
            return jj + 1, acc, c, jnp.max(c)

        _, acc, c, _ = lax.while_loop(live, older, (jnp.int32(1), acc, c, jnp.max(c)))
        g = g_ref[pl.ds(head, 1), :]
        ms = jnp.mean(acc * acc, axis=1, keepdims=True)
        o_ref[...] = (acc * lax.rsqrt(ms + RMS_EPS) * g).astype(o_ref.dtype)


def _sb_attention(proj, gain, tq):
    seq = proj.shape[0]
    half = SB_HEADS // 2
    kern = functools.partial(_sb_kernel, tq=tq, scale=1.0 / math.sqrt(HEAD_DIM), head_stride=half)
    specs = []
    for off in (0, half):
        specs += [pl.BlockSpec((tq, HEAD_DIM), lambda h, i, off=off: (i, off + h)),
                  pl.BlockSpec((seq, HEAD_DIM), lambda h, i, off=off: (0, SB_HEADS + off + h)),
                  pl.BlockSpec((seq, HEAD_DIM), lambda h, i, off=off: (0, 2 * SB_HEADS + off + h))]
    out = jax.ShapeDtypeStruct((seq, half * HEAD_DIM), BF16)
    return pl.pallas_call(
        kern,
        grid=(half, seq // tq),
        in_specs=specs + [pl.BlockSpec((SB_HEADS, HEAD_DIM), lambda h, i: (0, 0))],
        out_specs=[pl.BlockSpec((tq, HEAD_DIM), lambda h, i: (i, h))] * 2,
        out_shape=[out, out],
        compiler_params=_params("parallel", "arbitrary"),
        name="stickbreak_attn",
    )(*([proj] * 6), gain)


def _diff_kernel(q_ref, k_ref, vt_ref, lam_ref, g_ref, o_ref, s0_ref, s1_ref, p0_ref, p1_ref, acc_ref,
                 *, tq, lambda_init):
    i = pl.program_id(1)
    s_refs, p_refs = (s0_ref, s1_ref), (p0_ref, p1_ref)
    q = q_ref[...] * jnp.asarray(1.0 / math.sqrt(DIFF_QK_DIM), BF16)
    lane = lax.broadcasted_iota(jnp.int32, (tq, HEAD_DIM), 1)
    zero = jnp.zeros_like(q)
    qmaps = (jnp.where(lane < DIFF_QK_DIM, q, zero),
             jnp.where(lane >= DIFF_QK_DIM, q, zero))
    ones = jnp.ones((SUM_ROWS, tq), BF16)

    def scores(kb, mp):
        k = k_ref[pl.ds(pl.multiple_of(kb * tq, tq), tq), :]
        return _dot_nt(k, qmaps[mp])

    def softmax_step(s, m, slot, mp):
        m_new = jnp.maximum(m, jnp.max(s, axis=0, keepdims=True))
        p_refs[slot][mp] = jnp.exp(s - m_new).astype(BF16)
        return m_new, jnp.exp(m - m_new)

    def accumulate(kb, alpha, slot, mp):
        v_aug = jnp.concatenate([vt_ref[kb], ones], axis=0)
        acc_ref[mp] = alpha * acc_ref[mp] + _dot(v_aug, p_refs[slot][mp])

    key = lax.broadcasted_iota(jnp.int32, (tq, tq), 0)
    qry = lax.broadcasted_iota(jnp.int32, (tq, tq), 1)
    causal = key <= qry
    m0 = jnp.full((1, tq), NEG_INF, F32)
    acc_ref[...] = jnp.zeros_like(acc_ref)
    stats = []
    for mp in range(2):
        s_refs[1][mp] = scores(0, mp)
        stats.extend(softmax_step(jnp.where(causal, scores(i, mp), NEG_INF), m0, 0, mp))
    stats = tuple(stats)

    def step(t, carry, cur):
        nxt = 1 - cur
        vb = jnp.where(t == 0, i, t - 1)
        out = []
        for mp in range(2):
            m, alpha = carry[2 * mp], carry[2 * mp + 1]
            s_refs[cur][mp] = scores(jnp.minimum(t + 1, i - 1), mp)
            accumulate(vb, alpha, cur, mp)
            out.extend(softmax_step(s_refs[nxt][mp], m, nxt, mp))
        return tuple(out)

    stats = lax.fori_loop(0, i // 2, lambda u, c: step(2 * u + 1, step(2 * u, c, 0), 1), stats)
    odd = i % 2 == 1
    stats = lax.cond(odd, lambda c: step(i - 1, c, 0), lambda c: c, stats)
    last = jnp.maximum(i - 1, 0)
    for slot in range(2):
        @pl.when((i % 2) == slot)
        def _():
            for mp in range(2):
                accumulate(last, stats[2 * mp + 1], slot, mp)

    lf = lam_ref[...]
    lam = (jnp.exp(jnp.sum(lf[0:1, :] * lf[1:2, :], axis=1, keepdims=True))
           - jnp.exp(jnp.sum(lf[2:3, :] * lf[3:4, :], axis=1, keepdims=True)) + lambda_init)
    o1 = acc_ref[0, :HEAD_DIM, :] / acc_ref[0, HEAD_DIM:HEAD_DIM + 1, :]
    o2 = acc_ref[1, :HEAD_DIM, :] / acc_ref[1, HEAD_DIM:HEAD_DIM + 1, :]
    d = o1 - lam * o2
    ms = jnp.mean(d * d, axis=0, keepdims=True)
    dn = (d * lax.rsqrt(ms + RMS_EPS)).T
    o_ref[...] = (dn * g_ref[...] * (1.0 - lambda_init)).astype(o_ref.dtype)


def _diff_attention(proj, vt, df_lambda, gain, tq, lambda_init):
    seq = proj.shape[0]
    qcol = 3 * SB_HEADS
    kcol = qcol + DIFF_HEADS
    kern = functools.partial(_diff_kernel, tq=tq, lambda_init=lambda_init)
    return pl.pallas_call(
        kern,
        grid=(DIFF_HEADS, seq // tq),
        in_specs=[pl.BlockSpec((tq, HEAD_DIM), lambda h, i: (i, qcol + h)),
                  pl.BlockSpec((seq, HEAD_DIM), lambda h, i: (0, kcol + h)),
                  pl.BlockSpec((seq // tq, HEAD_DIM, tq), lambda h, i: (0, h, 0)),
                  pl.BlockSpec((4, DIFF_QK_DIM), lambda h, i: (0, 0)),
                  pl.BlockSpec((1, HEAD_DIM), lambda h, i: (0, 0))],
        out_specs=pl.BlockSpec((tq, HEAD_DIM), lambda h, i: (i, h)),
        out_shape=jax.ShapeDtypeStruct((seq, DIFF_HEADS * HEAD_DIM), BF16),
        scratch_shapes=[pltpu.VMEM((2, tq, tq), F32), pltpu.VMEM((2, tq, tq), F32),
                        pltpu.VMEM((2, tq, tq), BF16), pltpu.VMEM((2, tq, tq), BF16),
                        pltpu.VMEM((2, HEAD_DIM + SUM_ROWS, tq), F32)],
        compiler_params=_params("parallel", "arbitrary"),
        name="diff_attn",
    )(proj, proj, vt, df_lambda, gain)


def _layer_norm(r, g, b):
    mu = jnp.mean(r, axis=1, keepdims=True)
    d = r - mu
    var = jnp.mean(d * d, axis=1, keepdims=True)
    return d * lax.rsqrt(var + LN_EPS) * g + b


def _proj_ln_kernel(*refs, n_parts, emit_bf16):
    a_refs = refs[:n_parts]
    w_ref, x_ref, g_ref, b_ref, wn_ref, o_ref = refs[n_parts:n_parts + 6]
    rest = refs[n_parts + 6:]
    y = None
    off = 0
    for a_ref in a_refs:
        kk = a_ref.shape[1]
        part = _dot(a_ref[...], w_ref[off:off + kk, :])
        y = part if y is None else y + part
        off += kk
    out = _layer_norm(DEEPNORM_ALPHA * x_ref[...] + y, g_ref[...], b_ref[...])
    o_ref[...] = out
    out_bf16 = out.astype(BF16)
    if emit_bf16:
        rest[0][...] = out_bf16
    rest[-1][...] = _dot(out_bf16, wn_ref[...]).astype(BF16)


def _proj_ln(parts, w, x, gain, bias, w_next, tm, emit_bf16):
    seq, d = x.shape
    dn = w_next.shape[1]
    kern = functools.partial(_proj_ln_kernel, n_parts=len(parts), emit_bf16=emit_bf16)
    row = lambda i: (i, 0)
    fixed = lambda i: (0, 0)
    out_specs = [pl.BlockSpec((tm, d), row)]
    out_shape = [jax.ShapeDtypeStruct((seq, d), F32)]
    if emit_bf16:
        out_specs.append(pl.BlockSpec((tm, d), row))
        out_shape.append(jax.ShapeDtypeStruct((seq, d), BF16))
    out_specs.append(pl.BlockSpec((tm, dn), row))
    out_shape.append(jax.ShapeDtypeStruct((seq, dn), BF16))
    return pl.pallas_call(
        kern,
        grid=(seq // tm,),
        in_specs=[pl.BlockSpec((tm, p.shape[1]), row) for p in parts] + [
            pl.BlockSpec(w.shape, fixed),
            pl.BlockSpec((tm, d), row),
            pl.BlockSpec((1, d), fixed),
            pl.BlockSpec((1, d), fixed),
            pl.BlockSpec(w_next.shape, fixed)],
        out_specs=out_specs,
        out_shape=out_shape,
        compiler_params=_params("parallel"),
        name="proj_residual_ln",
    )(*parts, w, x, gain, bias, w_next)


def _xattn_kernel(q_ref, kv_ref, o_ref, *, d_model, head_dim):
    scale = 1.0 / math.sqrt(head_dim)
    for hh in range(d_model // head_dim):
        lo = hh * head_dim
        q = q_ref[:, lo:lo + head_dim]
        k = kv_ref[:, lo:lo + head_dim]
        v = kv_ref[:, d_model + lo:d_model + lo + head_dim]
        s = _dot_nt(q, k) * scale
        m = jnp.max(s, axis=1, keepdims=True)
        p = jnp.exp(s - m)
        p = p / jnp.sum(p, axis=1, keepdims=True)
        o_ref[:, lo:lo + head_dim] = _dot(p.astype(BF16), v).astype(o_ref.dtype)


def _xattn(q, kv, tm):
    seq, d = q.shape
    kern = functools.partial(_xattn_kernel, d_model=d, head_dim=d // MEM_HEADS)
    return pl.pallas_call(
        kern,
        grid=(seq // tm,),
        in_specs=[pl.BlockSpec((tm, d), lambda i: (i, 0)),
                  pl.BlockSpec(kv.shape, lambda i: (0, 0))],
        out_specs=pl.BlockSpec((tm, d), lambda i: (i, 0)),
        out_shape=jax.ShapeDtypeStruct((seq, d), BF16),
        compiler_params=_params("parallel"),
        name="memory_xattn",
    )(q, kv)


def _top16(s, vals_ref, exact_ties):
    def step(k, carry):
        work, rank = carry
        m = jnp.max(work, axis=0, keepdims=True)
        sel = work == m
        if exact_ties:
            iota = lax.broadcasted_iota(jnp.int32, s.shape, 0).astype(F32)
            sel = iota == jnp.min(jnp.where(sel, iota, float(s.shape[0])), axis=0, keepdims=True)
        vals_ref[pl.ds(k, 1), :] = m
        return jnp.where(sel, NEG_INF, work), jnp.where(sel, lax.convert_element_type(k, F32), rank)

    _, rank = lax.fori_loop(0, PEER_TOPK, step, (s, jnp.full(s.shape, float(PEER_TOPK), F32)))
    tie_free = float(sum(range(PEER_TOPK)) + PEER_TOPK * (s.shape[0] - PEER_TOPK))
    return rank, tie_free - jnp.sum(rank, axis=0, keepdims=True)


CAND_KEEP = tuple(PEER_TOPK // (k1 + 1) for k1 in range(PEER_TOPK))
CAND_ROWS = -(-sum(CAND_KEEP) // 8) * 8


def _route_kernel(pq_ref, keys_ref, e2_ref, r2_ref, w1_ref, n1_ref, v1_ref, v2_ref, vc_ref, cand_ref):
    q = pq_ref[...]
    s1 = _dot_nt(keys_ref[0], q[:, :N_KEYS])
    s2 = _dot_nt(keys_ref[1], q[:, N_KEYS:])

    def route(exact_ties):
        rank1, deficit1 = _top16(s1, v1_ref, exact_ties)
        rank2, deficit2 = _top16(s2, v2_ref, exact_ties)
        cand_ref[...] = jnp.full(cand_ref.shape, NEG_INF, F32)
        row = lax.broadcasted_iota(jnp.int32, cand_ref.shape, 0)
        seg = jnp.zeros(cand_ref.shape, F32)
        off = 0
        for k1, keep in enumerate(CAND_KEEP):
            cand_ref[off:off + keep, :] = v1_ref[k1:k1 + 1, :] + v2_ref[0:keep, :]
            off += keep
            seg = seg + jnp.where(row >= off, 1.0, 0.0)
        crank, deficit3 = _top16(cand_ref[...], vc_ref, exact_ties)
        top = vc_ref[...]
        z = jnp.sum(jnp.exp(top - top[0:1, :]), axis=0, keepdims=True)
        chosen = jnp.where(crank < float(PEER_TOPK), 1.0, 0.0)
        rank1_packed = rank1.astype(BF16)
        n1 = jnp.zeros(s1.shape, BF16)
        for k1 in range(PEER_TOPK):
            cnt = jnp.sum(jnp.where(seg == float(k1), chosen, 0.0), axis=0, keepdims=True)
            cnt = jnp.tile(jnp.broadcast_to(cnt, (BF16_SUBLANES, cnt.shape[1])).astype(BF16),
                           (N_KEYS // BF16_SUBLANES, 1))
            n1 = jnp.where(rank1_packed == jnp.asarray(k1, BF16), cnt, n1)
        e2 = jnp.exp(s2 - v2_ref[0:1, :]).astype(e2_ref.dtype)
        w1 = jnp.exp(s1 - v1_ref[0:1, :]) / z
        tied = jnp.max(jnp.maximum(jnp.maximum(deficit1, deficit2), deficit3)) > 0.0
        return (e2, rank2.astype(r2_ref.dtype), w1, n1.astype(F32)), tied

    tables, tied = route(False)
    tables = lax.cond(tied, lambda _: route(True)[0], lambda t: t, tables)
    e2_ref[0], r2_ref[0], w1_ref[0], n1_ref[0] = tables


def _peer_route(pq, keys, tt):
    seq = pq.shape[0]
    qd = 2 * N_KEYS
    tab = pl.BlockSpec((1, N_KEYS, tt), lambda i, h: (h, 0, i))
    shape = jax.ShapeDtypeStruct((PEER_HEADS, N_KEYS, seq), F32)
    packed = jax.ShapeDtypeStruct((PEER_HEADS, N_KEYS, seq), BF16)
    return pl.pallas_call(
        _route_kernel,
        grid=(seq // tt, PEER_HEADS),
        in_specs=[pl.BlockSpec((tt, qd), lambda i, h: (i, h)),
                  pl.BlockSpec(keys.shape, lambda i, h: (0, 0, 0))],
        out_specs=[tab, tab, tab, tab],
        out_shape=[packed, packed, shape, shape],
        scratch_shapes=[pltpu.VMEM((PEER_TOPK, tt), F32),
                        pltpu.VMEM((PEER_TOPK, tt), F32),
                        pltpu.VMEM((PEER_TOPK, tt), F32),
                        pltpu.VMEM((CAND_ROWS, tt), F32)],
        compiler_params=_params("parallel", "parallel"),
        name="peer_route",
    )(pq, keys)


def _peer_kernel(xb_ref, u_ref, v_ref, e2_ref, r2_ref, w1_ref, n1_ref, x_ref, g_ref, b_ref,
                 o_ref, a_ref, gate_ref, *, keys_per_step, n_chunks):
    cc = pl.program_id(1)
    cur, prev = cc % 2, (cc + 1) % 2

    @pl.when(cc == 0)
    def _():
        o_ref[...] = jnp.zeros_like(o_ref)
        a_ref[...] = jnp.zeros_like(a_ref)
        gate_ref[...] = jnp.zeros_like(gate_ref)

    gated = (a_ref[prev].astype(F32) * gate_ref[prev]).astype(BF16)
    o_ref[...] += _dot(gated, v_ref[...])

    hidden = _dot(xb_ref[...], u_ref[...])
    a_ref[cur] = (0.5 * hidden * (1.0 + lax.erf(hidden * math.sqrt(0.5)))).astype(BF16)

    chunk = jnp.minimum(cc, n_chunks - 1)
    zero = jnp.zeros(r2_ref.shape[1:], BF16)

    def rows(row):
        tile = jnp.broadcast_to(row, (BF16_SUBLANES, row.shape[1])).astype(BF16)
        return jnp.tile(tile, (N_KEYS // BF16_SUBLANES, 1))

    for part in range(keys_per_step):
        c = chunk * keys_per_step + part
        gt = None
        for h in range(PEER_HEADS):
            n1 = rows(n1_ref[h, pl.ds(c, 1), :])
            w1 = rows(w1_ref[h, pl.ds(c, 1), :])
            term = jnp.where(r2_ref[h] < n1, e2_ref[h] * w1, zero)
            gt = term if gt is None else gt + term
        gate_ref[cur, :, part * N_KEYS:(part + 1) * N_KEYS] = gt.astype(F32).T

    @pl.when(cc == n_chunks)
    def _():
        o_ref[...] = _layer_norm(DEEPNORM_ALPHA * x_ref[...] + o_ref[...], g_ref[...], b_ref[...])


def _peer_dense(xb, x, ut, v, tables, gain, bias, tt, keys_per_step):
    seq, d = x.shape
    te = keys_per_step * N_KEYS
    n_chunks = v.shape[0] // te
    tab = pl.BlockSpec((PEER_HEADS, N_KEYS, tt), lambda i, c: (0, 0, i))
    row = lambda i, c: (i, 0)
    fixed = lambda i, c: (0, 0)
    kern = functools.partial(_peer_kernel, keys_per_step=keys_per_step, n_chunks=n_chunks)
    return pl.pallas_call(
        kern,
        grid=(seq // tt, n_chunks + 1),
        in_specs=[pl.BlockSpec((tt, d), row),
                  pl.BlockSpec((d, te), lambda i, c: (0, jnp.minimum(c, n_chunks - 1))),
                  pl.BlockSpec((te, d), lambda i, c: (jnp.maximum(c - 1, 0), 0)),
                  tab, tab, tab, tab,
                  pl.BlockSpec((tt, d), row),
                  pl.BlockSpec((1, d), fixed),
                  pl.BlockSpec((1, d), fixed)],
        out_specs=pl.BlockSpec((tt, d), row),
        out_shape=jax.ShapeDtypeStruct((seq, d), F32),
        scratch_shapes=[pltpu.VMEM((2, tt, te), BF16), pltpu.VMEM((2, tt, te), F32)],
        compiler_params=_params("parallel", "arbitrary"),
        name="peer_dense",
    )(xb, ut, v, *tables, x, gain, bias)


def _tile(n, want):
    return min(n, want)


def kernel(x, mem, w_in, sb_norm_gain, df_lambda, df_subln_gain, w_o, ln1_gain, ln1_bias, w_mq, w_mkv, w_mo, ln2_gain, ln2_bias, w_pq, peer_sub_keys, peer_u, peer_v, ln3_gain, ln3_bias):
    b, seq, d = x.shape
    assert b == 1 and w_in.shape[0] == DEPTH
    x2d = x.reshape(seq, d)
    mem2d = mem.reshape(mem.shape[1], d).astype(BF16)
    for l in range(DEPTH):
        lambda_init = 0.8 - 0.6 * math.exp(-0.3 * l)
        row1 = lambda a: a.reshape(1, -1)
        tq = _tile(seq, ATTN_BLOCK)
        width = SB_HEADS * HEAD_DIM
        proj, vt = _inproj(x2d, w_in[l].astype(BF16), _tile(seq, PROJ_ROWS), width, tq,
                           rope_cols=(3 * width, 5 * width), vt_cols=(5 * width, 6 * width))
        sb_lo, sb_hi = _sb_attention(proj, sb_norm_gain[l], tq)
        df_o = _diff_attention(proj, vt, df_lambda[l], row1(df_subln_gain[l]), tq, lambda_init)
        x1, q = _proj_ln([sb_lo, sb_hi, df_o], w_o[l].astype(BF16), x2d, row1(ln1_gain[l]), row1(ln1_bias[l]),
                         w_mq[l].astype(BF16), _tile(seq, LN_ROWS), emit_bf16=False)
        kv = _matmul(mem2d, w_mkv[l].astype(BF16), mem2d.shape[0], PROJ_COLS)
        xa = _xattn(q, kv, _tile(seq, LN_ROWS))
        x2, x2b, pq = _proj_ln([xa], w_mo[l].astype(BF16), x1, row1(ln2_gain[l]), row1(ln2_bias[l]),
                               w_pq[l].astype(BF16), _tile(seq, LN_ROWS), emit_bf16=True)
        tables = _peer_route(pq, peer_sub_keys[l].astype(BF16), _tile(seq, ROUTE_TOKENS))
        x2d = _peer_dense(x2b, x2, peer_u[l].T.astype(BF16), peer_v[l].astype(BF16), tables,
                          row1(ln3_gain[l]), row1(ln3_bias[l]), _tile(seq, PEER_TOKENS), PEER_KEYS_PER_STEP)
    return x2d.reshape(b, seq, d)
```

```python
import functools
import math

import jax
import jax.numpy as jnp
from jax import lax
from jax.experimental import pallas as pl
from jax.experimental.pallas import tpu as pltpu

F32 = jnp.float32
BF16 = jnp.bfloat16

LANES = 128
BF16_SUBLANES = 16
MXU_WIDTH = 256
VMEM_LIMIT_BYTES = 56 * 1024 * 1024

ATTN_BLOCK = 256
PROJ_ROWS = 1024
PROJ_COLS = 1024
LN_ROWS = 256
ROUTE_TOKENS = 256
PEER_TOKENS = 512
PEER_KEYS_PER_STEP = 4

DEPTH = 1
SB_HEADS = 8
DIFF_HEADS = 8
HEAD_DIM = 128
DIFF_QK_DIM = 64
ROPE_DIM = 16
ROPE_THETA = 500000.0
MEM_HEADS = 4
PEER_HEADS = 8
N_KEYS = 128
PEER_TOPK = 16
LN_EPS = 1e-5
RMS_EPS = 1e-6
DEEPNORM_ALPHA = (2 * DEPTH) ** 0.25
NEG_INF = float("-inf")
SB_DEAD_LOG = -105.0
SUM_ROWS = 16


def _dot(a, b):
    return jnp.dot(a, b, preferred_element_type=F32)


def _dot_nt(a, b):
    return lax.dot_general(a, b, (((1,), (1,)), ((), ())), preferred_element_type=F32)


def _params(*semantics):
    return pltpu.CompilerParams(dimension_semantics=semantics, vmem_limit_bytes=VMEM_LIMIT_BYTES)


def _matmul_kernel(a_ref, b_ref, o_ref):
    o_ref[...] = _dot(a_ref[...], b_ref[...]).astype(o_ref.dtype)


def _matmul(a, b, tm, tn, out_dtype=BF16):
    m, k = a.shape
    n = b.shape[1]
    return pl.pallas_call(
        _matmul_kernel,
        grid=(m // tm, n // tn),
        in_specs=[pl.BlockSpec((tm, k), lambda i, j: (i, 0)),
                  pl.BlockSpec((k, tn), lambda i, j: (0, j))],
        out_specs=pl.BlockSpec((tm, tn), lambda i, j: (i, j)),
        out_shape=jax.ShapeDtypeStruct((m, n), out_dtype),
        compiler_params=_params("parallel", "parallel"),
        name="matmul",
    )(a, b)


def _inproj_kernel(a_ref, b_ref, c_ref, s1_ref, s2_ref, o_ref, vt_ref, *, rope_lo, rope_hi, vt_col):
    j = pl.program_id(1)
    acc = _dot(a_ref[...].astype(BF16), b_ref[...])
    o_ref[...] = acc.astype(o_ref.dtype)

    @pl.when(j == vt_col)
    def _():
        tk = vt_ref.shape[2]
        for kb in range(vt_ref.shape[0]):
            vt_ref[kb] = acc[kb * tk:(kb + 1) * tk, :].T.astype(vt_ref.dtype)

    @pl.when(jnp.logical_and(j >= rope_lo, j < rope_hi))
    def _():
        cos, sin_up, sin_dn = c_ref[...], s1_ref[...], s2_ref[...]
        for cc in range(acc.shape[1] // LANES):
            t = acc[:, cc * LANES:(cc + 1) * LANES]
            up = pltpu.roll(t, LANES - ROPE_DIM // 2, axis=1)
            dn = pltpu.roll(t, ROPE_DIM // 2, axis=1)
            o_ref[:, cc * LANES:(cc + 1) * LANES] = (t * cos + up * sin_up + dn * sin_dn).astype(o_ref.dtype)


def _rope_tables(seq):
    half = ROPE_DIM // 2
    inv_freq = jnp.power(ROPE_THETA, -jnp.arange(half, dtype=F32) * 2.0 / ROPE_DIM)
    ang = jnp.arange(seq).astype(F32)[:, None] * inv_freq[None, :]
    cos, sin = jnp.cos(ang), jnp.sin(ang)
    ones = jnp.ones((seq, DIFF_QK_DIM - ROPE_DIM), F32)
    zeros = jnp.zeros((seq, DIFF_QK_DIM - ROPE_DIM), F32)
    zh = jnp.zeros((seq, half), F32)
    c64 = jnp.concatenate([cos, cos, ones], axis=1)
    up64 = jnp.concatenate([-sin, zh, zeros], axis=1)
    dn64 = jnp.concatenate([zh, sin, zeros], axis=1)
    rep = LANES // DIFF_QK_DIM
    return jnp.tile(c64, (1, rep)), jnp.tile(up64, (1, rep)), jnp.tile(dn64, (1, rep))


def _inproj(x, w_in, tm, tn, tk, rope_cols, vt_cols):
    seq, k = x.shape
    n = w_in.shape[1]
    assert vt_cols == (n - tn, n)
    cos, sin_up, sin_dn = _rope_tables(seq)
    tab = pl.BlockSpec((tm, LANES), lambda i, j: (i, 0))
    kern = functools.partial(_inproj_kernel, rope_lo=rope_cols[0] // tn, rope_hi=rope_cols[1] // tn,
                             vt_col=vt_cols[0] // tn)
    return pl.pallas_call(
        kern,
        grid=(seq // tm, n // tn),
        in_specs=[pl.BlockSpec((tm, k), lambda i, j: (i, 0)),
                  pl.BlockSpec((k, tn), lambda i, j: (0, j)),
                  tab, tab, tab],
        out_specs=[pl.BlockSpec((tm, tn), lambda i, j: (i, j)),
                   pl.BlockSpec((tm // tk, tn, tk), lambda i, j: (i, 0, 0))],
        out_shape=[jax.ShapeDtypeStruct((seq, n), BF16),
                   jax.ShapeDtypeStruct((seq // tk, tn, tk), BF16)],
        compiler_params=_params("parallel", "arbitrary"),
        name="inproj_rope",
    )(x, w_in, cos, sin_up, sin_dn)


def _sb_kernel(qa_ref, ka_ref, va_ref, qb_ref, kb_ref, vb_ref, g_ref, oa_ref, ob_ref, *, tq, scale, head_stride):
    h = pl.program_id(0)
    i = pl.program_id(1)
    row = lax.broadcasted_iota(jnp.int32, (tq, tq), 0)
    col = lax.broadcasted_iota(jnp.int32, (tq, tq), 1)
    later = (row > col).astype(BF16)
    before = col < row

    def block_terms(q, k_ref, v_ref, kb, masked):
        start = pl.multiple_of(kb * tq, tq)
        k = k_ref[pl.ds(start, tq), :]
        z = _dot_nt(q, k) * scale
        sp = jnp.log(1.0 + jnp.exp(-jnp.abs(z)))
        log_beta = jnp.minimum(z, 0.0) - sp
        log_keep = -jnp.maximum(z, 0.0) - sp
        if masked:
            log_keep = jnp.where(before, log_keep, 0.0)
        hi = log_keep.astype(BF16)
        lo = (log_keep - hi.astype(F32)).astype(BF16)
        stick = _dot(hi, later) + _dot(lo, later)
        return log_beta + stick, jnp.sum(log_keep, axis=1, keepdims=True), v_ref[pl.ds(start, tq), :]

    def block_apply(terms, acc, c, masked, valid=None):
        log_w, keep_sum, v = terms
        w = jnp.exp(log_w + c)
        if masked:
            w = jnp.where(before, w, 0.0)
        if valid is not None:
            w = jnp.where(valid, w, 0.0)
            keep_sum = jnp.where(valid, keep_sum, 0.0)
        return acc + _dot(w.astype(BF16), v), c + keep_sum

    acc0 = jnp.zeros((tq, HEAD_DIM), F32)
    c0 = jnp.zeros((tq, 1), F32)
    heads = ((qa_ref[...], ka_ref, va_ref), (qb_ref[...], kb_ref, vb_ref))
    first = [(block_terms(q, k_ref, v_ref, i, True), block_terms(q, k_ref, v_ref, jnp.maximum(i - 1, 0), False))
             for q, k_ref, v_ref in heads]
    state = []
    for diag, prev in first:
        acc, c = block_apply(diag, acc0, c0, True)
        state.append(block_apply(prev, acc, c, False, valid=i >= 1))

    def live(carry):
        jj, _, _, cmax = carry
        return jnp.logical_and(jj < i, cmax > SB_DEAD_LOG)

    for (q, k_ref, v_ref), (acc, c), o_ref, head in zip(heads, state, (oa_ref, ob_ref), (h, h + head_stride)):
        def older(carry, q=q, k_ref=k_ref, v_ref=v_ref):
            jj, acc, c, _ = carry
            acc, c = block_apply(block_terms(q, k_ref, v_ref, i - 1 - jj, False), acc, c, False)
            return jj + 1, acc, c, jnp.max(c)

        _, acc, c, _ = lax.while_loop(live, older, (jnp.int32(1), acc, c, jnp.max(c)))
        g = g_ref[pl.ds(head, 1), :]
        ms = jnp.mean(acc * acc, axis=1, keepdims=True)
        o_ref[...] = (acc * lax.rsqrt(ms + RMS_EPS) * g).astype(o_ref.dtype)


def _sb_attention(proj, gain, tq):
    seq = proj.shape[0]
    half = SB_HEADS // 2
    kern = functools.partial(_sb_kernel, tq=tq, scale=1.0 / math.sqrt(HEAD_DIM), head_stride=half)
    specs = []
    for off in (0, half):
        specs += [pl.BlockSpec((tq, HEAD_DIM), lambda h, i, off=off: (i, off + h)),
                  pl.BlockSpec((seq, HEAD_DIM), lambda h, i, off=off: (0, SB_HEADS + off + h)),
                  pl.BlockSpec((seq, HEAD_DIM), lambda h, i, off=off: (0, 2 * SB_HEADS + off + h))]
    out = jax.ShapeDtypeStruct((seq, half * HEAD_DIM), BF16)
    return pl.pallas_call(
        kern,
        grid=(half, seq // tq),
        in_specs=specs + [pl.BlockSpec((SB_HEADS, HEAD_DIM), lambda h, i: (0, 0))],
        out_specs=[pl.BlockSpec((tq, HEAD_DIM), lambda h, i: (i, h))] * 2,
        out_shape=[out, out],
        compiler_params=_params("parallel", "arbitrary"),
        name="stickbreak_attn",
    )(*([proj] * 6), gain)


def _diff_kernel(q_ref, k_ref, vt_ref, lam_ref, g_ref, o_ref, s0_ref, s1_ref, p0_ref, p1_ref, acc_ref,
                 *, tq, lambda_init):
    i = pl.program_id(1)
    s_refs, p_refs = (s0_ref, s1_ref), (p0_ref, p1_ref)
    q = q_ref[...] * jnp.asarray(1.0 / math.sqrt(DIFF_QK_DIM), BF16)
    lane = lax.broadcasted_iota(jnp.int32, (tq, HEAD_DIM), 1)
    zero = jnp.zeros_like(q)
    qmaps = (jnp.where(lane < DIFF_QK_DIM, q, zero),
             jnp.where(lane >= DIFF_QK_DIM, q, zero))
    ones = jnp.ones((SUM_ROWS, tq), BF16)

    def scores(kb, mp):
        k = k_ref[pl.ds(pl.multiple_of(kb * tq, tq), tq), :]
        return _dot_nt(k, qmaps[mp])

    def softmax_step(s, m, slot, mp):
        m_new = jnp.maximum(m, jnp.max(s, axis=0, keepdims=True))
        p_refs[slot][mp] = jnp.exp(s - m_new).astype(BF16)
        return m_new, jnp.exp(m - m_new)

    def accumulate(kb, alpha, slot, mp):
        v_aug = jnp.concatenate([vt_ref[kb], ones], axis=0)
        acc_ref[mp] = alpha * acc_ref[mp] + _dot(v_aug, p_refs[slot][mp])

    key = lax.broadcasted_iota(jnp.int32, (tq, tq), 0)
    qry = lax.broadcasted_iota(jnp.int32, (tq, tq), 1)
    causal = key <= qry
    m0 = jnp.full((1, tq), NEG_INF, F32)
    acc_ref[...] = jnp.zeros_like(acc_ref)
    stats = []
    for mp in range(2):
        s_refs[1][mp] = scores(0, mp)
        stats.extend(softmax_step(jnp.where(causal, scores(i, mp), NEG_INF), m0, 0, mp))
    stats = tuple(stats)

    def step(t, carry, cur):
        nxt = 1 - cur
        vb = jnp.where(t == 0, i, t - 1)
        out = []
        for mp in range(2):
            m, alpha = carry[2 * mp], carry[2 * mp + 1]
            s_refs[cur][mp] = scores(jnp.minimum(t + 1, i - 1), mp)
            accumulate(vb, alpha, cur, mp)
            out.extend(softmax_step(s_refs[nxt][mp], m, nxt, mp))
        return tuple(out)

    stats = lax.fori_loop(0, i // 2, lambda u, c: step(2 * u + 1, step(2 * u, c, 0), 1), stats)
    odd = i % 2 == 1
    stats = lax.cond(odd, lambda c: step(i - 1, c, 0), lambda c: c, stats)
    last = jnp.maximum(i - 1, 0)
    for slot in range(2):
        @pl.when((i % 2) == slot)
        def _():
            for mp in range(2):
                accumulate(last, stats[2 * mp + 1], slot, mp)

    lf = lam_ref[...]
    lam = (jnp.exp(jnp.sum(lf[0:1, :] * lf[1:2, :], axis=1, keepdims=True))
           - jnp.exp(jnp.sum(lf[2:3, :] * lf[3:4, :], axis=1, keepdims=True)) + lambda_init)
    o1 = acc_ref[0, :HEAD_DIM, :] / acc_ref[0, HEAD_DIM:HEAD_DIM + 1, :]
    o2 = acc_ref[1, :HEAD_DIM, :] / acc_ref[1, HEAD_DIM:HEAD_DIM + 1, :]
    d = o1 - lam * o2
    ms = jnp.mean(d * d, axis=0, keepdims=True)
    dn = (d * lax.rsqrt(ms + RMS_EPS)).T
    o_ref[...] = (dn * g_ref[...] * (1.0 - lambda_init)).astype(o_ref.dtype)


def _diff_attention(proj, vt, df_lambda, gain, tq, lambda_init):
    seq = proj.shape[0]
    qcol = 3 * SB_HEADS
    kcol = qcol + DIFF_HEADS
    kern = functools.partial(_diff_kernel, tq=tq, lambda_init=lambda_init)
    return pl.pallas_call(
        kern,
        grid=(DIFF_HEADS, seq // tq),
        in_specs=[pl.BlockSpec((tq, HEAD_DIM), lambda h, i: (i, qcol + h)),
                  pl.BlockSpec((seq, HEAD_DIM), lambda h, i: (0, kcol + h)),
                  pl.BlockSpec((seq // tq, HEAD_DIM, tq), lambda h, i: (0, h, 0)),
                  pl.BlockSpec((4, DIFF_QK_DIM), lambda h, i: (0, 0)),
                  pl.BlockSpec((1, HEAD_DIM), lambda h, i: (0, 0))],
        out_specs=pl.BlockSpec((tq, HEAD_DIM), lambda h, i: (i, h)),
        out_shape=jax.ShapeDtypeStruct((seq, DIFF_HEADS * HEAD_DIM), BF16),
        scratch_shapes=[pltpu.VMEM((2, tq, tq), F32), pltpu.VMEM((2, tq, tq), F32),
                        pltpu.VMEM((2, tq, tq), BF16), pltpu.VMEM((2, tq, tq), BF16),
                        pltpu.VMEM((2, HEAD_DIM + SUM_ROWS, tq), F32)],
        compiler_params=_params("parallel", "arbitrary"),
        name="diff_attn",
    )(proj, proj, vt, df_lambda, gain)


def _layer_norm(r, g, b):
    mu = jnp.mean(r, axis=1, keepdims=True)
    d = r - mu
    var = jnp.mean(d * d, axis=1, keepdims=True)
    return d * lax.rsqrt(var + LN_EPS) * g + b


def _proj_ln_kernel(*refs, n_parts, emit_bf16):
    a_refs = refs[:n_parts]
    w_ref, x_ref, g_ref, b_ref, wn_ref, o_ref = refs[n_parts:n_parts + 6]
    rest = refs[n_parts + 6:]
    y = None
    off = 0
    for a_ref in a_refs:
        kk = a_ref.shape[1]
        part = _dot(a_ref[...], w_ref[off:off + kk, :])
        y = part if y is None else y + part
        off += kk
    out = _layer_norm(DEEPNORM_ALPHA * x_ref[...] + y, g_ref[...], b_ref[...])
    o_ref[...] = out
    out_bf16 = out.astype(BF16)
    if emit_bf16:
        rest[0][...] = out_bf16
    rest[-1][...] = _dot(out_bf16, wn_ref[...]).astype(BF16)


def _proj_ln(parts, w, x, gain, bias, w_next, tm, emit_bf16):
    seq, d = x.shape
    dn = w_next.shape[1]
    kern = functools.partial(_proj_ln_kernel, n_parts=len(parts), emit_bf16=emit_bf16)
    row = lambda i: (i, 0)
    fixed = lambda i: (0, 0)
    out_specs = [pl.BlockSpec((tm, d), row)]
    out_shape = [jax.ShapeDtypeStruct((seq, d), F32)]
    if emit_bf16:
        out_specs.append(pl.BlockSpec((tm, d), row))
        out_shape.append(jax.ShapeDtypeStruct((seq, d), BF16))
    out_specs.append(pl.BlockSpec((tm, dn), row))
    out_shape.append(jax.ShapeDtypeStruct((seq, dn), BF16))
    return pl.pallas_call(
        kern,
        grid=(seq // tm,),
        in_specs=[pl.BlockSpec((tm, p.shape[1]), row) for p in parts] + [
            pl.BlockSpec(w.shape, fixed),
            pl.BlockSpec((tm, d), row),
            pl.BlockSpec((1, d), fixed),
            pl.BlockSpec((1, d), fixed),
            pl.BlockSpec(w_next.shape, fixed)],
        out_specs=out_specs,
        out_shape=out_shape,
        compiler_params=_params("parallel"),
        name="proj_residual_ln",
    )(*parts, w, x, gain, bias, w_next)


def _xattn_kernel(q_ref, kv_ref, o_ref, *, d_model, head_dim):
    scale = 1.0 / math.sqrt(head_dim)
    for hh in range(d_model // head_dim):
        lo = hh * head_dim
        q = q_ref[:, lo:lo + head_dim]
        k = kv_ref[:, lo:lo + head_dim]
        v = kv_ref[:, d_model + lo:d_model + lo + head_dim]
        s = _dot_nt(q, k) * scale
        m = jnp.max(s, axis=1, keepdims=True)
        p = jnp.exp(s - m)
        p = p / jnp.sum(p, axis=1, keepdims=True)
        o_ref[:, lo:lo + head_dim] = _dot(p.astype(BF16), v).astype(o_ref.dtype)


def _xattn(q, kv, tm):
    seq, d = q.shape
    kern = functools.partial(_xattn_kernel, d_model=d, head_dim=d // MEM_HEADS)
    return pl.pallas_call(
        kern,
        grid=(seq // tm,),
        in_specs=[pl.BlockSpec((tm, d), lambda i: (i, 0)),
                  pl.BlockSpec(kv.shape, lambda i: (0, 0))],
        out_specs=pl.BlockSpec((tm, d), lambda i: (i, 0)),
        out_shape=jax.ShapeDtypeStruct((seq, d), BF16),
        compiler_params=_params("parallel"),
        name="memory_xattn",
    )(q, kv)


def _top16(s, vals_ref, exact_ties):
    def step(k, carry):
        work, rank = carry
        m = jnp.max(work, axis=0, keepdims=True)
        sel = work == m
        if exact_ties:
            iota = lax.broadcasted_iota(jnp.int32, s.shape, 0).astype(F32)
            sel = iota == jnp.min(jnp.where(sel, iota, float(s.shape[0])), axis=0, keepdims=True)
        vals_ref[pl.ds(k, 1), :] = m
        return jnp.where(sel, NEG_INF, work), jnp.where(sel, lax.convert_element_type(k, F32), rank)

    _, rank = lax.fori_loop(0, PEER_TOPK, step, (s, jnp.full(s.shape, float(PEER_TOPK), F32)))
    tie_free = float(sum(range(PEER_TOPK)) + PEER_TOPK * (s.shape[0] - PEER_TOPK))
    return rank, tie_free - jnp.sum(rank, axis=0, keepdims=True)


CAND_KEEP = tuple(PEER_TOPK // (k1 + 1) for k1 in range(PEER_TOPK))
CAND_ROWS = -(-sum(CAND_KEEP) // 8) * 8


def _route_kernel(pq_ref, keys_ref, e2_ref, r2_ref, w1_ref, n1_ref, v1_ref, v2_ref, vc_ref, cand_ref):
    q = pq_ref[...]
    s1 = _dot_nt(keys_ref[0], q[:, :N_KEYS])
    s2 = _dot_nt(keys_ref[1], q[:, N_KEYS:])

    def route(exact_ties):
        rank1, deficit1 = _top16(s1, v1_ref, exact_ties)
        rank2, deficit2 = _top16(s2, v2_ref, exact_ties)
        cand_ref[...] = jnp.full(cand_ref.shape, NEG_INF, F32)
        row = lax.broadcasted_iota(jnp.int32, cand_ref.shape, 0)
        seg = jnp.zeros(cand_ref.shape, F32)
        off = 0
        for k1, keep in enumerate(CAND_KEEP):
            cand_ref[off:off + keep, :] = v1_ref[k1:k1 + 1, :] + v2_ref[0:keep, :]
            off += keep
            seg = seg + jnp.where(row >= off, 1.0, 0.0)
        crank, deficit3 = _top16(cand_ref[...], vc_ref, exact_ties)
        top = vc_ref[...]
        z = jnp.sum(jnp.exp(top - top[0:1, :]), axis=0, keepdims=True)
        chosen = jnp.where(crank < float(PEER_TOPK), 1.0, 0.0)
        rank1_packed = rank1.astype(BF16)
        n1 = jnp.zeros(s1.shape, BF16)
        for k1 in range(PEER_TOPK):
            cnt = jnp.sum(jnp.where(seg == float(k1), chosen, 0.0), axis=0, keepdims=True)
            cnt = jnp.tile(jnp.broadcast_to(cnt, (BF16_SUBLANES, cnt.shape[1])).astype(BF16),
                           (N_KEYS // BF16_SUBLANES, 1))
            n1 = jnp.where(rank1_packed == jnp.asarray(k1, BF16), cnt, n1)
        e2 = jnp.exp(s2 - v2_ref[0:1, :]).astype(e2_ref.dtype)
        w1 = jnp.exp(s1 - v1_ref[0:1, :]) / z
        tied = jnp.max(jnp.maximum(jnp.maximum(deficit1, deficit2), deficit3)) > 0.0
        return (e2, rank2.astype(r2_ref.dtype), w1, n1.astype(F32)), tied

    tables, tied = route(False)
    tables = lax.cond(tied, lambda _: route(True)[0], lambda t: t, tables)
    e2_ref[0], r2_ref[0], w1_ref[0], n1_ref[0] = tables


def _peer_route(pq, keys, tt):
    seq = pq.shape[0]
    qd = 2 * N_KEYS
    tab = pl.BlockSpec((1, N_KEYS, tt), lambda i, h: (h, 0, i))
    shape = jax.ShapeDtypeStruct((PEER_HEADS, N_KEYS, seq), F32)
    packed = jax.ShapeDtypeStruct((PEER_HEADS, N_KEYS, seq), BF16)
    return pl.pallas_call(
        _route_kernel,
        grid=(seq // tt, PEER_HEADS),
        in_specs=[pl.BlockSpec((tt, qd), lambda i, h: (i, h)),
                  pl.BlockSpec(keys.shape, lambda i, h: (0, 0, 0))],
        out_specs=[tab, tab, tab, tab],
        out_shape=[packed, packed, shape, shape],
        scratch_shapes=[pltpu.VMEM((PEER_TOPK, tt), F32),
                        pltpu.VMEM((PEER_TOPK, tt), F32),
                        pltpu.VMEM((PEER_TOPK, tt), F32),
                        pltpu.VMEM((CAND_ROWS, tt), F32)],
        compiler_params=_params("parallel", "parallel"),
        name="peer_route",
    )(pq, keys)


def _peer_kernel(xb_ref, u_ref, v_ref, e2_ref, r2_ref, w1_ref, n1_ref, x_ref, g_ref, b_ref,
                 o_ref, a_ref, gate_ref, *, keys_per_step, n_chunks):
    step = pl.program_id(0)
    cur, prev = step % 2, (step + 1) % 2
    chunk = step % n_chunks
    chunk_prev = (step + n_chunks - 1) % n_chunks

    @pl.when(step == 0)
    def _():
        a_ref[...] = jnp.zeros_like(a_ref)
        gate_ref[...] = jnp.zeros_like(gate_ref)

    @pl.when(jnp.logical_or(chunk_prev == 0, step == 0))
    def _():
        o_ref[...] = jnp.zeros_like(o_ref)

    gated = (a_ref[prev].astype(F32) * gate_ref[prev]).astype(BF16)
    o_ref[...] += _dot(gated, v_ref[...])

    hidden = _dot(xb_ref[...], u_ref[...])
    a_ref[cur] = (0.5 * hidden * (1.0 + lax.erf(hidden * math.sqrt(0.5)))).astype(BF16)

    zero = jnp.zeros(r2_ref.shape[1:], BF16)

    def rows(row):
        tile = jnp.broadcast_to(row, (BF16_SUBLANES, row.shape[1])).astype(BF16)
        return jnp.tile(tile, (N_KEYS // BF16_SUBLANES, 1))

    for part in range(keys_per_step):
        c = chunk * keys_per_step + part
        gt = None
        for h in range(PEER_HEADS):
            n1 = rows(n1_ref[h, pl.ds(c, 1), :])
            w1 = rows(w1_ref[h, pl.ds(c, 1), :])
            term = jnp.where(r2_ref[h] < n1, e2_ref[h] * w1, zero)
            gt = term if gt is None else gt + term
        gate_ref[cur, :, part * N_KEYS:(part + 1) * N_KEYS] = gt.astype(F32).T

    @pl.when(jnp.logical_and(chunk_prev == n_chunks - 1, step > 0))
    def _():
        o_ref[...] = _layer_norm(DEEPNORM_ALPHA * x_ref[...] + o_ref[...], g_ref[...], b_ref[...])


def _peer_dense(xb, x, ut, v, tables, gain, bias, tt, keys_per_step):
    seq, d = x.shape
    te = keys_per_step * N_KEYS
    n_chunks = v.shape[0] // te
    n_tiles = seq // tt
    tile = lambda s: jnp.minimum(s // n_chunks, n_tiles - 1)
    tile_prev = lambda s: jnp.maximum(s - 1, 0) // n_chunks
    tab = pl.BlockSpec((PEER_HEADS, N_KEYS, tt), lambda s: (0, 0, tile(s)))
    fixed = lambda s: (0, 0)
    kern = functools.partial(_peer_kernel, keys_per_step=keys_per_step, n_chunks=n_chunks)
    return pl.pallas_call(
        kern,
        grid=(n_tiles * n_chunks + 1,),
        in_specs=[pl.BlockSpec((tt, d), lambda s: (tile(s), 0)),
                  pl.BlockSpec((d, te), lambda s: (0, s % n_chunks)),
                  pl.BlockSpec((te, d), lambda s: ((s + n_chunks - 1) % n_chunks, 0)),
                  tab, tab, tab, tab,
                  pl.BlockSpec((tt, d), lambda s: (tile_prev(s), 0)),
                  pl.BlockSpec((1, d), fixed),
                  pl.BlockSpec((1, d), fixed)],
        out_specs=pl.BlockSpec((tt, d), lambda s: (tile_prev(s), 0)),
        out_shape=jax.ShapeDtypeStruct((seq, d), F32),
        scratch_shapes=[pltpu.VMEM((2, tt, te), BF16), pltpu.VMEM((2, tt, te), F32)],
        compiler_params=_params("arbitrary"),
        name="peer_dense",
    )(xb, ut, v, *tables, x, gain, bias)


def _tile(n, want):
    return min(n, want)


def kernel(x, mem, w_in, sb_norm_gain, df_lambda, df_subln_gain, w_o, ln1_gain, ln1_bias, w_mq, w_mkv, w_mo, ln2_gain, ln2_bias, w_pq, peer_sub_keys, peer_u, peer_v, ln3_gain, ln3_bias):
    b, seq, d = x.shape
    assert b == 1 and w_in.shape[0] == DEPTH
    x2d = x.reshape(seq, d)
    mem2d = mem.reshape(mem.shape[1], d).astype(BF16)
    for l in range(DEPTH):
        lambda_init = 0.8 - 0.6 * math.exp(-0.3 * l)
        row1 = lambda a: a.reshape(1, -1)
        tq = _tile(seq, ATTN_BLOCK)
        width = SB_HEADS * HEAD_DIM
        proj, vt = _inproj(x2d, w_in[l].astype(BF16), _tile(seq, PROJ_ROWS), width, tq,
                           rope_cols=(3 * width, 5 * width), vt_cols=(5 * width, 6 * width))
        sb_lo, sb_hi = _sb_attention(proj, sb_norm_gain[l], tq)
        df_o = _diff_attention(proj, vt, df_lambda[l], row1(df_subln_gain[l]), tq, lambda_init)
        x1, q = _proj_ln([sb_lo, sb_hi, df_o], w_o[l].astype(BF16), x2d, row1(ln1_gain[l]), row1(ln1_bias[l]),
                         w_mq[l].astype(BF16), _tile(seq, LN_ROWS), emit_bf16=False)
        kv = _matmul(mem2d, w_mkv[l].astype(BF16), mem2d.shape[0], PROJ_COLS)
        xa = _xattn(q, kv, _tile(seq, LN_ROWS))
        x2, x2b, pq = _proj_ln([xa], w_mo[l].astype(BF16), x1, row1(ln2_gain[l]), row1(ln2_bias[l]),
                               w_pq[l].astype(BF16), _tile(seq, LN_ROWS), emit_bf16=True)
        tables = _peer_route(pq, peer_sub_keys[l].astype(BF16), _tile(seq, ROUTE_TOKENS))
        x2d = _peer_dense(x2b, x2, peer_u[l].T.astype(BF16), peer_v[l].astype(BF16), tables,
                          row1(ln3_gain[l]), row1(ln3_bias[l]), _tile(seq, PEER_TOKENS), PEER_KEYS_PER_STEP)
    return x2d.reshape(b, seq, d)
```

```python
import functools
import math

import jax
import jax.numpy as jnp
from jax import lax
from jax.experimental import pallas as pl
from jax.experimental.pallas import tpu as pltpu

F32 = jnp.float32
BF16 = jnp.bfloat16

LANES = 128
BF16_SUBLANES = 16
MXU_WIDTH = 256
VMEM_LIMIT_BYTES = 56 * 1024 * 1024

ATTN_BLOCK = 256
PROJ_ROWS = 1024
PROJ_COLS = 1024
LN_ROWS = 256
ROUTE_TOKENS = 256
PEER_TOKENS = 512
PEER_KEYS_PER_STEP = 4

DEPTH = 1
SB_HEADS = 8
DIFF_HEADS = 8
HEAD_DIM = 128
DIFF_QK_DIM = 64
ROPE_DIM = 16
ROPE_THETA = 500000.0
MEM_HEADS = 4
PEER_HEADS = 8
N_KEYS = 128
PEER_TOPK = 16
LN_EPS = 1e-5
RMS_EPS = 1e-6
DEEPNORM_ALPHA = (2 * DEPTH) ** 0.25
NEG_INF = float("-inf")
SB_DEAD_LOG = -105.0
SUM_ROWS = 16
DIFF_UNROLL = 4


def _dot(a, b):
    return jnp.dot(a, b, preferred_element_type=F32)


def _dot_nt(a, b):
    return lax.dot_general(a, b, (((1,), (1,)), ((), ())), preferred_element_type=F32)


def _params(*semantics):
    return pltpu.CompilerParams(dimension_semantics=semantics, vmem_limit_bytes=VMEM_LIMIT_BYTES)


def _matmul_kernel(a_ref, b_ref, o_ref):
    o_ref[...] = _dot(a_ref[...], b_ref[...]).astype(o_ref.dtype)


def _matmul(a, b, tm, tn, out_dtype=BF16):
    m, k = a.shape
    n = b.shape[1]
    return pl.pallas_call(
        _matmul_kernel,
        grid=(m // tm, n // tn),
        in_specs=[pl.BlockSpec((tm, k), lambda i, j: (i, 0)),
                  pl.BlockSpec((k, tn), lambda i, j: (0, j))],
        out_specs=pl.BlockSpec((tm, tn), lambda i, j: (i, j)),
        out_shape=jax.ShapeDtypeStruct((m, n), out_dtype),
        compiler_params=_params("parallel", "parallel"),
        name="matmul",
    )(a, b)


def _inproj_kernel(a_ref, b_ref, c_ref, s1_ref, s2_ref, o_ref, vt_ref, *, rope_lo, rope_hi, vt_col):
    j = pl.program_id(1)
    acc = _dot(a_ref[...].astype(BF16), b_ref[...])
    o_ref[...] = acc.astype(o_ref.dtype)

    @pl.when(j == vt_col)
    def _():
        tk = vt_ref.shape[2]
        for kb in range(vt_ref.shape[0]):
            vt_ref[kb] = acc[kb * tk:(kb + 1) * tk, :].T.astype(vt_ref.dtype)

    @pl.when(jnp.logical_and(j >= rope_lo, j < rope_hi))
    def _():
        cos, sin_up, sin_dn = c_ref[...], s1_ref[...], s2_ref[...]
        for cc in range(acc.shape[1] // LANES):
            t = acc[:, cc * LANES:(cc + 1) * LANES]
            up = pltpu.roll(t, LANES - ROPE_DIM // 2, axis=1)
            dn = pltpu.roll(t, ROPE_DIM // 2, axis=1)
            o_ref[:, cc * LANES:(cc + 1) * LANES] = (t * cos + up * sin_up + dn * sin_dn).astype(o_ref.dtype)


def _rope_tables(seq):
    half = ROPE_DIM // 2
    inv_freq = jnp.power(ROPE_THETA, -jnp.arange(half, dtype=F32) * 2.0 / ROPE_DIM)
    ang = jnp.arange(seq).astype(F32)[:, None] * inv_freq[None, :]
    cos, sin = jnp.cos(ang), jnp.sin(ang)
    ones = jnp.ones((seq, DIFF_QK_DIM - ROPE_DIM), F32)
    zeros = jnp.zeros((seq, DIFF_QK_DIM - ROPE_DIM), F32)
    zh = jnp.zeros((seq, half), F32)
    c64 = jnp.concatenate([cos, cos, ones], axis=1)
    up64 = jnp.concatenate([-sin, zh, zeros], axis=1)
    dn64 = jnp.concatenate([zh, sin, zeros], axis=1)
    rep = LANES // DIFF_QK_DIM
    return jnp.tile(c64, (1, rep)), jnp.tile(up64, (1, rep)), jnp.tile(dn64, (1, rep))


def _inproj(x, w_in, tm, tn, tk, rope_cols, vt_cols):
    seq, k = x.shape
    n = w_in.shape[1]
    assert vt_cols == (n - tn, n)
    cos, sin_up, sin_dn = _rope_tables(seq)
    tab = pl.BlockSpec((tm, LANES), lambda i, j: (i, 0))
    kern = functools.partial(_inproj_kernel, rope_lo=rope_cols[0] // tn, rope_hi=rope_cols[1] // tn,
                             vt_col=vt_cols[0] // tn)
    return pl.pallas_call(
        kern,
        grid=(seq // tm, n // tn),
        in_specs=[pl.BlockSpec((tm, k), lambda i, j: (i, 0)),
                  pl.BlockSpec((k, tn), lambda i, j: (0, j)),
                  tab, tab, tab],
        out_specs=[pl.BlockSpec((tm, tn), lambda i, j: (i, j)),
                   pl.BlockSpec((tm // tk, tn, tk), lambda i, j: (i, 0, 0))],
        out_shape=[jax.ShapeDtypeStruct((seq, n), BF16),
                   jax.ShapeDtypeStruct((seq // tk, tn, tk), BF16)],
        compiler_params=_params("parallel", "arbitrary"),
        name="inproj_rope",
    )(x, w_in, cos, sin_up, sin_dn)


def _sb_kernel(qa_ref, ka_ref, va_ref, qb_ref, kb_ref, vb_ref, g_ref, oa_ref, ob_ref, *, tq, scale, head_stride):
    h = pl.program_id(0)
    i = pl.program_id(1)
    row = lax.broadcasted_iota(jnp.int32, (tq, tq), 0)
    col = lax.broadcasted_iota(jnp.int32, (tq, tq), 1)
    later = (row > col).astype(BF16)
    before = col < row

    def block_terms(q, k_ref, v_ref, kb, masked):
        start = pl.multiple_of(kb * tq, tq)
        k = k_ref[pl.ds(start, tq), :]
        z = _dot_nt(q, k) * scale
        sp = jnp.log(1.0 + jnp.exp(-jnp.abs(z)))
        log_beta = jnp.minimum(z, 0.0) - sp
        log_keep = -jnp.maximum(z, 0.0) - sp
        if masked:
            log_keep = jnp.where(before, log_keep, 0.0)
        hi = log_keep.astype(BF16)
        lo = (log_keep - hi.astype(F32)).astype(BF16)
        stick = _dot(hi, later) + _dot(lo, later)
        return log_beta + stick, jnp.sum(log_keep, axis=1, keepdims=True), v_ref[pl.ds(start, tq), :]

    def block_apply(terms, acc, c, masked, valid=None):
        log_w, keep_sum, v = terms
        w = jnp.exp(log_w + c)
        if masked:
            w = jnp.where(before, w, 0.0)
        if valid is not None:
            w = jnp.where(valid, w, 0.0)
            keep_sum = jnp.where(valid, keep_sum, 0.0)
        return acc + _dot(w.astype(BF16), v), c + keep_sum

    acc0 = jnp.zeros((tq, HEAD_DIM), F32)
    c0 = jnp.zeros((tq, 1), F32)
    heads = ((qa_ref[...], ka_ref, va_ref), (qb_ref[...], kb_ref, vb_ref))
    first = [(block_terms(q, k_ref, v_ref, i, True), block_terms(q, k_ref, v_ref, jnp.maximum(i - 1, 0), False))
             for q, k_ref, v_ref in heads]
    state = []
    for diag, prev in first:
        acc, c = block_apply(diag, acc0, c0, True)
        state.append(block_apply(prev, acc, c, False, valid=i >= 1))

    def live(carry):
        jj, _, _, cmax = carry
        return jnp.logical_and(jj < i, cmax > SB_DEAD_LOG)

    for (q, k_ref, v_ref), (acc, c), o_ref, head in zip(heads, state, (oa_ref, ob_ref), (h, h + head_stride)):
        def older(carry, q=q, k_ref=k_ref, v_ref=v_ref):
            jj, acc, c, _ = carry
            acc, c = block_apply(block_terms(q, k_ref, v_ref, i - 1 - jj, False), acc, c, False)
            return jj + 1, acc, c, jnp.max(c)

        _, acc, c, _ = lax.while_loop(live, older, (jnp.int32(1), acc, c, jnp.max(c)))
        g = g_ref[pl.ds(head, 1), :]
        ms = jnp.mean(acc * acc, axis=1, keepdims=True)
        o_ref[...] = (acc * lax.rsqrt(ms + RMS_EPS) * g).astype(o_ref.dtype)


def _sb_attention(proj, gain, tq):
    seq = proj.shape[0]
    half = SB_HEADS // 2
    kern = functools.partial(_sb_kernel, tq=tq, scale=1.0 / math.sqrt(HEAD_DIM), head_stride=half)
    specs = []
    for off in (0, half):
        specs += [pl.BlockSpec((tq, HEAD_DIM), lambda h, i, off=off: (i, off + h)),
                  pl.BlockSpec((seq, HEAD_DIM), lambda h, i, off=off: (0, SB_HEADS + off + h)),
                  pl.BlockSpec((seq, HEAD_DIM), lambda h, i, off=off: (0, 2 * SB_HEADS + off + h))]
    out = jax.ShapeDtypeStruct((seq, half * HEAD_DIM), BF16)
    return pl.pallas_call(
        kern,
        grid=(half, seq // tq),
        in_specs=specs + [pl.BlockSpec((SB_HEADS, HEAD_DIM), lambda h, i: (0, 0))],
        out_specs=[pl.BlockSpec((tq, HEAD_DIM), lambda h, i: (i, h))] * 2,
        out_shape=[out, out],
        compiler_params=_params("parallel", "arbitrary"),
        name="stickbreak_attn",
    )(*([proj] * 6), gain)


def _diff_kernel(q_ref, k_ref, vt_ref, lam_ref, g_ref, o_ref, s0_ref, s1_ref, p0_ref, p1_ref, acc_ref,
                 *, tq, lambda_init):
    i = pl.program_id(1)
    s_refs, p_refs = (s0_ref, s1_ref), (p0_ref, p1_ref)
    q = q_ref[...] * jnp.asarray(1.0 / math.sqrt(DIFF_QK_DIM), BF16)
    lane = lax.broadcasted_iota(jnp.int32, (tq, HEAD_DIM), 1)
    zero = jnp.zeros_like(q)
    qmaps = (jnp.where(lane < DIFF_QK_DIM, q, zero),
             jnp.where(lane >= DIFF_QK_DIM, q, zero))
    ones = jnp.ones((SUM_ROWS, tq), BF16)

    def scores(kb, mp):
        k = k_ref[pl.ds(pl.multiple_of(kb * tq, tq), tq), :]
        return _dot_nt(k, qmaps[mp])

    def softmax_step(s, m, slot, mp):
        m_new = jnp.maximum(m, jnp.max(s, axis=0, keepdims=True))
        p_refs[slot][mp] = jnp.exp(s - m_new).astype(BF16)
        return m_new, jnp.exp(m - m_new)

    def accumulate(kb, alpha, slot, mp):
        v_aug = jnp.concatenate([vt_ref[kb], ones], axis=0)
        acc_ref[mp] = alpha * acc_ref[mp] + _dot(v_aug, p_refs[slot][mp])

    key = lax.broadcasted_iota(jnp.int32, (tq, tq), 0)
    qry = lax.broadcasted_iota(jnp.int32, (tq, tq), 1)
    causal = key <= qry
    m0 = jnp.full((1, tq), NEG_INF, F32)
    acc_ref[...] = jnp.zeros_like(acc_ref)
    stats = []
    for mp in range(2):
        s_refs[1][mp] = scores(0, mp)
        stats.extend(softmax_step(jnp.where(causal, scores(i, mp), NEG_INF), m0, 0, mp))
    stats = tuple(stats)

    def step(t, carry, cur):
        nxt = 1 - cur
        vb = jnp.where(t == 0, i, t - 1)
        out = []
        for mp in range(2):
            m, alpha = carry[2 * mp], carry[2 * mp + 1]
            s_refs[cur][mp] = scores(jnp.minimum(t + 1, i - 1), mp)
            accumulate(vb, alpha, cur, mp)
            out.extend(softmax_step(s_refs[nxt][mp], m, nxt, mp))
        return tuple(out)

    def unrolled(u, carry):
        for j in range(DIFF_UNROLL):
            carry = step(DIFF_UNROLL * u + j, carry, j % 2)
        return carry

    stats = lax.fori_loop(0, i // DIFF_UNROLL, unrolled, stats)
    done = (i // DIFF_UNROLL) * DIFF_UNROLL
    for j in range(DIFF_UNROLL - 1):
        stats = lax.cond(i - done > j, lambda c, j=j: step(done + j, c, j % 2), lambda c: c, stats)
    last = jnp.maximum(i - 1, 0)
    for slot in range(2):
        @pl.when((i % 2) == slot)
        def _():
            for mp in range(2):
                accumulate(last, stats[2 * mp + 1], slot, mp)

    lf = lam_ref[...]
    lam = (jnp.exp(jnp.sum(lf[0:1, :] * lf[1:2, :], axis=1, keepdims=True))
           - jnp.exp(jnp.sum(lf[2:3, :] * lf[3:4, :], axis=1, keepdims=True)) + lambda_init)
    o1 = acc_ref[0, :HEAD_DIM, :] / acc_ref[0, HEAD_DIM:HEAD_DIM + 1, :]
    o2 = acc_ref[1, :HEAD_DIM, :] / acc_ref[1, HEAD_DIM:HEAD_DIM + 1, :]
    d = o1 - lam * o2
    ms = jnp.mean(d * d, axis=0, keepdims=True)
    dn = (d * lax.rsqrt(ms + RMS_EPS)).T
    o_ref[...] = (dn * g_ref[...] * (1.0 - lambda_init)).astype(o_ref.dtype)


def _diff_attention(proj, vt, df_lambda, gain, tq, lambda_init):
    seq = proj.shape[0]
    qcol = 3 * SB_HEADS
    kcol = qcol + DIFF_HEADS
    kern = functools.partial(_diff_kernel, tq=tq, lambda_init=lambda_init)
    return pl.pallas_call(
        kern,
        grid=(DIFF_HEADS, seq // tq),
        in_specs=[pl.BlockSpec((tq, HEAD_DIM), lambda h, i: (i, qcol + h)),
                  pl.BlockSpec((seq, HEAD_DIM), lambda h, i: (0, kcol + h)),
                  pl.BlockSpec((seq // tq, HEAD_DIM, tq), lambda h, i: (0, h, 0)),
                  pl.BlockSpec((4, DIFF_QK_DIM), lambda h, i: (0, 0)),
                  pl.BlockSpec((1, HEAD_DIM), lambda h, i: (0, 0))],
        out_specs=pl.BlockSpec((tq, HEAD_DIM), lambda h, i: (i, h)),
        out_shape=jax.ShapeDtypeStruct((seq, DIFF_HEADS * HEAD_DIM), BF16),
        scratch_shapes=[pltpu.VMEM((2, tq, tq), F32), pltpu.VMEM((2, tq, tq), F32),
                        pltpu.VMEM((2, tq, tq), BF16), pltpu.VMEM((2, tq, tq), BF16),
                        pltpu.VMEM((2, HEAD_DIM + SUM_ROWS, tq), F32)],
        compiler_params=_params("parallel", "arbitrary"),
        name="diff_attn",
    )(proj, proj, vt, df_lambda, gain)


def _layer_norm(r, g, b):
    mu = jnp.mean(r, axis=1, keepdims=True)
    d = r - mu
    var = jnp.mean(d * d, axis=1, keepdims=True)
    return d * lax.rsqrt(var + LN_EPS) * g + b


def _proj_ln_kernel(*refs, n_parts, emit_bf16):
    a_refs = refs[:n_parts]
    w_ref, x_ref, g_ref, b_ref, wn_ref, o_ref = refs[n_parts:n_parts + 6]
    rest = refs[n_parts + 6:]
    y = None
    off = 0
    for a_ref in a_refs:
        kk = a_ref.shape[1]
        part = _dot(a_ref[...], w_ref[off:off + kk, :])
        y = part if y is None else y + part
        off += kk
    out = _layer_norm(DEEPNORM_ALPHA * x_ref[...] + y, g_ref[...], b_ref[...])
    o_ref[...] = out
    out_bf16 = out.astype(BF16)
    if emit_bf16:
        rest[0][...] = out_bf16
    rest[-1][...] = _dot(out_bf16, wn_ref[...]).astype(BF16)


def _proj_ln(parts, w, x, gain, bias, w_next, tm, emit_bf16):
    seq, d = x.shape
    dn = w_next.shape[1]
    kern = functools.partial(_proj_ln_kernel, n_parts=len(parts), emit_bf16=emit_bf16)
    row = lambda i: (i, 0)
    fixed = lambda i: (0, 0)
    out_specs = [pl.BlockSpec((tm, d), row)]
    out_shape = [jax.ShapeDtypeStruct((seq, d), F32)]
    if emit_bf16:
        out_specs.append(pl.BlockSpec((tm, d), row))
        out_shape.append(jax.ShapeDtypeStruct((seq, d), BF16))
    out_specs.append(pl.BlockSpec((tm, dn), row))
    out_shape.append(jax.ShapeDtypeStruct((seq, dn), BF16))
    return pl.pallas_call(
        kern,
        grid=(seq // tm,),
        in_specs=[pl.BlockSpec((tm, p.shape[1]), row) for p in parts] + [
            pl.BlockSpec(w.shape, fixed),
            pl.BlockSpec((tm, d), row),
            pl.BlockSpec((1, d), fixed),
            pl.BlockSpec((1, d), fixed),
            pl.BlockSpec(w_next.shape, fixed)],
        out_specs=out_specs,
        out_shape=out_shape,
        compiler_params=_params("parallel"),
        name="proj_residual_ln",
    )(*parts, w, x, gain, bias, w_next)


def _xattn_kernel(q_ref, kv_ref, o_ref, *, d_model, head_dim):
    scale = 1.0 / math.sqrt(head_dim)
    for hh in range(d_model // head_dim):
        lo = hh * head_dim
        q = q_ref[:, lo:lo + head_dim]
        k = kv_ref[:, lo:lo + head_dim]
        v = kv_ref[:, d_model + lo:d_model + lo + head_dim]
        s = _dot_nt(q, k) * scale
        m = jnp.max(s, axis=1, keepdims=True)
        p = jnp.exp(s - m)
        p = p / jnp.sum(p, axis=1, keepdims=True)
        o_ref[:, lo:lo + head_dim] = _dot(p.astype(BF16), v).astype(o_ref.dtype)


def _xattn(q, kv, tm):
    seq, d = q.shape
    kern = functools.partial(_xattn_kernel, d_model=d, head_dim=d // MEM_HEADS)
    return pl.pallas_call(
        kern,
        grid=(seq // tm,),
        in_specs=[pl.BlockSpec((tm, d), lambda i: (i, 0)),
                  pl.BlockSpec(kv.shape, lambda i: (0, 0))],
        out_specs=pl.BlockSpec((tm, d), lambda i: (i, 0)),
        out_shape=jax.ShapeDtypeStruct((seq, d), BF16),
        compiler_params=_params("parallel"),
        name="memory_xattn",
    )(q, kv)


def _top16(s, vals_ref, exact_ties):
    def step(k, carry):
        work, rank = carry
        m = jnp.max(work, axis=0, keepdims=True)
        sel = work == m
        if exact_ties:
            iota = lax.broadcasted_iota(jnp.int32, s.shape, 0).astype(F32)
            sel = iota == jnp.min(jnp.where(sel, iota, float(s.shape[0])), axis=0, keepdims=True)
        vals_ref[pl.ds(k, 1), :] = m
        return jnp.where(sel, NEG_INF, work), jnp.where(sel, lax.convert_element_type(k, F32), rank)

    _, rank = lax.fori_loop(0, PEER_TOPK, step, (s, jnp.full(s.shape, float(PEER_TOPK), F32)), unroll=8)
    tie_free = float(sum(range(PEER_TOPK)) + PEER_TOPK * (s.shape[0] - PEER_TOPK))
    return rank, tie_free - jnp.sum(rank, axis=0, keepdims=True)


CAND_KEEP = tuple(PEER_TOPK // (k1 + 1) for k1 in range(PEER_TOPK))
CAND_ROWS = -(-sum(CAND_KEEP) // 8) * 8


def _route_kernel(pq_ref, keys_ref, e2_ref, r2_ref, w1_ref, n1_ref, v1_ref, v2_ref, vc_ref, cand_ref):
    q = pq_ref[...]
    s1 = _dot_nt(keys_ref[0], q[:, :N_KEYS])
    s2 = _dot_nt(keys_ref[1], q[:, N_KEYS:])

    def route(exact_ties):
        rank1, deficit1 = _top16(s1, v1_ref, exact_ties)
        rank2, deficit2 = _top16(s2, v2_ref, exact_ties)
        cand_ref[...] = jnp.full(cand_ref.shape, NEG_INF, F32)
        row = lax.broadcasted_iota(jnp.int32, cand_ref.shape, 0)
        seg = jnp.zeros(cand_ref.shape, F32)
        off = 0
        for k1, keep in enumerate(CAND_KEEP):
            cand_ref[off:off + keep, :] = v1_ref[k1:k1 + 1, :] + v2_ref[0:keep, :]
            off += keep
            seg = seg + jnp.where(row >= off, 1.0, 0.0)
        crank, deficit3 = _top16(cand_ref[...], vc_ref, exact_ties)
        top = vc_ref[...]
        z = jnp.sum(jnp.exp(top - top[0:1, :]), axis=0, keepdims=True)
        chosen = jnp.where(crank < float(PEER_TOPK), 1.0, 0.0)
        rank1_packed = rank1.astype(BF16)
        n1 = jnp.zeros(s1.shape, BF16)
        for k1 in range(PEER_TOPK):
            cnt = jnp.sum(jnp.where(seg == float(k1), chosen, 0.0), axis=0, keepdims=True)
            cnt = jnp.tile(jnp.broadcast_to(cnt, (BF16_SUBLANES, cnt.shape[1])).astype(BF16),
                           (N_KEYS // BF16_SUBLANES, 1))
            n1 = jnp.where(rank1_packed == jnp.asarray(k1, BF16), cnt, n1)
        e2 = jnp.exp(s2 - v2_ref[0:1, :]).astype(e2_ref.dtype)
        w1 = jnp.exp(s1 - v1_ref[0:1, :]) / z
        tied = jnp.max(jnp.maximum(jnp.maximum(deficit1, deficit2), deficit3)) > 0.0
        return (e2, rank2.astype(r2_ref.dtype), w1, n1.astype(F32)), tied

    tables, tied = route(False)
    tables = lax.cond(tied, lambda _: route(True)[0], lambda t: t, tables)
    e2_ref[0], r2_ref[0], w1_ref[0], n1_ref[0] = tables


def _peer_route(pq, keys, tt):
    seq = pq.shape[0]
    qd = 2 * N_KEYS
    tab = pl.BlockSpec((1, N_KEYS, tt), lambda i, h: (h, 0, i))
    shape = jax.ShapeDtypeStruct((PEER_HEADS, N_KEYS, seq), F32)
    packed = jax.ShapeDtypeStruct((PEER_HEADS, N_KEYS, seq), BF16)
    return pl.pallas_call(
        _route_kernel,
        grid=(seq // tt, PEER_HEADS),
        in_specs=[pl.BlockSpec((tt, qd), lambda i, h: (i, h)),
                  pl.BlockSpec(keys.shape, lambda i, h: (0, 0, 0))],
        out_specs=[tab, tab, tab, tab],
        out_shape=[packed, packed, shape, shape],
        scratch_shapes=[pltpu.VMEM((PEER_TOPK, tt), F32),
                        pltpu.VMEM((PEER_TOPK, tt), F32),
                        pltpu.VMEM((PEER_TOPK, tt), F32),
                        pltpu.VMEM((CAND_ROWS, tt), F32)],
        compiler_params=_params("parallel", "parallel"),
        name="peer_route",
    )(pq, keys)


def _peer_kernel(xb_ref, u_ref, v_ref, e2_ref, r2_ref, w1_ref, n1_ref, x_ref, g_ref, b_ref,
                 o_ref, a_ref, gate_ref, *, keys_per_step, n_chunks):
    step = pl.program_id(0)
    cur, prev = step % 2, (step + 1) % 2
    chunk = step % n_chunks
    chunk_prev = (step + n_chunks - 1) % n_chunks

    @pl.when(step == 0)
    def _():
        a_ref[...] = jnp.zeros_like(a_ref)
        gate_ref[...] = jnp.zeros_like(gate_ref)

    @pl.when(jnp.logical_or(chunk_prev == 0, step == 0))
    def _():
        o_ref[...] = jnp.zeros_like(o_ref)

    gated = (a_ref[prev].astype(F32) * gate_ref[prev]).astype(BF16)
    o_ref[...] += _dot(gated, v_ref[...])

    hidden = _dot(xb_ref[...], u_ref[...])
    a_ref[cur] = (0.5 * hidden * (1.0 + lax.erf(hidden * math.sqrt(0.5)))).astype(BF16)

    zero = jnp.zeros(r2_ref.shape[1:], BF16)

    def rows(row):
        tile = jnp.broadcast_to(row, (BF16_SUBLANES, row.shape[1])).astype(BF16)
        return jnp.tile(tile, (N_KEYS // BF16_SUBLANES, 1))

    for part in range(keys_per_step):
        c = chunk * keys_per_step + part
        gt = None
        for h in range(PEER_HEADS):
            n1 = rows(n1_ref[h, pl.ds(c, 1), :])
            w1 = rows(w1_ref[h, pl.ds(c, 1), :])
            term = jnp.where(r2_ref[h] < n1, e2_ref[h] * w1, zero)
            gt = term if gt is None else gt + term
        gate_ref[cur, :, part * N_KEYS:(part + 1) * N_KEYS] = gt.astype(F32).T

    @pl.when(jnp.logical_and(chunk_prev == n_chunks - 1, step > 0))
    def _():
        o_ref[...] = _layer_norm(DEEPNORM_ALPHA * x_ref[...] + o_ref[...], g_ref[...], b_ref[...])


def _peer_dense(xb, x, ut, v, tables, gain, bias, tt, keys_per_step):
    seq, d = x.shape
    te = keys_per_step * N_KEYS
    n_chunks = v.shape[0] // te
    n_tiles = seq // tt
    tile = lambda s: jnp.minimum(s // n_chunks, n_tiles - 1)
    tile_prev = lambda s: jnp.maximum(s - 1, 0) // n_chunks
    tab = pl.BlockSpec((PEER_HEADS, N_KEYS, tt), lambda s: (0, 0, tile(s)))
    fixed = lambda s: (0, 0)
    kern = functools.partial(_peer_kernel, keys_per_step=keys_per_step, n_chunks=n_chunks)
    return pl.pallas_call(
        kern,
        grid=(n_tiles * n_chunks + 1,),
        in_specs=[pl.BlockSpec((tt, d), lambda s: (tile(s), 0)),
                  pl.BlockSpec((d, te), lambda s: (0, s % n_chunks)),
                  pl.BlockSpec((te, d), lambda s: ((s + n_chunks - 1) % n_chunks, 0)),
                  tab, tab, tab, tab,
                  pl.BlockSpec((tt, d), lambda s: (tile_prev(s), 0)),
                  pl.BlockSpec((1, d), fixed),
                  pl.BlockSpec((1, d), fixed)],
        out_specs=pl.BlockSpec((tt, d), lambda s: (tile_prev(s), 0)),
        out_shape=jax.ShapeDtypeStruct((seq, d), F32),
        scratch_shapes=[pltpu.VMEM((2, tt, te), BF16), pltpu.VMEM((2, tt, te), F32)],
        compiler_params=_params("arbitrary"),
        name="peer_dense",
    )(xb, ut, v, *tables, x, gain, bias)


def _tile(n, want):
    return min(n, want)


def kernel(x, mem, w_in, sb_norm_gain, df_lambda, df_subln_gain, w_o, ln1_gain, ln1_bias, w_mq, w_mkv, w_mo, ln2_gain, ln2_bias, w_pq, peer_sub_keys, peer_u, peer_v, ln3_gain, ln3_bias):
    b, seq, d = x.shape
    assert b == 1 and w_in.shape[0] == DEPTH
    x2d = x.reshape(seq, d)
    mem2d = mem.reshape(mem.shape[1], d).astype(BF16)
    for l in range(DEPTH):
        lambda_init = 0.8 - 0.6 * math.exp(-0.3 * l)
        row1 = lambda a: a.reshape(1, -1)
        tq = _tile(seq, ATTN_BLOCK)
        width = SB_HEADS * HEAD_DIM
        proj, vt = _inproj(x2d, w_in[l].astype(BF16), _tile(seq, PROJ_ROWS), width, tq,
                           rope_cols=(3 * width, 5 * width), vt_cols=(5 * width, 6 * width))
        sb_lo, sb_hi = _sb_attention(proj, sb_norm_gain[l], tq)
        df_o = _diff_attention(proj, vt, df_lambda[l], row1(df_subln_gain[l]), tq, lambda_init)
        x1, q = _proj_ln([sb_lo, sb_hi, df_o], w_o[l].astype(BF16), x2d, row1(ln1_gain[l]), row1(ln1_bias[l]),
                         w_mq[l].astype(BF16), _tile(seq, LN_ROWS), emit_bf16=False)
        kv = _matmul(mem2d, w_mkv[l].astype(BF16), mem2d.shape[0], PROJ_COLS)
        xa = _xattn(q, kv, _tile(seq, LN_ROWS))
        x2, x2b, pq = _proj_ln([xa], w_mo[l].astype(BF16), x1, row1(ln2_gain[l]), row1(ln2_bias[l]),
                               w_pq[l].astype(BF16), _tile(seq, LN_ROWS), emit_bf16=True)
        tables = _peer_route(pq, peer_sub_keys[l].astype(BF16), _tile(seq, ROUTE_TOKENS))
        x2d = _peer_dense(x2b, x2, peer_u[l].T.astype(BF16), peer_v[l].astype(BF16), tables,
                          row1(ln3_gain[l]), row1(ln3_bias[l]), _tile(seq, PEER_TOKENS), PEER_KEYS_PER_STEP)
    return x2d.reshape(b, seq, d)
```

```python
import functools
import math

import jax
import jax.numpy as jnp
from jax import lax
from jax.experimental import pallas as pl
from jax.experimental.pallas import tpu as pltpu

F32 = jnp.float32
BF16 = jnp.bfloat16

LANES = 128
BF16_SUBLANES = 16
MXU_WIDTH = 256
VMEM_LIMIT_BYTES = 56 * 1024 * 1024

ATTN_BLOCK = 256
PROJ_ROWS = 1024
PROJ_COLS = 1024
LN_ROWS = 256
ROUTE_TOKENS = 256
PEER_TOKENS = 512
PEER_KEYS_PER_STEP = 4

DEPTH = 1
SB_HEADS = 8
DIFF_HEADS = 8
HEAD_DIM = 128
DIFF_QK_DIM = 64
ROPE_DIM = 16
ROPE_THETA = 500000.0
MEM_HEADS = 4
PEER_HEADS = 8
N_KEYS = 128
PEER_TOPK = 16
LN_EPS = 1e-5
RMS_EPS = 1e-6
DEEPNORM_ALPHA = (2 * DEPTH) ** 0.25
NEG_INF = float("-inf")
SB_DEAD_LOG = -105.0
SUM_ROWS = 16
DIFF_UNROLL = 4


def _dot(a, b):
    return jnp.dot(a, b, preferred_element_type=F32)


def _dot_nt(a, b):
    return lax.dot_general(a, b, (((1,), (1,)), ((), ())), preferred_element_type=F32)


def _params(*semantics):
    return pltpu.CompilerParams(dimension_semantics=semantics, vmem_limit_bytes=VMEM_LIMIT_BYTES)


def _matmul_kernel(a_ref, b_ref, o_ref):
    o_ref[...] = _dot(a_ref[...], b_ref[...]).astype(o_ref.dtype)


def _matmul(a, b, tm, tn, out_dtype=BF16):
    m, k = a.shape
    n = b.shape[1]
    return pl.pallas_call(
        _matmul_kernel,
        grid=(m // tm, n // tn),
        in_specs=[pl.BlockSpec((tm, k), lambda i, j: (i, 0)),
                  pl.BlockSpec((k, tn), lambda i, j: (0, j))],
        out_specs=pl.BlockSpec((tm, tn), lambda i, j: (i, j)),
        out_shape=jax.ShapeDtypeStruct((m, n), out_dtype),
        compiler_params=_params("parallel", "parallel"),
        name="matmul",
    )(a, b)


def _inproj_kernel(a_ref, b_ref, c_ref, s1_ref, s2_ref, o_ref, vt_ref, *, rope_lo, rope_hi, vt_col):
    j = pl.program_id(1)
    acc = _dot(a_ref[...].astype(BF16), b_ref[...])
    o_ref[...] = acc.astype(o_ref.dtype)

    @pl.when(j == vt_col)
    def _():
        tk = vt_ref.shape[2]
        for kb in range(vt_ref.shape[0]):
            vt_ref[kb] = acc[kb * tk:(kb + 1) * tk, :].T.astype(vt_ref.dtype)

    @pl.when(jnp.logical_and(j >= rope_lo, j < rope_hi))
    def _():
        cos, sin_up, sin_dn = c_ref[...], s1_ref[...], s2_ref[...]
        for cc in range(acc.shape[1] // LANES):
            t = acc[:, cc * LANES:(cc + 1) * LANES]
            up = pltpu.roll(t, LANES - ROPE_DIM // 2, axis=1)
            dn = pltpu.roll(t, ROPE_DIM // 2, axis=1)
            o_ref[:, cc * LANES:(cc + 1) * LANES] = (t * cos + up * sin_up + dn * sin_dn).astype(o_ref.dtype)


def _rope_tables(seq):
    half = ROPE_DIM // 2
    inv_freq = jnp.power(ROPE_THETA, -jnp.arange(half, dtype=F32) * 2.0 / ROPE_DIM)
    ang = jnp.arange(seq).astype(F32)[:, None] * inv_freq[None, :]
    cos, sin = jnp.cos(ang), jnp.sin(ang)
    ones = jnp.ones((seq, DIFF_QK_DIM - ROPE_DIM), F32)
    zeros = jnp.zeros((seq, DIFF_QK_DIM - ROPE_DIM), F32)
    zh = jnp.zeros((seq, half), F32)
    c64 = jnp.concatenate([cos, cos, ones], axis=1)
    up64 = jnp.concatenate([-sin, zh, zeros], axis=1)
    dn64 = jnp.concatenate([zh, sin, zeros], axis=1)
    rep = LANES // DIFF_QK_DIM
    return jnp.tile(c64, (1, rep)), jnp.tile(up64, (1, rep)), jnp.tile(dn64, (1, rep))


def _inproj(x, w_in, tm, tn, tk, rope_cols, vt_cols):
    seq, k = x.shape
    n = w_in.shape[1]
    assert vt_cols == (n - tn, n)
    cos, sin_up, sin_dn = _rope_tables(seq)
    tab = pl.BlockSpec((tm, LANES), lambda i, j: (i, 0))
    kern = functools.partial(_inproj_kernel, rope_lo=rope_cols[0] // tn, rope_hi=rope_cols[1] // tn,
                             vt_col=vt_cols[0] // tn)
    return pl.pallas_call(
        kern,
        grid=(seq // tm, n // tn),
        in_specs=[pl.BlockSpec((tm, k), lambda i, j: (i, 0)),
                  pl.BlockSpec((k, tn), lambda i, j: (0, j)),
                  tab, tab, tab],
        out_specs=[pl.BlockSpec((tm, tn), lambda i, j: (i, j)),
                   pl.BlockSpec((tm // tk, tn, tk), lambda i, j: (i, 0, 0))],
        out_shape=[jax.ShapeDtypeStruct((seq, n), BF16),
                   jax.ShapeDtypeStruct((seq // tk, tn, tk), BF16)],
        compiler_params=_params("parallel", "arbitrary"),
        name="inproj_rope",
    )(x, w_in, cos, sin_up, sin_dn)


def _sb_kernel(qa_ref, ka_ref, va_ref, qb_ref, kb_ref, vb_ref, g_ref, oa_ref, ob_ref, *, tq, scale, head_stride):
    h = pl.program_id(0)
    i = pl.program_id(1)
    row = lax.broadcasted_iota(jnp.int32, (tq, tq), 0)
    col = lax.broadcasted_iota(jnp.int32, (tq, tq), 1)
    later = (row > col).astype(BF16)
    before = col < row

    def block_terms(q, k_ref, v_ref, kb, masked):
        start = pl.multiple_of(kb * tq, tq)
        k = k_ref[pl.ds(start, tq), :]
        z = _dot_nt(q, k) * scale
        sp = jnp.log(1.0 + jnp.exp(-jnp.abs(z)))
        log_beta = jnp.minimum(z, 0.0) - sp
        log_keep = -jnp.maximum(z, 0.0) - sp
        if masked:
            log_keep = jnp.where(before, log_keep, 0.0)
        hi = log_keep.astype(BF16)
        lo = (log_keep - hi.astype(F32)).astype(BF16)
        stick = _dot(hi, later) + _dot(lo, later)
        return log_beta + stick, jnp.sum(log_keep, axis=1, keepdims=True), v_ref[pl.ds(start, tq), :]

    def block_apply(terms, acc, c, masked, valid=None):
        log_w, keep_sum, v = terms
        w = jnp.exp(log_w + c)
        if masked:
            w = jnp.where(before, w, 0.0)
        if valid is not None:
            w = jnp.where(valid, w, 0.0)
            keep_sum = jnp.where(valid, keep_sum, 0.0)
        return acc + _dot(w.astype(BF16), v), c + keep_sum

    acc0 = jnp.zeros((tq, HEAD_DIM), F32)
    c0 = jnp.zeros((tq, 1), F32)
    heads = ((qa_ref[...], ka_ref, va_ref), (qb_ref[...], kb_ref, vb_ref))
    first = [(block_terms(q, k_ref, v_ref, i, True), block_terms(q, k_ref, v_ref, jnp.maximum(i - 1, 0), False))
             for q, k_ref, v_ref in heads]
    state = []
    for diag, prev in first:
        acc, c = block_apply(diag, acc0, c0, True)
        state.append(block_apply(prev, acc, c, False, valid=i >= 1))

    def live(carry):
        jj, _, _, cmax = carry
        return jnp.logical_and(jj < i, cmax > SB_DEAD_LOG)

    for (q, k_ref, v_ref), (acc, c), o_ref, head in zip(heads, state, (oa_ref, ob_ref), (h, h + head_stride)):
        def older(carry, q=q, k_ref=k_ref, v_ref=v_ref):
            jj, acc, c, _ = carry
            acc, c = block_apply(block_terms(q, k_ref, v_ref, i - 1 - jj, False), acc, c, False)
            return jj + 1, acc, c, jnp.max(c)

        _, acc, c, _ = lax.while_loop(live, older, (jnp.int32(1), acc, c, jnp.max(c)))
        g = g_ref[pl.ds(head, 1), :]
        ms = jnp.mean(acc * acc, axis=1, keepdims=True)
        o_ref[...] = (acc * lax.rsqrt(ms + RMS_EPS) * g).astype(o_ref.dtype)


def _sb_attention(proj, gain, tq):
    seq = proj.shape[0]
    half = SB_HEADS // 2
    kern = functools.partial(_sb_kernel, tq=tq, scale=1.0 / math.sqrt(HEAD_DIM), head_stride=half)
    specs = []
    for off in (0, half):
        specs += [pl.BlockSpec((tq, HEAD_DIM), lambda h, i, off=off: (i, off + h)),
                  pl.BlockSpec((seq, HEAD_DIM), lambda h, i, off=off: (0, SB_HEADS + off + h)),
                  pl.BlockSpec((seq, HEAD_DIM), lambda h, i, off=off: (0, 2 * SB_HEADS + off + h))]
    out = jax.ShapeDtypeStruct((seq, half * HEAD_DIM), BF16)
    return pl.pallas_call(
        kern,
        grid=(half, seq // tq),
        in_specs=specs + [pl.BlockSpec((SB_HEADS, HEAD_DIM), lambda h, i: (0, 0))],
        out_specs=[pl.BlockSpec((tq, HEAD_DIM), lambda h, i: (i, h))] * 2,
        out_shape=[out, out],
        compiler_params=_params("parallel", "arbitrary"),
        name="stickbreak_attn",
    )(*([proj] * 6), gain)


def _diff_kernel(q_ref, k_ref, vt_ref, lam_ref, g_ref, o_ref, s0_ref, s1_ref, p0_ref, p1_ref, acc_ref,
                 *, tq, lambda_init):
    i = pl.program_id(1)
    s_refs, p_refs = (s0_ref, s1_ref), (p0_ref, p1_ref)
    q = q_ref[...] * jnp.asarray(1.0 / math.sqrt(DIFF_QK_DIM), BF16)
    lane = lax.broadcasted_iota(jnp.int32, (tq, HEAD_DIM), 1)
    zero = jnp.zeros_like(q)
    qmaps = (jnp.where(lane < DIFF_QK_DIM, q, zero),
             jnp.where(lane >= DIFF_QK_DIM, q, zero))
    ones = jnp.ones((SUM_ROWS, tq), BF16)

    def scores(kb, mp):
        k = k_ref[pl.ds(pl.multiple_of(kb * tq, tq), tq), :]
        return _dot_nt(k, qmaps[mp])

    def softmax_step(s, m, slot, mp):
        m_new = jnp.maximum(m, jnp.max(s, axis=0, keepdims=True))
        p_refs[slot][mp] = jnp.exp(s - m_new).astype(BF16)
        return m_new, jnp.exp(m - m_new)

    def accumulate(t, alpha, slot, mp):
        kb = jnp.where(t == 0, i, jnp.minimum(t, i) - 1)
        v_aug = jnp.concatenate([vt_ref[kb], ones], axis=0)
        v_aug = jnp.where(t <= i, v_aug, jnp.zeros_like(v_aug))
        acc_ref[mp] = alpha * acc_ref[mp] + _dot(v_aug, p_refs[slot][mp])

    key = lax.broadcasted_iota(jnp.int32, (tq, tq), 0)
    qry = lax.broadcasted_iota(jnp.int32, (tq, tq), 1)
    causal = key <= qry
    m0 = jnp.full((1, tq), NEG_INF, F32)
    acc_ref[...] = jnp.zeros_like(acc_ref)
    stats = []
    for mp in range(2):
        s_refs[1][mp] = scores(0, mp)
        stats.extend(softmax_step(jnp.where(causal, scores(i, mp), NEG_INF), m0, 0, mp))
    stats = tuple(stats)

    def step(t, carry, cur):
        nxt = 1 - cur
        out = []
        for mp in range(2):
            m, alpha = carry[2 * mp], carry[2 * mp + 1]
            s_refs[cur][mp] = scores(jnp.minimum(t + 1, i - 1), mp)
            accumulate(t, alpha, cur, mp)
            out.extend(softmax_step(s_refs[nxt][mp], m, nxt, mp))
        return tuple(out)

    def unrolled(u, carry):
        for j in range(DIFF_UNROLL):
            carry = step(DIFF_UNROLL * u + j, carry, j % 2)
        return carry

    n_iter = (i + DIFF_UNROLL - 1) // DIFF_UNROLL
    stats = lax.fori_loop(0, n_iter, unrolled, stats)
    for mp in range(2):
        accumulate(n_iter * DIFF_UNROLL, stats[2 * mp + 1], 0, mp)

    lf = lam_ref[...]
    lam = (jnp.exp(jnp.sum(lf[0:1, :] * lf[1:2, :], axis=1, keepdims=True))
           - jnp.exp(jnp.sum(lf[2:3, :] * lf[3:4, :], axis=1, keepdims=True)) + lambda_init)
    o1 = acc_ref[0, :HEAD_DIM, :] / acc_ref[0, HEAD_DIM:HEAD_DIM + 1, :]
    o2 = acc_ref[1, :HEAD_DIM, :] / acc_ref[1, HEAD_DIM:HEAD_DIM + 1, :]
    d = o1 - lam * o2
    ms = jnp.mean(d * d, axis=0, keepdims=True)
    dn = (d * lax.rsqrt(ms + RMS_EPS)).T
    o_ref[...] = (dn * g_ref[...] * (1.0 - lambda_init)).astype(o_ref.dtype)


def _diff_attention(proj, vt, df_lambda, gain, tq, lambda_init):
    seq = proj.shape[0]
    qcol = 3 * SB_HEADS
    kcol = qcol + DIFF_HEADS
    kern = functools.partial(_diff_kernel, tq=tq, lambda_init=lambda_init)
    return pl.pallas_call(
        kern,
        grid=(DIFF_HEADS, seq // tq),
        in_specs=[pl.BlockSpec((tq, HEAD_DIM), lambda h, i: (i, qcol + h)),
                  pl.BlockSpec((seq, HEAD_DIM), lambda h, i: (0, kcol + h)),
                  pl.BlockSpec((seq // tq, HEAD_DIM, tq), lambda h, i: (0, h, 0)),
                  pl.BlockSpec((4, DIFF_QK_DIM), lambda h, i: (0, 0)),
                  pl.BlockSpec((1, HEAD_DIM), lambda h, i: (0, 0))],
        out_specs=pl.BlockSpec((tq, HEAD_DIM), lambda h, i: (i, h)),
        out_shape=jax.ShapeDtypeStruct((seq, DIFF_HEADS * HEAD_DIM), BF16),
        scratch_shapes=[pltpu.VMEM((2, tq, tq), F32), pltpu.VMEM((2, tq, tq), F32),
                        pltpu.VMEM((2, tq, tq), BF16), pltpu.VMEM((2, tq, tq), BF16),
                        pltpu.VMEM((2, HEAD_DIM + SUM_ROWS, tq), F32)],
        compiler_params=_params("parallel", "arbitrary"),
        name="diff_attn",
    )(proj, proj, vt, df_lambda, gain)


def _layer_norm(r, g, b):
    mu = jnp.mean(r, axis=1, keepdims=True)
    d = r - mu
    var = jnp.mean(d * d, axis=1, keepdims=True)
    return d * lax.rsqrt(var + LN_EPS) * g + b


def _proj_ln_kernel(*refs, n_parts, emit_bf16):
    a_refs = refs[:n_parts]
    w_ref, x_ref, g_ref, b_ref, wn_ref, o_ref = refs[n_parts:n_parts + 6]
    rest = refs[n_parts + 6:]
    y = None
    off = 0
    for a_ref in a_refs:
        kk = a_ref.shape[1]
        part = _dot(a_ref[...], w_ref[off:off + kk, :])
        y = part if y is None else y + part
        off += kk
    out = _layer_norm(DEEPNORM_ALPHA * x_ref[...] + y, g_ref[...], b_ref[...])
    o_ref[...] = out
    out_bf16 = out.astype(BF16)
    if emit_bf16:
        rest[0][...] = out_bf16
    rest[-1][...] = _dot(out_bf16, wn_ref[...]).astype(BF16)


def _proj_ln(parts, w, x, gain, bias, w_next, tm, emit_bf16):
    seq, d = x.shape
    dn = w_next.shape[1]
    kern = functools.partial(_proj_ln_kernel, n_parts=len(parts), emit_bf16=emit_bf16)
    row = lambda i: (i, 0)
    fixed = lambda i: (0, 0)
    out_specs = [pl.BlockSpec((tm, d), row)]
    out_shape = [jax.ShapeDtypeStruct((seq, d), F32)]
    if emit_bf16:
        out_specs.append(pl.BlockSpec((tm, d), row))
        out_shape.append(jax.ShapeDtypeStruct((seq, d), BF16))
    out_specs.append(pl.BlockSpec((tm, dn), row))
    out_shape.append(jax.ShapeDtypeStruct((seq, dn), BF16))
    return pl.pallas_call(
        kern,
        grid=(seq // tm,),
        in_specs=[pl.BlockSpec((tm, p.shape[1]), row) for p in parts] + [
            pl.BlockSpec(w.shape, fixed),
            pl.BlockSpec((tm, d), row),
            pl.BlockSpec((1, d), fixed),
            pl.BlockSpec((1, d), fixed),
            pl.BlockSpec(w_next.shape, fixed)],
        out_specs=out_specs,
        out_shape=out_shape,
        compiler_params=_params("parallel"),
        name="proj_residual_ln",
    )(*parts, w, x, gain, bias, w_next)


def _xattn_kernel(q_ref, kv_ref, o_ref, *, d_model, head_dim):
    scale = 1.0 / math.sqrt(head_dim)
    for hh in range(d_model // head_dim):
        lo = hh * head_dim
        q = q_ref[:, lo:lo + head_dim]
        k = kv_ref[:, lo:lo + head_dim]
        v = kv_ref[:, d_model + lo:d_model + lo + head_dim]
        s = _dot_nt(q, k) * scale
        m = jnp.max(s, axis=1, keepdims=True)
        p = jnp.exp(s - m)
        p = p / jnp.sum(p, axis=1, keepdims=True)
        o_ref[:, lo:lo + head_dim] = _dot(p.astype(BF16), v).astype(o_ref.dtype)


def _xattn(q, kv, tm):
    seq, d = q.shape
    kern = functools.partial(_xattn_kernel, d_model=d, head_dim=d // MEM_HEADS)
    return pl.pallas_call(
        kern,
        grid=(seq // tm,),
        in_specs=[pl.BlockSpec((tm, d), lambda i: (i, 0)),
                  pl.BlockSpec(kv.shape, lambda i: (0, 0))],
        out_specs=pl.BlockSpec((tm, d), lambda i: (i, 0)),
        out_shape=jax.ShapeDtypeStruct((seq, d), BF16),
        compiler_params=_params("parallel"),
        name="memory_xattn",
    )(q, kv)


def _top16(s, vals_ref, exact_ties):
    def step(k, carry):
        work, rank = carry
        m = jnp.max(work, axis=0, keepdims=True)
        sel = work == m
        if exact_ties:
            iota = lax.broadcasted_iota(jnp.int32, s.shape, 0).astype(F32)
            sel = iota == jnp.min(jnp.where(sel, iota, float(s.shape[0])), axis=0, keepdims=True)
        vals_ref[pl.ds(k, 1), :] = m
        return jnp.where(sel, NEG_INF, work), jnp.where(sel, lax.convert_element_type(k, F32), rank)

    _, rank = lax.fori_loop(0, PEER_TOPK, step, (s, jnp.full(s.shape, float(PEER_TOPK), F32)), unroll=True)
    tie_free = float(sum(range(PEER_TOPK)) + PEER_TOPK * (s.shape[0] - PEER_TOPK))
    return rank, tie_free - jnp.sum(rank, axis=0, keepdims=True)


CAND_KEEP = tuple(PEER_TOPK // (k1 + 1) for k1 in range(PEER_TOPK))
CAND_ROWS = -(-sum(CAND_KEEP) // 8) * 8


def _route_kernel(pq_ref, keys_ref, e2_ref, r2_ref, w1_ref, n1_ref, v1_ref, v2_ref, vc_ref, cand_ref):
    q = pq_ref[...]
    s1 = _dot_nt(keys_ref[0], q[:, :N_KEYS])
    s2 = _dot_nt(keys_ref[1], q[:, N_KEYS:])

    def route(exact_ties):
        rank1, deficit1 = _top16(s1, v1_ref, exact_ties)
        rank2, deficit2 = _top16(s2, v2_ref, exact_ties)
        cand_ref[...] = jnp.full(cand_ref.shape, NEG_INF, F32)
        row = lax.broadcasted_iota(jnp.int32, cand_ref.shape, 0)
        seg = jnp.zeros(cand_ref.shape, F32)
        off = 0
        for k1, keep in enumerate(CAND_KEEP):
            cand_ref[off:off + keep, :] = v1_ref[k1:k1 + 1, :] + v2_ref[0:keep, :]
            off += keep
            seg = seg + jnp.where(row >= off, 1.0, 0.0)
        crank, deficit3 = _top16(cand_ref[...], vc_ref, exact_ties)
        top = vc_ref[...]
        z = jnp.sum(jnp.exp(top - top[0:1, :]), axis=0, keepdims=True)
        chosen = jnp.where(crank < float(PEER_TOPK), 1.0, 0.0)
        rank1_packed = rank1.astype(BF16)
        n1 = jnp.zeros(s1.shape, BF16)
        for k1 in range(PEER_TOPK):
            cnt = jnp.sum(jnp.where(seg == float(k1), chosen, 0.0), axis=0, keepdims=True)
            cnt = jnp.tile(jnp.broadcast_to(cnt, (BF16_SUBLANES, cnt.shape[1])).astype(BF16),
                           (N_KEYS // BF16_SUBLANES, 1))
            n1 = jnp.where(rank1_packed == jnp.asarray(k1, BF16), cnt, n1)
        e2 = jnp.exp(s2 - v2_ref[0:1, :]).astype(e2_ref.dtype)
        w1 = jnp.exp(s1 - v1_ref[0:1, :]) / z
        tied = jnp.max(jnp.maximum(jnp.maximum(deficit1, deficit2), deficit3)) > 0.0
        return (e2, rank2.astype(r2_ref.dtype), w1, n1.astype(F32)), tied

    tables, tied = route(False)
    tables = lax.cond(tied, lambda _: route(True)[0], lambda t: t, tables)
    e2_ref[0], r2_ref[0], w1_ref[0], n1_ref[0] = tables


def _peer_route(pq, keys, tt):
    seq = pq.shape[0]
    qd = 2 * N_KEYS
    tab = pl.BlockSpec((1, N_KEYS, tt), lambda i, h: (h, 0, i))
    shape = jax.ShapeDtypeStruct((PEER_HEADS, N_KEYS, seq), F32)
    packed = jax.ShapeDtypeStruct((PEER_HEADS, N_KEYS, seq), BF16)
    return pl.pallas_call(
        _route_kernel,
        grid=(seq // tt, PEER_HEADS),
        in_specs=[pl.BlockSpec((tt, qd), lambda i, h: (i, h)),
                  pl.BlockSpec(keys.shape, lambda i, h: (0, 0, 0))],
        out_specs=[tab, tab, tab, tab],
        out_shape=[packed, packed, shape, shape],
        scratch_shapes=[pltpu.VMEM((PEER_TOPK, tt), F32),
                        pltpu.VMEM((PEER_TOPK, tt), F32),
                        pltpu.VMEM((PEER_TOPK, tt), F32),
                        pltpu.VMEM((CAND_ROWS, tt), F32)],
        compiler_params=_params("parallel", "parallel"),
        name="peer_route",
    )(pq, keys)


def _peer_kernel(xb_ref, u_ref, v_ref, e2_ref, r2_ref, w1_ref, n1_ref, x_ref, g_ref, b_ref,
                 o_ref, a_ref, gate_ref, *, keys_per_step, n_chunks):
    step = pl.program_id(0)
    cur, prev = step % 2, (step + 1) % 2
    chunk = step % n_chunks
    chunk_prev = (step + n_chunks - 1) % n_chunks

    @pl.when(step == 0)
    def _():
        a_ref[...] = jnp.zeros_like(a_ref)
        gate_ref[...] = jnp.zeros_like(gate_ref)

    @pl.when(jnp.logical_or(chunk_prev == 0, step == 0))
    def _():
        o_ref[...] = jnp.zeros_like(o_ref)

    gated = (a_ref[prev].astype(F32) * gate_ref[prev]).astype(BF16)
    o_ref[...] += _dot(gated, v_ref[...])

    hidden = _dot(xb_ref[...], u_ref[...])
    a_ref[cur] = (0.5 * hidden * (1.0 + lax.erf(hidden * math.sqrt(0.5)))).astype(BF16)

    zero = jnp.zeros(r2_ref.shape[1:], BF16)

    def rows(row):
        tile = jnp.broadcast_to(row, (BF16_SUBLANES, row.shape[1])).astype(BF16)
        return jnp.tile(tile, (N_KEYS // BF16_SUBLANES, 1))

    for part in range(keys_per_step):
        c = chunk * keys_per_step + part
        gt = None
        for h in range(PEER_HEADS):
            n1 = rows(n1_ref[h, pl.ds(c, 1), :])
            w1 = rows(w1_ref[h, pl.ds(c, 1), :])
            term = jnp.where(r2_ref[h] < n1, e2_ref[h] * w1, zero)
            gt = term if gt is None else gt + term
        gate_ref[cur, :, part * N_KEYS:(part + 1) * N_KEYS] = gt.astype(F32).T

    @pl.when(jnp.logical_and(chunk_prev == n_chunks - 1, step > 0))
    def _():
        o_ref[...] = _layer_norm(DEEPNORM_ALPHA * x_ref[...] + o_ref[...], g_ref[...], b_ref[...])


def _peer_dense(xb, x, ut, v, tables, gain, bias, tt, keys_per_step):
    seq, d = x.shape
    te = keys_per_step * N_KEYS
    n_chunks = v.shape[0] // te
    n_tiles = seq // tt
    tile = lambda s: jnp.minimum(s // n_chunks, n_tiles - 1)
    tile_prev = lambda s: jnp.maximum(s - 1, 0) // n_chunks
    tab = pl.BlockSpec((PEER_HEADS, N_KEYS, tt), lambda s: (0, 0, tile(s)))
    fixed = lambda s: (0, 0)
    kern = functools.partial(_peer_kernel, keys_per_step=keys_per_step, n_chunks=n_chunks)
    return pl.pallas_call(
        kern,
        grid=(n_tiles * n_chunks + 1,),
        in_specs=[pl.BlockSpec((tt, d), lambda s: (tile(s), 0)),
                  pl.BlockSpec((d, te), lambda s: (0, s % n_chunks)),
                  pl.BlockSpec((te, d), lambda s: ((s + n_chunks - 1) % n_chunks, 0)),
                  tab, tab, tab, tab,
                  pl.BlockSpec((tt, d), lambda s: (tile_prev(s), 0)),
                  pl.BlockSpec((1, d), fixed),
                  pl.BlockSpec((1, d), fixed)],
        out_specs=pl.BlockSpec((tt, d), lambda s: (tile_prev(s), 0)),
        out_shape=jax.ShapeDtypeStruct((seq, d), F32),
        scratch_shapes=[pltpu.VMEM((2, tt, te), BF16), pltpu.VMEM((2, tt, te), F32)],
        compiler_params=_params("arbitrary"),
        name="peer_dense",
    )(xb, ut, v, *tables, x, gain, bias)


def _tile(n, want):
    return min(n, want)


def kernel(x, mem, w_in, sb_norm_gain, df_lambda, df_subln_gain, w_o, ln1_gain, ln1_bias, w_mq, w_mkv, w_mo, ln2_gain, ln2_bias, w_pq, peer_sub_keys, peer_u, peer_v, ln3_gain, ln3_bias):
    b, seq, d = x.shape
    assert b == 1 and w_in.shape[0] == DEPTH
    x2d = x.reshape(seq, d)
    mem2d = mem.reshape(mem.shape[1], d).astype(BF16)
    for l in range(DEPTH):
        lambda_init = 0.8 - 0.6 * math.exp(-0.3 * l)
        row1 = lambda a: a.reshape(1, -1)
        tq = _tile(seq, ATTN_BLOCK)
        width = SB_HEADS * HEAD_DIM
        proj, vt = _inproj(x2d, w_in[l].astype(BF16), _tile(seq, PROJ_ROWS), width, tq,
                           rope_cols=(3 * width, 5 * width), vt_cols=(5 * width, 6 * width))
        sb_lo, sb_hi = _sb_attention(proj, sb_norm_gain[l], tq)
        df_o = _diff_attention(proj, vt, df_lambda[l], row1(df_subln_gain[l]), tq, lambda_init)
        x1, q = _proj_ln([sb_lo, sb_hi, df_o], w_o[l].astype(BF16), x2d, row1(ln1_gain[l]), row1(ln1_bias[l]),
                         w_mq[l].astype(BF16), _tile(seq, LN_ROWS), emit_bf16=False)
        kv = _matmul(mem2d, w_mkv[l].astype(BF16), mem2d.shape[0], PROJ_COLS)
        xa = _xattn(q, kv, _tile(seq, LN_ROWS))
        x2, x2b, pq = _proj_ln([xa], w_mo[l].astype(BF16), x1, row1(ln2_gain[l]), row1(ln2_bias[l]),
                               w_pq[l].astype(BF16), _tile(seq, LN_ROWS), emit_bf16=True)
        tables = _peer_route(pq, peer_sub_keys[l].astype(BF16), _tile(seq, ROUTE_TOKENS))
        x2d = _peer_dense(x2b, x2, peer_u[l].T.astype(BF16), peer_v[l].astype(BF16), tables,
                          row1(ln3_gain[l]), row1(ln3_bias[l]), _tile(seq, PEER_TOKENS), PEER_KEYS_PER_STEP)
    return x2d.reshape(b, seq, d)
```

```python
import functools
import math

import jax
import jax.numpy as jnp
from jax import lax
from jax.experimental import pallas as pl
from jax.experimental.pallas import tpu as pltpu

F32 = jnp.float32
BF16 = jnp.bfloat16

LANES = 128
BF16_SUBLANES = 16
VMEM_LIMIT_BYTES = 56 * 1024 * 1024

ATTN_BLOCK = 256
PROJ_ROWS = 1024
PROJ_COLS = 1024
LN_ROWS = 256
ROUTE_TOKENS = 256
PEER_TOKENS = 512
PEER_KEYS_PER_STEP = 4
SB_HEADS_PER_STEP = 4

DEPTH = 1
SB_HEADS = 8
DIFF_HEADS = 8
HEAD_DIM = 128
DIFF_QK_DIM = 64
ROPE_DIM = 16
ROPE_THETA = 500000.0
MEM_HEADS = 4
PEER_HEADS = 8
N_KEYS = 128
PEER_TOPK = 16
LN_EPS = 1e-5
RMS_EPS = 1e-6
DEEPNORM_ALPHA = (2 * DEPTH) ** 0.25
NEG_INF = float("-inf")
SB_DEAD_LOG = -105.0
SUM_ROWS = 16
DIFF_UNROLL = 4


def _dot(a, b):
    return jnp.dot(a, b, preferred_element_type=F32)


def _dot_nt(a, b):
    return lax.dot_general(a, b, (((1,), (1,)), ((), ())), preferred_element_type=F32)


def _params(*semantics):
    return pltpu.CompilerParams(dimension_semantics=semantics, vmem_limit_bytes=VMEM_LIMIT_BYTES)


def _matmul_kernel(a_ref, b_ref, o_ref):
    o_ref[...] = _dot(a_ref[...], b_ref[...]).astype(o_ref.dtype)


def _matmul(a, b, tm, tn, out_dtype=BF16):
    m, k = a.shape
    n = b.shape[1]
    return pl.pallas_call(
        _matmul_kernel,
        grid=(m // tm, n // tn),
        in_specs=[pl.BlockSpec((tm, k), lambda i, j: (i, 0)),
                  pl.BlockSpec((k, tn), lambda i, j: (0, j))],
        out_specs=pl.BlockSpec((tm, tn), lambda i, j: (i, j)),
        out_shape=jax.ShapeDtypeStruct((m, n), out_dtype),
        compiler_params=_params("parallel", "parallel"),
        name="matmul",
    )(a, b)


def _inproj_kernel(a_ref, b_ref, c_ref, s1_ref, s2_ref, o_ref, vt_ref, *, rope_lo, rope_hi, vt_col):
    j = pl.program_id(1)
    acc = _dot(a_ref[...].astype(BF16), b_ref[...])
    o_ref[...] = acc.astype(o_ref.dtype)

    @pl.when(j == vt_col)
    def _():
        tk = vt_ref.shape[2]
        for kb in range(vt_ref.shape[0]):
            vt_ref[kb] = acc[kb * tk:(kb + 1) * tk, :].T.astype(vt_ref.dtype)

    @pl.when(jnp.logical_and(j >= rope_lo, j < rope_hi))
    def _():
        cos, sin_up, sin_dn = c_ref[...], s1_ref[...], s2_ref[...]
        for cc in range(acc.shape[1] // LANES):
            t = acc[:, cc * LANES:(cc + 1) * LANES]
            up = pltpu.roll(t, LANES - ROPE_DIM // 2, axis=1)
            dn = pltpu.roll(t, ROPE_DIM // 2, axis=1)
            o_ref[:, cc * LANES:(cc + 1) * LANES] = (t * cos + up * sin_up + dn * sin_dn).astype(o_ref.dtype)


def _rope_tables(seq):
    half = ROPE_DIM // 2
    inv_freq = jnp.power(ROPE_THETA, -jnp.arange(half, dtype=F32) * 2.0 / ROPE_DIM)
    ang = jnp.arange(seq).astype(F32)[:, None] * inv_freq[None, :]
    cos, sin = jnp.cos(ang), jnp.sin(ang)
    ones = jnp.ones((seq, DIFF_QK_DIM - ROPE_DIM), F32)
    zeros = jnp.zeros((seq, DIFF_QK_DIM - ROPE_DIM), F32)
    zh = jnp.zeros((seq, half), F32)
    c64 = jnp.concatenate([cos, cos, ones], axis=1)
    up64 = jnp.concatenate([-sin, zh, zeros], axis=1)
    dn64 = jnp.concatenate([zh, sin, zeros], axis=1)
    rep = LANES // DIFF_QK_DIM
    return jnp.tile(c64, (1, rep)), jnp.tile(up64, (1, rep)), jnp.tile(dn64, (1, rep))


def _inproj(x, w_in, tm, tn, tk, rope_cols, vt_cols):
    seq, k = x.shape
    n = w_in.shape[1]
    assert vt_cols == (n - tn, n)
    cos, sin_up, sin_dn = _rope_tables(seq)
    tab = pl.BlockSpec((tm, LANES), lambda i, j: (i, 0))
    kern = functools.partial(_inproj_kernel, rope_lo=rope_cols[0] // tn, rope_hi=rope_cols[1] // tn,
                             vt_col=vt_cols[0] // tn)
    return pl.pallas_call(
        kern,
        grid=(seq // tm, n // tn),
        in_specs=[pl.BlockSpec((tm, k), lambda i, j: (i, 0)),
                  pl.BlockSpec((k, tn), lambda i, j: (0, j)),
                  tab, tab, tab],
        out_specs=[pl.BlockSpec((tm, tn), lambda i, j: (i, j)),
                   pl.BlockSpec((tm // tk, tn, tk), lambda i, j: (i, 0, 0))],
        out_shape=[jax.ShapeDtypeStruct((seq, n), BF16),
                   jax.ShapeDtypeStruct((seq // tk, tn, tk), BF16)],
        compiler_params=_params("parallel", "arbitrary"),
        name="inproj_rope",
    )(x, w_in, cos, sin_up, sin_dn)


def _sb_kernel(*refs, tq, scale, n_heads, head_stride):
    g_ref = refs[3 * n_heads]
    o_refs = refs[3 * n_heads + 1:]
    h = pl.program_id(0)
    i = pl.program_id(1)
    row = lax.broadcasted_iota(jnp.int32, (tq, tq), 0)
    col = lax.broadcasted_iota(jnp.int32, (tq, tq), 1)
    later = (row > col).astype(BF16)
    before = col < row

    def block_terms(q, k_ref, v_ref, kb, masked):
        start = pl.multiple_of(kb * tq, tq)
        k = k_ref[pl.ds(start, tq), :]
        z = _dot_nt(q, k) * scale
        sp = jnp.log(1.0 + jnp.exp(-jnp.abs(z)))
        log_beta = jnp.minimum(z, 0.0) - sp
        log_keep = -jnp.maximum(z, 0.0) - sp
        if masked:
            log_keep = jnp.where(before, log_keep, 0.0)
        hi = log_keep.astype(BF16)
        lo = (log_keep - hi.astype(F32)).astype(BF16)
        stick = _dot(hi, later) + _dot(lo, later)
        return log_beta + stick, jnp.sum(log_keep, axis=1, keepdims=True), v_ref[pl.ds(start, tq), :]

    def block_apply(terms, acc, c, masked, valid=None):
        log_w, keep_sum, v = terms
        w = jnp.exp(log_w + c)
        if masked:
            w = jnp.where(before, w, 0.0)
        if valid is not None:
            w = jnp.where(valid, w, 0.0)
            keep_sum = jnp.where(valid, keep_sum, 0.0)
        return acc + _dot(w.astype(BF16), v), c + keep_sum

    acc0 = jnp.zeros((tq, HEAD_DIM), F32)
    c0 = jnp.zeros((tq, 1), F32)
    heads = [(refs[3 * n][...], refs[3 * n + 1], refs[3 * n + 2]) for n in range(n_heads)]
    first = [(block_terms(q, k_ref, v_ref, i, True), block_terms(q, k_ref, v_ref, jnp.maximum(i - 1, 0), False))
             for q, k_ref, v_ref in heads]
    state = []
    for diag, prev in first:
        acc, c = block_apply(diag, acc0, c0, True)
        state.append(block_apply(prev, acc, c, False, valid=i >= 1))

    def live(carry):
        jj, _, _, cmax = carry
        return jnp.logical_and(jj < i, cmax > SB_DEAD_LOG)

    for n, ((q, k_ref, v_ref), (acc, c), o_ref) in enumerate(zip(heads, state, o_refs)):
        def older(carry, q=q, k_ref=k_ref, v_ref=v_ref):
            jj, acc, c, _ = carry
            acc, c = block_apply(block_terms(q, k_ref, v_ref, i - 1 - jj, False), acc, c, False)
            return jj + 1, acc, c, jnp.max(c)

        _, acc, c, _ = lax.while_loop(live, older, (jnp.int32(1), acc, c, jnp.max(c)))
        g = g_ref[pl.ds(h + n * head_stride, 1), :]
        ms = jnp.mean(acc * acc, axis=1, keepdims=True)
        o_ref[...] = (acc * lax.rsqrt(ms + RMS_EPS) * g).astype(o_ref.dtype)


def _sb_attention(proj, gain, tq, n_heads):
    seq = proj.shape[0]
    stride = SB_HEADS // n_heads
    kern = functools.partial(_sb_kernel, tq=tq, scale=1.0 / math.sqrt(HEAD_DIM), n_heads=n_heads,
                             head_stride=stride)
    specs = []
    for n in range(n_heads):
        off = n * stride
        specs += [pl.BlockSpec((tq, HEAD_DIM), lambda h, i, off=off: (i, off + h)),
                  pl.BlockSpec((seq, HEAD_DIM), lambda h, i, off=off: (0, SB_HEADS + off + h)),
                  pl.BlockSpec((seq, HEAD_DIM), lambda h, i, off=off: (0, 2 * SB_HEADS + off + h))]
    out = jax.ShapeDtypeStruct((seq, stride * HEAD_DIM), BF16)
    return pl.pallas_call(
        kern,
        grid=(stride, seq // tq),
        in_specs=specs + [pl.BlockSpec((SB_HEADS, HEAD_DIM), lambda h, i: (0, 0))],
        out_specs=[pl.BlockSpec((tq, HEAD_DIM), lambda h, i: (i, h))] * n_heads,
        out_shape=[out] * n_heads,
        compiler_params=_params("parallel", "arbitrary"),
        name="stickbreak_attn",
    )(*([proj] * (3 * n_heads)), gain)


def _diff_kernel(q_ref, k_ref, vt_ref, lam_ref, g_ref, o_ref, s0_ref, s1_ref, p0_ref, p1_ref, acc_ref,
                 *, tq, lambda_init):
    i = pl.program_id(1)
    s_refs, p_refs = (s0_ref, s1_ref), (p0_ref, p1_ref)
    q = q_ref[...] * jnp.asarray(1.0 / math.sqrt(DIFF_QK_DIM), BF16)
    lane = lax.broadcasted_iota(jnp.int32, (tq, HEAD_DIM), 1)
    zero = jnp.zeros_like(q)
    qmaps = (jnp.where(lane < DIFF_QK_DIM, q, zero),
             jnp.where(lane >= DIFF_QK_DIM, q, zero))
    ones = jnp.ones((SUM_ROWS, tq), BF16)

    def scores(kb, mp):
        k = k_ref[pl.ds(pl.multiple_of(kb * tq, tq), tq), :]
        return _dot_nt(k, qmaps[mp])

    def softmax_step(s, m, slot, mp):
        m_new = jnp.maximum(m, jnp.max(s, axis=0, keepdims=True))
        p_refs[slot][mp] = jnp.exp(s - m_new).astype(BF16)
        return m_new, jnp.exp(m - m_new)

    def accumulate(t, alpha, slot, mp):
        kb = jnp.where(t == 0, i, jnp.minimum(t, i) - 1)
        v_aug = jnp.concatenate([vt_ref[kb], ones], axis=0)
        v_aug = jnp.where(t <= i, v_aug, jnp.zeros_like(v_aug))
        acc_ref[mp] = alpha * acc_ref[mp] + _dot(v_aug, p_refs[slot][mp])

    key = lax.broadcasted_iota(jnp.int32, (tq, tq), 0)
    qry = lax.broadcasted_iota(jnp.int32, (tq, tq), 1)
    causal = key <= qry
    m0 = jnp.full((1, tq), NEG_INF, F32)
    acc_ref[...] = jnp.zeros_like(acc_ref)
    stats = []
    for mp in range(2):
        s_refs[1][mp] = scores(0, mp)
        stats.extend(softmax_step(jnp.where(causal, scores(i, mp), NEG_INF), m0, 0, mp))
    stats = tuple(stats)

    def step(t, carry, cur):
        nxt = 1 - cur
        out = []
        for mp in range(2):
            m, alpha = carry[2 * mp], carry[2 * mp + 1]
            s_refs[cur][mp] = scores(jnp.minimum(t + 1, i - 1), mp)
            accumulate(t, alpha, cur, mp)
            out.extend(softmax_step(s_refs[nxt][mp], m, nxt, mp))
        return tuple(out)

    def unrolled(u, carry):
        for j in range(DIFF_UNROLL):
            carry = step(DIFF_UNROLL * u + j, carry, j % 2)
        return carry

    n_iter = (i + DIFF_UNROLL - 1) // DIFF_UNROLL
    stats = lax.fori_loop(0, n_iter, unrolled, stats)
    for mp in range(2):
        accumulate(n_iter * DIFF_UNROLL, stats[2 * mp + 1], 0, mp)

    lf = lam_ref[...]
    lam = (jnp.exp(jnp.sum(lf[0:1, :] * lf[1:2, :], axis=1, keepdims=True))
           - jnp.exp(jnp.sum(lf[2:3, :] * lf[3:4, :], axis=1, keepdims=True)) + lambda_init)
    o1 = acc_ref[0, :HEAD_DIM, :] / acc_ref[0, HEAD_DIM:HEAD_DIM + 1, :]
    o2 = acc_ref[1, :HEAD_DIM, :] / acc_ref[1, HEAD_DIM:HEAD_DIM + 1, :]
    d = o1 - lam * o2
    ms = jnp.mean(d * d, axis=0, keepdims=True)
    dn = (d * lax.rsqrt(ms + RMS_EPS)).T
    o_ref[...] = (dn * g_ref[...] * (1.0 - lambda_init)).astype(o_ref.dtype)


def _diff_attention(proj, vt, df_lambda, gain, tq, lambda_init):
    seq = proj.shape[0]
    qcol = 3 * SB_HEADS
    kcol = qcol + DIFF_HEADS
    kern = functools.partial(_diff_kernel, tq=tq, lambda_init=lambda_init)
    return pl.pallas_call(
        kern,
        grid=(DIFF_HEADS, seq // tq),
        in_specs=[pl.BlockSpec((tq, HEAD_DIM), lambda h, i: (i, qcol + h)),
                  pl.BlockSpec((seq, HEAD_DIM), lambda h, i: (0, kcol + h)),
                  pl.BlockSpec((seq // tq, HEAD_DIM, tq), lambda h, i: (0, h, 0)),
                  pl.BlockSpec((4, DIFF_QK_DIM), lambda h, i: (0, 0)),
                  pl.BlockSpec((1, HEAD_DIM), lambda h, i: (0, 0))],
        out_specs=pl.BlockSpec((tq, HEAD_DIM), lambda h, i: (i, h)),
        out_shape=jax.ShapeDtypeStruct((seq, DIFF_HEADS * HEAD_DIM), BF16),
        scratch_shapes=[pltpu.VMEM((2, tq, tq), F32), pltpu.VMEM((2, tq, tq), F32),
                        pltpu.VMEM((2, tq, tq), BF16), pltpu.VMEM((2, tq, tq), BF16),
                        pltpu.VMEM((2, HEAD_DIM + SUM_ROWS, tq), F32)],
        compiler_params=_params("parallel", "arbitrary"),
        name="diff_attn",
    )(proj, proj, vt, df_lambda, gain)


def _layer_norm(r, g, b):
    mu = jnp.mean(r, axis=1, keepdims=True)
    d = r - mu
    var = jnp.mean(d * d, axis=1, keepdims=True)
    return d * lax.rsqrt(var + LN_EPS) * g + b


def _proj_ln_kernel(*refs, n_parts, emit_bf16):
    a_refs = refs[:n_parts]
    w_ref, x_ref, g_ref, b_ref, wn_ref, o_ref = refs[n_parts:n_parts + 6]
    rest = refs[n_parts + 6:]
    y = None
    off = 0
    for a_ref in a_refs:
        kk = a_ref.shape[1]
        part = _dot(a_ref[...], w_ref[off:off + kk, :])
        y = part if y is None else y + part
        off += kk
    out = _layer_norm(DEEPNORM_ALPHA * x_ref[...] + y, g_ref[...], b_ref[...])
    o_ref[...] = out
    out_bf16 = out.astype(BF16)
    if emit_bf16:
        rest[0][...] = out_bf16
    rest[-1][...] = _dot(out_bf16, wn_ref[...]).astype(BF16)


def _proj_ln(parts, w, x, gain, bias, w_next, tm, emit_bf16):
    seq, d = x.shape
    dn = w_next.shape[1]
    kern = functools.partial(_proj_ln_kernel, n_parts=len(parts), emit_bf16=emit_bf16)
    row = lambda i: (i, 0)
    fixed = lambda i: (0, 0)
    out_specs = [pl.BlockSpec((tm, d), row)]
    out_shape = [jax.ShapeDtypeStruct((seq, d), F32)]
    if emit_bf16:
        out_specs.append(pl.BlockSpec((tm, d), row))
        out_shape.append(jax.ShapeDtypeStruct((seq, d), BF16))
    out_specs.append(pl.BlockSpec((tm, dn), row))
    out_shape.append(jax.ShapeDtypeStruct((seq, dn), BF16))
    return pl.pallas_call(
        kern,
        grid=(seq // tm,),
        in_specs=[pl.BlockSpec((tm, p.shape[1]), row) for p in parts] + [
            pl.BlockSpec(w.shape, fixed),
            pl.BlockSpec((tm, d), row),
            pl.BlockSpec((1, d), fixed),
            pl.BlockSpec((1, d), fixed),
            pl.BlockSpec(w_next.shape, fixed)],
        out_specs=out_specs,
        out_shape=out_shape,
        compiler_params=_params("parallel"),
        name="proj_residual_ln",
    )(*parts, w, x, gain, bias, w_next)


def _xattn_kernel(q_ref, kv_ref, o_ref, *, d_model, head_dim):
    scale = 1.0 / math.sqrt(head_dim)
    for hh in range(d_model // head_dim):
        lo = hh * head_dim
        q = q_ref[:, lo:lo + head_dim]
        k = kv_ref[:, lo:lo + head_dim]
        v = kv_ref[:, d_model + lo:d_model + lo + head_dim]
        s = _dot_nt(q, k) * scale
        m = jnp.max(s, axis=1, keepdims=True)
        p = jnp.exp(s - m)
        p = p / jnp.sum(p, axis=1, keepdims=True)
        o_ref[:, lo:lo + head_dim] = _dot(p.astype(BF16), v).astype(o_ref.dtype)


def _xattn(q, kv, tm):
    seq, d = q.shape
    kern = functools.partial(_xattn_kernel, d_model=d, head_dim=d // MEM_HEADS)
    return pl.pallas_call(
        kern,
        grid=(seq // tm,),
        in_specs=[pl.BlockSpec((tm, d), lambda i: (i, 0)),
                  pl.BlockSpec(kv.shape, lambda i: (0, 0))],
        out_specs=pl.BlockSpec((tm, d), lambda i: (i, 0)),
        out_shape=jax.ShapeDtypeStruct((seq, d), BF16),
        compiler_params=_params("parallel"),
        name="memory_xattn",
    )(q, kv)


def _top16(s, vals_ref, exact_ties):
    def step(k, carry):
        work, rank = carry
        m = jnp.max(work, axis=0, keepdims=True)
        sel = work == m
        if exact_ties:
            iota = lax.broadcasted_iota(jnp.int32, s.shape, 0).astype(F32)
            sel = iota == jnp.min(jnp.where(sel, iota, float(s.shape[0])), axis=0, keepdims=True)
        vals_ref[pl.ds(k, 1), :] = m
        return jnp.where(sel, NEG_INF, work), jnp.where(sel, lax.convert_element_type(k, F32), rank)

    _, rank = lax.fori_loop(0, PEER_TOPK, step, (s, jnp.full(s.shape, float(PEER_TOPK), F32)), unroll=True)
    tie_free = float(sum(range(PEER_TOPK)) + PEER_TOPK * (s.shape[0] - PEER_TOPK))
    return rank, tie_free - jnp.sum(rank, axis=0, keepdims=True)


CAND_KEEP = tuple(PEER_TOPK // (k1 + 1) for k1 in range(PEER_TOPK))
CAND_ROWS = -(-sum(CAND_KEEP) // 8) * 8


def _route_kernel(pq_ref, keys_ref, e2_ref, r2_ref, w1_ref, n1_ref, v1_ref, v2_ref, vc_ref, cand_ref):
    q = pq_ref[...]
    s1 = _dot_nt(keys_ref[0], q[:, :N_KEYS])
    s2 = _dot_nt(keys_ref[1], q[:, N_KEYS:])

    def route(exact_ties):
        rank1, deficit1 = _top16(s1, v1_ref, exact_ties)
        rank2, deficit2 = _top16(s2, v2_ref, exact_ties)
        cand_ref[...] = jnp.full(cand_ref.shape, NEG_INF, F32)
        off = 0
        for k1, keep in enumerate(CAND_KEEP):
            cand_ref[off:off + keep, :] = v1_ref[k1:k1 + 1, :] + v2_ref[0:keep, :]
            off += keep
        crank, deficit3 = _top16(cand_ref[...], vc_ref, exact_ties)
        top = vc_ref[...]
        z = jnp.sum(jnp.exp(top - top[0:1, :]), axis=0, keepdims=True)
        cand_ref[...] = jnp.where(crank < float(PEER_TOPK), 1.0, 0.0)
        rank1_packed = rank1.astype(BF16)
        n1 = jnp.zeros(s1.shape, BF16)
        off = 0
        for k1, keep in enumerate(CAND_KEEP):
            cnt = jnp.sum(cand_ref[off:off + keep, :], axis=0, keepdims=True)
            off += keep
            cnt = jnp.tile(jnp.broadcast_to(cnt, (BF16_SUBLANES, cnt.shape[1])).astype(BF16),
                           (N_KEYS // BF16_SUBLANES, 1))
            n1 = jnp.where(rank1_packed == jnp.asarray(k1, BF16), cnt, n1)
        e2 = jnp.exp(s2 - v2_ref[0:1, :]).astype(e2_ref.dtype)
        w1 = jnp.exp(s1 - v1_ref[0:1, :]) / z
        tied = jnp.max(jnp.maximum(jnp.maximum(deficit1, deficit2), deficit3)) > 0.0
        return (e2, rank2.astype(r2_ref.dtype), w1, n1.astype(F32)), tied

    tables, tied = route(False)
    tables = lax.cond(tied, lambda _: route(True)[0], lambda t: t, tables)
    e2_ref[0], r2_ref[0], w1_ref[0], n1_ref[0] = tables


def _peer_route(pq, keys, tt):
    seq = pq.shape[0]
    qd = 2 * N_KEYS
    tab = pl.BlockSpec((1, N_KEYS, tt), lambda i, h: (h, 0, i))
    shape = jax.ShapeDtypeStruct((PEER_HEADS, N_KEYS, seq), F32)
    packed = jax.ShapeDtypeStruct((PEER_HEADS, N_KEYS, seq), BF16)
    return pl.pallas_call(
        _route_kernel,
        grid=(seq // tt, PEER_HEADS),
        in_specs=[pl.BlockSpec((tt, qd), lambda i, h: (i, h)),
                  pl.BlockSpec(keys.shape, lambda i, h: (0, 0, 0))],
        out_specs=[tab, tab, tab, tab],
        out_shape=[packed, packed, shape, shape],
        scratch_shapes=[pltpu.VMEM((PEER_TOPK, tt), F32),
                        pltpu.VMEM((PEER_TOPK, tt), F32),
                        pltpu.VMEM((PEER_TOPK, tt), F32),
                        pltpu.VMEM((CAND_ROWS, tt), F32)],
        compiler_params=_params("parallel", "parallel"),
        name="peer_route",
    )(pq, keys)


def _peer_kernel(xb_ref, u_ref, v_ref, e2_ref, r2_ref, w1_ref, n1_ref, x_ref, g_ref, b_ref,
                 o_ref, a_ref, gate_ref, *, keys_per_step, n_chunks):
    step = pl.program_id(0)
    cur, prev = step % 2, (step + 1) % 2
    chunk = step % n_chunks
    chunk_prev = (step + n_chunks - 1) % n_chunks

    @pl.when(step == 0)
    def _():
        a_ref[...] = jnp.zeros_like(a_ref)
        gate_ref[...] = jnp.zeros_like(gate_ref)

    @pl.when(jnp.logical_or(chunk_prev == 0, step == 0))
    def _():
        o_ref[...] = jnp.zeros_like(o_ref)

    gated = (a_ref[prev].astype(F32) * gate_ref[prev]).astype(BF16)
    o_ref[...] += _dot(gated, v_ref[...])

    hidden = _dot(xb_ref[...], u_ref[...])
    a_ref[cur] = (0.5 * hidden * (1.0 + lax.erf(hidden * math.sqrt(0.5)))).astype(BF16)

    zero = jnp.zeros(r2_ref.shape[1:], BF16)

    def rows(row):
        tile = jnp.broadcast_to(row, (BF16_SUBLANES, row.shape[1])).astype(BF16)
        return jnp.tile(tile, (N_KEYS // BF16_SUBLANES, 1))

    for part in range(keys_per_step):
        c = chunk * keys_per_step + part
        gt = None
        for h in range(PEER_HEADS):
            n1 = rows(n1_ref[h, pl.ds(c, 1), :])
            w1 = rows(w1_ref[h, pl.ds(c, 1), :])
            term = jnp.where(r2_ref[h] < n1, e2_ref[h] * w1, zero)
            gt = term if gt is None else gt + term
        gate_ref[cur, :, part * N_KEYS:(part + 1) * N_KEYS] = gt.astype(F32).T

    @pl.when(jnp.logical_and(chunk_prev == n_chunks - 1, step > 0))
    def _():
        o_ref[...] = _layer_norm(DEEPNORM_ALPHA * x_ref[...] + o_ref[...], g_ref[...], b_ref[...])


def _peer_dense(xb, x, ut, v, tables, gain, bias, tt, keys_per_step):
    seq, d = x.shape
    te = keys_per_step * N_KEYS
    n_chunks = v.shape[0] // te
    n_tiles = seq // tt
    tile = lambda s: jnp.minimum(s // n_chunks, n_tiles - 1)
    tile_prev = lambda s: jnp.maximum(s - 1, 0) // n_chunks
    tab = pl.BlockSpec((PEER_HEADS, N_KEYS, tt), lambda s: (0, 0, tile(s)))
    fixed = lambda s: (0, 0)
    kern = functools.partial(_peer_kernel, keys_per_step=keys_per_step, n_chunks=n_chunks)
    return pl.pallas_call(
        kern,
        grid=(n_tiles * n_chunks + 1,),
        in_specs=[pl.BlockSpec((tt, d), lambda s: (tile(s), 0)),
                  pl.BlockSpec((d, te), lambda s: (0, s % n_chunks)),
                  pl.BlockSpec((te, d), lambda s: ((s + n_chunks - 1) % n_chunks, 0)),
                  tab, tab, tab, tab,
                  pl.BlockSpec((tt, d), lambda s: (tile_prev(s), 0)),
                  pl.BlockSpec((1, d), fixed),
                  pl.BlockSpec((1, d), fixed)],
        out_specs=pl.BlockSpec((tt, d), lambda s: (tile_prev(s), 0)),
        out_shape=jax.ShapeDtypeStruct((seq, d), F32),
        scratch_shapes=[pltpu.VMEM((2, tt, te), BF16), pltpu.VMEM((2, tt, te), F32)],
        compiler_params=_params("arbitrary"),
        name="peer_dense",
    )(xb, ut, v, *tables, x, gain, bias)


def _tile(n, want):
    return min(n, want)


def kernel(x, mem, w_in, sb_norm_gain, df_lambda, df_subln_gain, w_o, ln1_gain, ln1_bias, w_mq, w_mkv, w_mo, ln2_gain, ln2_bias, w_pq, peer_sub_keys, peer_u, peer_v, ln3_gain, ln3_bias):
    b, seq, d = x.shape
    assert b == 1 and w_in.shape[0] == DEPTH
    x2d = x.reshape(seq, d)
    mem2d = mem.reshape(mem.shape[1], d).astype(BF16)
    for l in range(DEPTH):
        lambda_init = 0.8 - 0.6 * math.exp(-0.3 * l)
        row1 = lambda a: a.reshape(1, -1)
        tq = _tile(seq, ATTN_BLOCK)
        width = SB_HEADS * HEAD_DIM
        proj, vt = _inproj(x2d, w_in[l].astype(BF16), _tile(seq, PROJ_ROWS), width, tq,
                           rope_cols=(3 * width, 5 * width), vt_cols=(5 * width, 6 * width))
        sb_parts = _sb_attention(proj, sb_norm_gain[l], tq, SB_HEADS_PER_STEP)
        df_o = _diff_attention(proj, vt, df_lambda[l], row1(df_subln_gain[l]), tq, lambda_init)
        x1, q = _proj_ln([*sb_parts, df_o], w_o[l].astype(BF16), x2d, row1(ln1_gain[l]), row1(ln1_bias[l]),
                         w_mq[l].astype(BF16), _tile(seq, LN_ROWS), emit_bf16=False)
        kv = _matmul(mem2d, w_mkv[l].astype(BF16), mem2d.shape[0], PROJ_COLS)
        xa = _xattn(q, kv, _tile(seq, LN_ROWS))
        x2, x2b, pq = _proj_ln([xa], w_mo[l].astype(BF16), x1, row1(ln2_gain[l]), row1(ln2_bias[l]),
                               w_pq[l].astype(BF16), _tile(seq, LN_ROWS), emit_bf16=True)
        tables = _peer_route(pq, peer_sub_keys[l].astype(BF16), _tile(seq, ROUTE_TOKENS))
        x2d = _peer_dense(x2b, x2, peer_u[l].T.astype(BF16), peer_v[l].astype(BF16), tables,
                          row1(ln3_gain[l]), row1(ln3_bias[l]), _tile(seq, PEER_TOKENS), PEER_KEYS_PER_STEP)
    return x2d.reshape(b, seq, d)
```

```python
import functools
import math

import jax
import jax.numpy as jnp
from jax import lax
from jax.experimental import pallas as pl
from jax.experimental.pallas import tpu as pltpu

F32 = jnp.float32
BF16 = jnp.bfloat16

LANES = 128
BF16_SUBLANES = 16
VMEM_LIMIT_BYTES = 56 * 1024 * 1024

ATTN_BLOCK = 256
PROJ_ROWS = 1024
PROJ_COLS = 1024
LN_ROWS = 256
ROUTE_TOKENS = 256
PEER_TOKENS = 512
PEER_KEYS_PER_STEP = 4
SB_HEADS_PER_STEP = 4

DEPTH = 1
SB_HEADS = 8
DIFF_HEADS = 8
HEAD_DIM = 128
DIFF_QK_DIM = 64
ROPE_DIM = 16
ROPE_THETA = 500000.0
MEM_HEADS = 4
PEER_HEADS = 8
N_KEYS = 128
PEER_TOPK = 16
LN_EPS = 1e-5
RMS_EPS = 1e-6
DEEPNORM_ALPHA = (2 * DEPTH) ** 0.25
NEG_INF = float("-inf")
SB_DEAD_LOG = -105.0
SUM_ROWS = 16
DIFF_UNROLL = 4


def _dot(a, b):
    return jnp.dot(a, b, preferred_element_type=F32)


def _dot_nt(a, b):
    return lax.dot_general(a, b, (((1,), (1,)), ((), ())), preferred_element_type=F32)


def _params(*semantics):
    return pltpu.CompilerParams(dimension_semantics=semantics, vmem_limit_bytes=VMEM_LIMIT_BYTES)


def _matmul_kernel(a_ref, b_ref, o_ref):
    o_ref[...] = _dot(a_ref[...], b_ref[...]).astype(o_ref.dtype)


def _matmul(a, b, tm, tn, out_dtype=BF16):
    m, k = a.shape
    n = b.shape[1]
    return pl.pallas_call(
        _matmul_kernel,
        grid=(m // tm, n // tn),
        in_specs=[pl.BlockSpec((tm, k), lambda i, j: (i, 0)),
                  pl.BlockSpec((k, tn), lambda i, j: (0, j))],
        out_specs=pl.BlockSpec((tm, tn), lambda i, j: (i, j)),
        out_shape=jax.ShapeDtypeStruct((m, n), out_dtype),
        compiler_params=_params("parallel", "parallel"),
        name="matmul",
    )(a, b)


def _inproj_kernel(a_ref, b_ref, c_ref, s1_ref, s2_ref, o_ref, vt_ref, *, rope_lo, rope_hi, vt_col):
    j = pl.program_id(1)
    acc = _dot(a_ref[...].astype(BF16), b_ref[...])
    o_ref[...] = acc.astype(o_ref.dtype)

    @pl.when(j == vt_col)
    def _():
        tk = vt_ref.shape[2]
        for kb in range(vt_ref.shape[0]):
            vt_ref[kb] = acc[kb * tk:(kb + 1) * tk, :].T.astype(vt_ref.dtype)

    @pl.when(jnp.logical_and(j >= rope_lo, j < rope_hi))
    def _():
        cos, sin_up, sin_dn = c_ref[...], s1_ref[...], s2_ref[...]
        for cc in range(acc.shape[1] // LANES):
            t = acc[:, cc * LANES:(cc + 1) * LANES]
            up = pltpu.roll(t, LANES - ROPE_DIM // 2, axis=1)
            dn = pltpu.roll(t, ROPE_DIM // 2, axis=1)
            o_ref[:, cc * LANES:(cc + 1) * LANES] = (t * cos + up * sin_up + dn * sin_dn).astype(o_ref.dtype)


def _rope_tables(seq):
    half = ROPE_DIM // 2
    inv_freq = jnp.power(ROPE_THETA, -jnp.arange(half, dtype=F32) * 2.0 / ROPE_DIM)
    ang = jnp.arange(seq).astype(F32)[:, None] * inv_freq[None, :]
    cos, sin = jnp.cos(ang), jnp.sin(ang)
    ones = jnp.ones((seq, DIFF_QK_DIM - ROPE_DIM), F32)
    zeros = jnp.zeros((seq, DIFF_QK_DIM - ROPE_DIM), F32)
    zh = jnp.zeros((seq, half), F32)
    c64 = jnp.concatenate([cos, cos, ones], axis=1)
    up64 = jnp.concatenate([-sin, zh, zeros], axis=1)
    dn64 = jnp.concatenate([zh, sin, zeros], axis=1)
    rep = LANES // DIFF_QK_DIM
    return jnp.tile(c64, (1, rep)), jnp.tile(up64, (1, rep)), jnp.tile(dn64, (1, rep))


def _inproj(x, w_in, tm, tn, tk, rope_cols, vt_cols):
    seq, k = x.shape
    n = w_in.shape[1]
    assert vt_cols == (n - tn, n)
    cos, sin_up, sin_dn = _rope_tables(seq)
    tab = pl.BlockSpec((tm, LANES), lambda i, j: (i, 0))
    kern = functools.partial(_inproj_kernel, rope_lo=rope_cols[0] // tn, rope_hi=rope_cols[1] // tn,
                             vt_col=vt_cols[0] // tn)
    return pl.pallas_call(
        kern,
        grid=(seq // tm, n // tn),
        in_specs=[pl.BlockSpec((tm, k), lambda i, j: (i, 0)),
                  pl.BlockSpec((k, tn), lambda i, j: (0, j)),
                  tab, tab, tab],
        out_specs=[pl.BlockSpec((tm, tn), lambda i, j: (i, j)),
                   pl.BlockSpec((tm // tk, tn, tk), lambda i, j: (i, 0, 0))],
        out_shape=[jax.ShapeDtypeStruct((seq, n), BF16),
                   jax.ShapeDtypeStruct((seq // tk, tn, tk), BF16)],
        compiler_params=_params("parallel", "arbitrary"),
        name="inproj_rope",
    )(x, w_in, cos, sin_up, sin_dn)


def _sb_kernel(*refs, tq, scale, n_heads, head_stride):
    g_ref = refs[3 * n_heads]
    o_refs = refs[3 * n_heads + 1:]
    h = pl.program_id(0)
    i = pl.program_id(1)
    row = lax.broadcasted_iota(jnp.int32, (tq, tq), 0)
    col = lax.broadcasted_iota(jnp.int32, (tq, tq), 1)
    later = (row > col).astype(BF16)
    before = col < row

    def block_terms(q, k_ref, v_ref, kb, masked):
        start = pl.multiple_of(kb * tq, tq)
        k = k_ref[pl.ds(start, tq), :]
        z = _dot_nt(q, k) * scale
        sp = jnp.log(1.0 + jnp.exp(-jnp.abs(z)))
        log_beta = jnp.minimum(z, 0.0) - sp
        log_keep = -jnp.maximum(z, 0.0) - sp
        if masked:
            log_keep = jnp.where(before, log_keep, 0.0)
        hi = log_keep.astype(BF16)
        lo = (log_keep - hi.astype(F32)).astype(BF16)
        stick = _dot(hi, later) + _dot(lo, later)
        return log_beta + stick, jnp.sum(log_keep, axis=1, keepdims=True), v_ref[pl.ds(start, tq), :]

    def block_apply(terms, acc, c, masked, valid=None):
        log_w, keep_sum, v = terms
        w = jnp.exp(log_w + c)
        if masked:
            w = jnp.where(before, w, 0.0)
        if valid is not None:
            w = jnp.where(valid, w, 0.0)
            keep_sum = jnp.where(valid, keep_sum, 0.0)
        return acc + _dot(w.astype(BF16), v), c + keep_sum

    acc0 = jnp.zeros((tq, HEAD_DIM), F32)
    c0 = jnp.zeros((tq, 1), F32)
    heads = [(refs[3 * n][...], refs[3 * n + 1], refs[3 * n + 2]) for n in range(n_heads)]
    first = [(block_terms(q, k_ref, v_ref, i, True), block_terms(q, k_ref, v_ref, jnp.maximum(i - 1, 0), False))
             for q, k_ref, v_ref in heads]
    state = []
    for diag, prev in first:
        acc, c = block_apply(diag, acc0, c0, True)
        state.append(block_apply(prev, acc, c, False, valid=i >= 1))

    def live(carry):
        jj, _, _, cmax = carry
        return jnp.logical_and(jj < i, cmax > SB_DEAD_LOG)

    for n, ((q, k_ref, v_ref), (acc, c), o_ref) in enumerate(zip(heads, state, o_refs)):
        def older(carry, q=q, k_ref=k_ref, v_ref=v_ref):
            jj, acc, c, _ = carry
            acc, c = block_apply(block_terms(q, k_ref, v_ref, i - 1 - jj, False), acc, c, False)
            return jj + 1, acc, c, jnp.max(c)

        _, acc, c, _ = lax.while_loop(live, older, (jnp.int32(1), acc, c, jnp.max(c)))
        g = g_ref[pl.ds(h + n * head_stride, 1), :]
        ms = jnp.mean(acc * acc, axis=1, keepdims=True)
        o_ref[...] = (acc * lax.rsqrt(ms + RMS_EPS) * g).astype(o_ref.dtype)


def _sb_attention(proj, gain, tq, n_heads):
    seq = proj.shape[0]
    stride = SB_HEADS // n_heads
    kern = functools.partial(_sb_kernel, tq=tq, scale=1.0 / math.sqrt(HEAD_DIM), n_heads=n_heads,
                             head_stride=stride)
    specs = []
    for n in range(n_heads):
        off = n * stride
        specs += [pl.BlockSpec((tq, HEAD_DIM), lambda h, i, off=off: (i, off + h)),
                  pl.BlockSpec((seq, HEAD_DIM), lambda h, i, off=off: (0, SB_HEADS + off + h)),
                  pl.BlockSpec((seq, HEAD_DIM), lambda h, i, off=off: (0, 2 * SB_HEADS + off + h))]
    out = jax.ShapeDtypeStruct((seq, stride * HEAD_DIM), BF16)
    return pl.pallas_call(
        kern,
        grid=(stride, seq // tq),
        in_specs=specs + [pl.BlockSpec((SB_HEADS, HEAD_DIM), lambda h, i: (0, 0))],
        out_specs=[pl.BlockSpec((tq, HEAD_DIM), lambda h, i: (i, h))] * n_heads,
        out_shape=[out] * n_heads,
        compiler_params=_params("parallel", "arbitrary"),
        name="stickbreak_attn",
    )(*([proj] * (3 * n_heads)), gain)


def _diff_kernel(q_ref, k_ref, vt_ref, lam_ref, g_ref, o_ref, s0_ref, s1_ref, p0_ref, p1_ref, acc_ref,
                 *, tq, lambda_init):
    i = pl.program_id(1)
    s_refs, p_refs = (s0_ref, s1_ref), (p0_ref, p1_ref)
    q = q_ref[...] * jnp.asarray(1.0 / math.sqrt(DIFF_QK_DIM), BF16)
    lane = lax.broadcasted_iota(jnp.int32, (tq, HEAD_DIM), 1)
    zero = jnp.zeros_like(q)
    qmaps = (jnp.where(lane < DIFF_QK_DIM, q, zero),
             jnp.where(lane >= DIFF_QK_DIM, q, zero))
    ones = jnp.ones((SUM_ROWS, tq), BF16)

    def scores(kb, mp):
        k = k_ref[pl.ds(pl.multiple_of(kb * tq, tq), tq), :]
        return _dot_nt(k, qmaps[mp])

    def softmax_step(s, m, slot, mp):
        m_new = jnp.maximum(m, jnp.max(s, axis=0, keepdims=True))
        p_refs[slot][mp] = jnp.exp(s - m_new).astype(BF16)
        return m_new, jnp.exp(m - m_new)

    def accumulate(t, alpha, slot, mp):
        kb = jnp.where(t == 0, i, jnp.minimum(t, i) - 1)
        v_aug = jnp.concatenate([vt_ref[kb], ones], axis=0)
        v_aug = jnp.where(t <= i, v_aug, jnp.zeros_like(v_aug))
        acc_ref[mp] = alpha * acc_ref[mp] + _dot(v_aug, p_refs[slot][mp])

    key = lax.broadcasted_iota(jnp.int32, (tq, tq), 0)
    qry = lax.broadcasted_iota(jnp.int32, (tq, tq), 1)
    causal = key <= qry
    m0 = jnp.full((1, tq), NEG_INF, F32)
    acc_ref[...] = jnp.zeros_like(acc_ref)
    stats = []
    for mp in range(2):
        s_refs[1][mp] = scores(0, mp)
        stats.extend(softmax_step(jnp.where(causal, scores(i, mp), NEG_INF), m0, 0, mp))
    stats = tuple(stats)

    def step(t, carry, cur):
        nxt = 1 - cur
        out = []
        for mp in range(2):
            m, alpha = carry[2 * mp], carry[2 * mp + 1]
            s_refs[cur][mp] = scores(jnp.minimum(t + 1, i - 1), mp)
            accumulate(t, alpha, cur, mp)
            out.extend(softmax_step(s_refs[nxt][mp], m, nxt, mp))
        return tuple(out)

    def unrolled(u, carry):
        for j in range(DIFF_UNROLL):
            carry = step(DIFF_UNROLL * u + j, carry, j % 2)
        return carry

    n_iter = (i + DIFF_UNROLL - 1) // DIFF_UNROLL
    stats = lax.fori_loop(0, n_iter, unrolled, stats)
    for mp in range(2):
        accumulate(n_iter * DIFF_UNROLL, stats[2 * mp + 1], 0, mp)

    lf = lam_ref[...]
    lam = (jnp.exp(jnp.sum(lf[0:1, :] * lf[1:2, :], axis=1, keepdims=True))
           - jnp.exp(jnp.sum(lf[2:3, :] * lf[3:4, :], axis=1, keepdims=True)) + lambda_init)
    o1 = acc_ref[0, :HEAD_DIM, :] / acc_ref[0, HEAD_DIM:HEAD_DIM + 1, :]
    o2 = acc_ref[1, :HEAD_DIM, :] / acc_ref[1, HEAD_DIM:HEAD_DIM + 1, :]
    d = o1 - lam * o2
    ms = jnp.mean(d * d, axis=0, keepdims=True)
    dn = (d * lax.rsqrt(ms + RMS_EPS)).T
    o_ref[...] = (dn * g_ref[...] * (1.0 - lambda_init)).astype(o_ref.dtype)


def _diff_attention(proj, vt, df_lambda, gain, tq, lambda_init):
    seq = proj.shape[0]
    qcol = 3 * SB_HEADS
    kcol = qcol + DIFF_HEADS
    kern = functools.partial(_diff_kernel, tq=tq, lambda_init=lambda_init)
    return pl.pallas_call(
        kern,
        grid=(DIFF_HEADS, seq // tq),
        in_specs=[pl.BlockSpec((tq, HEAD_DIM), lambda h, i: (i, qcol + h)),
                  pl.BlockSpec((seq, HEAD_DIM), lambda h, i: (0, kcol + h)),
                  pl.BlockSpec((seq // tq, HEAD_DIM, tq), lambda h, i: (0, h, 0)),
                  pl.BlockSpec((4, DIFF_QK_DIM), lambda h, i: (0, 0)),
                  pl.BlockSpec((1, HEAD_DIM), lambda h, i: (0, 0))],
        out_specs=pl.BlockSpec((tq, HEAD_DIM), lambda h, i: (i, h)),
        out_shape=jax.ShapeDtypeStruct((seq, DIFF_HEADS * HEAD_DIM), BF16),
        scratch_shapes=[pltpu.VMEM((2, tq, tq), F32), pltpu.VMEM((2, tq, tq), F32),
                        pltpu.VMEM((2, tq, tq), BF16), pltpu.VMEM((2, tq, tq), BF16),
                        pltpu.VMEM((2, HEAD_DIM + SUM_ROWS, tq), F32)],
        compiler_params=_params("parallel", "arbitrary"),
        name="diff_attn",
    )(proj, proj, vt, df_lambda, gain)


def _layer_norm(r, g, b):
    mu = jnp.mean(r, axis=1, keepdims=True)
    d = r - mu
    var = jnp.mean(d * d, axis=1, keepdims=True)
    return d * lax.rsqrt(var + LN_EPS) * g + b


def _proj_ln_kernel(*refs, n_parts, emit_bf16):
    a_refs = refs[:n_parts]
    w_ref, x_ref, g_ref, b_ref, wn_ref, o_ref = refs[n_parts:n_parts + 6]
    rest = refs[n_parts + 6:]
    y = None
    off = 0
    for a_ref in a_refs:
        kk = a_ref.shape[1]
        part = _dot(a_ref[...], w_ref[off:off + kk, :])
        y = part if y is None else y + part
        off += kk
    out = _layer_norm(DEEPNORM_ALPHA * x_ref[...] + y, g_ref[...], b_ref[...])
    o_ref[...] = out
    out_bf16 = out.astype(BF16)
    if emit_bf16:
        rest[0][...] = out_bf16
    rest[-1][...] = _dot(out_bf16, wn_ref[...]).astype(BF16)


def _proj_ln(parts, w, x, gain, bias, w_next, tm, emit_bf16):
    seq, d = x.shape
    dn = w_next.shape[1]
    kern = functools.partial(_proj_ln_kernel, n_parts=len(parts), emit_bf16=emit_bf16)
    row = lambda i: (i, 0)
    fixed = lambda i: (0, 0)
    out_specs = [pl.BlockSpec((tm, d), row)]
    out_shape = [jax.ShapeDtypeStruct((seq, d), F32)]
    if emit_bf16:
        out_specs.append(pl.BlockSpec((tm, d), row))
        out_shape.append(jax.ShapeDtypeStruct((seq, d), BF16))
    out_specs.append(pl.BlockSpec((tm, dn), row))
    out_shape.append(jax.ShapeDtypeStruct((seq, dn), BF16))
    return pl.pallas_call(
        kern,
        grid=(seq // tm,),
        in_specs=[pl.BlockSpec((tm, p.shape[1]), row) for p in parts] + [
            pl.BlockSpec(w.shape, fixed),
            pl.BlockSpec((tm, d), row),
            pl.BlockSpec((1, d), fixed),
            pl.BlockSpec((1, d), fixed),
            pl.BlockSpec(w_next.shape, fixed)],
        out_specs=out_specs,
        out_shape=out_shape,
        compiler_params=_params("parallel"),
        name="proj_residual_ln",
    )(*parts, w, x, gain, bias, w_next)


def _xattn_kernel(q_ref, kv_ref, o_ref, *, d_model, head_dim):
    scale = 1.0 / math.sqrt(head_dim)
    for hh in range(d_model // head_dim):
        lo = hh * head_dim
        q = q_ref[:, lo:lo + head_dim]
        k = kv_ref[:, lo:lo + head_dim]
        v = kv_ref[:, d_model + lo:d_model + lo + head_dim]
        s = _dot_nt(q, k) * scale
        m = jnp.max(s, axis=1, keepdims=True)
        p = jnp.exp(s - m)
        p = p / jnp.sum(p, axis=1, keepdims=True)
        o_ref[:, lo:lo + head_dim] = _dot(p.astype(BF16), v).astype(o_ref.dtype)


def _xattn(q, kv, tm):
    seq, d = q.shape
    kern = functools.partial(_xattn_kernel, d_model=d, head_dim=d // MEM_HEADS)
    return pl.pallas_call(
        kern,
        grid=(seq // tm,),
        in_specs=[pl.BlockSpec((tm, d), lambda i: (i, 0)),
                  pl.BlockSpec(kv.shape, lambda i: (0, 0))],
        out_specs=pl.BlockSpec((tm, d), lambda i: (i, 0)),
        out_shape=jax.ShapeDtypeStruct((seq, d), BF16),
        compiler_params=_params("parallel"),
        name="memory_xattn",
    )(q, kv)


def _top16(s, vals_ref, exact_ties):
    tie_free = float(sum(range(PEER_TOPK)) + PEER_TOPK * (s.shape[0] - PEER_TOPK))
    assert tie_free != float((PEER_TOPK - 1) * s.shape[0])

    def step(k, carry):
        work, rank = carry
        m = jnp.max(work, axis=0, keepdims=True)
        sel = work == m
        if exact_ties:
            iota = lax.broadcasted_iota(jnp.int32, s.shape, 0).astype(F32)
            sel = iota == jnp.min(jnp.where(sel, iota, float(s.shape[0])), axis=0, keepdims=True)
        vals_ref[pl.ds(k, 1), :] = m
        return jnp.where(sel, NEG_INF, work), jnp.where(sel, lax.convert_element_type(k, F32), rank)

    _, rank = lax.fori_loop(0, PEER_TOPK, step, (s, jnp.full(s.shape, float(PEER_TOPK), F32)), unroll=True)
    return rank, tie_free - jnp.sum(rank, axis=0, keepdims=True)


CAND_KEEP = tuple(PEER_TOPK // (k1 + 1) for k1 in range(PEER_TOPK))
CAND_ROWS = -(-sum(CAND_KEEP) // 8) * 8


def _route_kernel(pq_ref, keys_ref, e2_ref, r2_ref, w1_ref, n1_ref, v1_ref, v2_ref, vc_ref, cand_ref):
    q = pq_ref[...]
    s1 = _dot_nt(keys_ref[0], q[:, :N_KEYS])
    s2 = _dot_nt(keys_ref[1], q[:, N_KEYS:])

    def route(exact_ties):
        rank1, deficit1 = _top16(s1, v1_ref, exact_ties)
        rank2, deficit2 = _top16(s2, v2_ref, exact_ties)
        cand_ref[...] = jnp.full(cand_ref.shape, NEG_INF, F32)
        off = 0
        for k1, keep in enumerate(CAND_KEEP):
            cand_ref[off:off + keep, :] = v1_ref[k1:k1 + 1, :] + v2_ref[0:keep, :]
            off += keep
        crank, deficit3 = _top16(cand_ref[...], vc_ref, exact_ties)
        top = vc_ref[...]
        z = jnp.sum(jnp.exp(top - top[0:1, :]), axis=0, keepdims=True)
        cand_ref[...] = jnp.where(crank < float(PEER_TOPK), 1.0, 0.0)
        rank1_packed = rank1.astype(BF16)
        n1 = jnp.zeros(s1.shape, BF16)
        off = 0
        for k1, keep in enumerate(CAND_KEEP):
            cnt = jnp.sum(cand_ref[off:off + keep, :], axis=0, keepdims=True)
            off += keep
            cnt = jnp.tile(jnp.broadcast_to(cnt, (BF16_SUBLANES, cnt.shape[1])).astype(BF16),
                           (N_KEYS // BF16_SUBLANES, 1))
            n1 = jnp.where(rank1_packed == jnp.asarray(k1, BF16), cnt, n1)
        e2 = jnp.exp(s2 - v2_ref[0:1, :]).astype(e2_ref.dtype)
        w1 = jnp.exp(s1 - v1_ref[0:1, :]) / z
        tied = jnp.max(jnp.maximum(jnp.maximum(jnp.abs(deficit1), jnp.abs(deficit2)), jnp.abs(deficit3))) > 0.0
        return (e2, rank2.astype(r2_ref.dtype), w1, n1.astype(F32)), tied

    tables, tied = route(False)
    tables = lax.cond(tied, lambda _: route(True)[0], lambda t: t, tables)
    e2_ref[0], r2_ref[0], w1_ref[0], n1_ref[0] = tables


def _peer_route(pq, keys, tt):
    seq = pq.shape[0]
    qd = 2 * N_KEYS
    tab = pl.BlockSpec((1, N_KEYS, tt), lambda i, h: (h, 0, i))
    shape = jax.ShapeDtypeStruct((PEER_HEADS, N_KEYS, seq), F32)
    packed = jax.ShapeDtypeStruct((PEER_HEADS, N_KEYS, seq), BF16)
    return pl.pallas_call(
        _route_kernel,
        grid=(seq // tt, PEER_HEADS),
        in_specs=[pl.BlockSpec((tt, qd), lambda i, h: (i, h)),
                  pl.BlockSpec(keys.shape, lambda i, h: (0, 0, 0))],
        out_specs=[tab, tab, tab, tab],
        out_shape=[packed, packed, shape, shape],
        scratch_shapes=[pltpu.VMEM((PEER_TOPK, tt), F32),
                        pltpu.VMEM((PEER_TOPK, tt), F32),
                        pltpu.VMEM((PEER_TOPK, tt), F32),
                        pltpu.VMEM((CAND_ROWS, tt), F32)],
        compiler_params=_params("parallel", "parallel"),
        name="peer_route",
    )(pq, keys)


def _peer_kernel(xb_ref, u_ref, v_ref, e2_ref, r2_ref, w1_ref, n1_ref, x_ref, g_ref, b_ref,
                 o_ref, a_ref, gate_ref, *, keys_per_step, n_chunks):
    step = pl.program_id(0)
    cur, prev = step % 2, (step + 1) % 2
    chunk = step % n_chunks
    chunk_prev = (step + n_chunks - 1) % n_chunks

    @pl.when(step == 0)
    def _():
        a_ref[...] = jnp.zeros_like(a_ref)
        gate_ref[...] = jnp.zeros_like(gate_ref)

    @pl.when(jnp.logical_or(chunk_prev == 0, step == 0))
    def _():
        o_ref[...] = jnp.zeros_like(o_ref)

    gated = (a_ref[prev].astype(F32) * gate_ref[prev]).astype(BF16)
    o_ref[...] += _dot(gated, v_ref[...])

    hidden = _dot(xb_ref[...], u_ref[...])
    a_ref[cur] = (0.5 * hidden * (1.0 + lax.erf(hidden * math.sqrt(0.5)))).astype(BF16)

    zero = jnp.zeros(r2_ref.shape[1:], BF16)

    def rows(row):
        tile = jnp.broadcast_to(row, (BF16_SUBLANES, row.shape[1])).astype(BF16)
        return jnp.tile(tile, (N_KEYS // BF16_SUBLANES, 1))

    for part in range(keys_per_step):
        c = chunk * keys_per_step + part
        gt = None
        for h in range(PEER_HEADS):
            n1 = rows(n1_ref[h, pl.ds(c, 1), :])
            w1 = rows(w1_ref[h, pl.ds(c, 1), :])
            term = jnp.where(r2_ref[h] < n1, e2_ref[h] * w1, zero)
            gt = term if gt is None else gt + term
        gate_ref[cur, :, part * N_KEYS:(part + 1) * N_KEYS] = gt.astype(F32).T

    @pl.when(jnp.logical_and(chunk_prev == n_chunks - 1, step > 0))
    def _():
        o_ref[...] = _layer_norm(DEEPNORM_ALPHA * x_ref[...] + o_ref[...], g_ref[...], b_ref[...])


def _peer_dense(xb, x, ut, v, tables, gain, bias, tt, keys_per_step):
    seq, d = x.shape
    te = keys_per_step * N_KEYS
    n_chunks = v.shape[0] // te
    n_tiles = seq // tt
    tile = lambda s: jnp.minimum(s // n_chunks, n_tiles - 1)
    tile_prev = lambda s: jnp.maximum(s - 1, 0) // n_chunks
    tab = pl.BlockSpec((PEER_HEADS, N_KEYS, tt), lambda s: (0, 0, tile(s)))
    fixed = lambda s: (0, 0)
    kern = functools.partial(_peer_kernel, keys_per_step=keys_per_step, n_chunks=n_chunks)
    return pl.pallas_call(
        kern,
        grid=(n_tiles * n_chunks + 1,),
        in_specs=[pl.BlockSpec((tt, d), lambda s: (tile(s), 0)),
                  pl.BlockSpec((d, te), lambda s: (0, s % n_chunks)),
                  pl.BlockSpec((te, d), lambda s: ((s + n_chunks - 1) % n_chunks, 0)),
                  tab, tab, tab, tab,
                  pl.BlockSpec((tt, d), lambda s: (tile_prev(s), 0)),
                  pl.BlockSpec((1, d), fixed),
                  pl.BlockSpec((1, d), fixed)],
        out_specs=pl.BlockSpec((tt, d), lambda s: (tile_prev(s), 0)),
        out_shape=jax.ShapeDtypeStruct((seq, d), F32),
        scratch_shapes=[pltpu.VMEM((2, tt, te), BF16), pltpu.VMEM((2, tt, te), F32)],
        compiler_params=_params("arbitrary"),
        name="peer_dense",
    )(xb, ut, v, *tables, x, gain, bias)


def _tile(n, want):
    return min(n, want)


def kernel(x, mem, w_in, sb_norm_gain, df_lambda, df_subln_gain, w_o, ln1_gain, ln1_bias, w_mq, w_mkv, w_mo, ln2_gain, ln2_bias, w_pq, peer_sub_keys, peer_u, peer_v, ln3_gain, ln3_bias):
    b, seq, d = x.shape
    assert b == 1 and w_in.shape[0] == DEPTH
    for rows in (ATTN_BLOCK, PROJ_ROWS, LN_ROWS, ROUTE_TOKENS, PEER_TOKENS):
        assert seq % _tile(seq, rows) == 0, (seq, rows)
    x2d = x.reshape(seq, d)
    mem2d = mem.reshape(mem.shape[1], d).astype(BF16)
    for l in range(DEPTH):
        lambda_init = 0.8 - 0.6 * math.exp(-0.3 * l)
        row1 = lambda a: a.reshape(1, -1)
        tq = _tile(seq, ATTN_BLOCK)
        width = SB_HEADS * HEAD_DIM
        proj, vt = _inproj(x2d, w_in[l].astype(BF16), _tile(seq, PROJ_ROWS), width, tq,
                           rope_cols=(3 * width, 5 * width), vt_cols=(5 * width, 6 * width))
        sb_parts = _sb_attention(proj, sb_norm_gain[l], tq, SB_HEADS_PER_STEP)
        df_o = _diff_attention(proj, vt, df_lambda[l], row1(df_subln_gain[l]), tq, lambda_init)
        x1, q = _proj_ln([*sb_parts, df_o], w_o[l].astype(BF16), x2d, row1(ln1_gain[l]), row1(ln1_bias[l]),
                         w_mq[l].astype(BF16), _tile(seq, LN_ROWS), emit_bf16=False)
        kv = _matmul(mem2d, w_mkv[l].astype(BF16), mem2d.shape[0], PROJ_COLS)
        xa = _xattn(q, kv, _tile(seq, LN_ROWS))
        x2, x2b, pq = _proj_ln([xa], w_mo[l].astype(BF16), x1, row1(ln2_gain[l]), row1(ln2_bias[l]),
                               w_pq[l].astype(BF16), _tile(seq, LN_ROWS), emit_bf16=True)
        tables = _peer_route(pq, peer_sub_keys[l].astype(BF16), _tile(seq, ROUTE_TOKENS))
        x2d = _peer_dense(x2b, x2, peer_u[l].T.astype(BF16), peer_v[l].astype(BF16), tables,
                          row1(ln3_gain[l]), row1(ln3_bias[l]), _tile(seq, PEER_TOKENS), PEER_KEYS_PER_STEP)
    return x2d.reshape(b, seq, d)
```

```python
import functools
import math

import jax
import jax.numpy as jnp
from jax import lax
from jax.experimental import pallas as pl
from jax.experimental.pallas import tpu as pltpu

F32 = jnp.float32
BF16 = jnp.bfloat16

LANES = 128
BF16_SUBLANES = 16
VMEM_LIMIT_BYTES = 56 * 1024 * 1024

ATTN_BLOCK = 256
PROJ_ROWS = 1024
PROJ_COLS = 1024
LN_ROWS = 256
ROUTE_TOKENS = 256
ROUTE_HEADS_PER_STEP = 2
PEER_TOKENS = 512
PEER_KEYS_PER_STEP = 4
SB_HEADS_PER_STEP = 4

DEPTH = 1
SB_HEADS = 8
DIFF_HEADS = 8
HEAD_DIM = 128
DIFF_QK_DIM = 64
ROPE_DIM = 16
ROPE_THETA = 500000.0
MEM_HEADS = 4
PEER_HEADS = 8
N_KEYS = 128
PEER_TOPK = 16
LN_EPS = 1e-5
RMS_EPS = 1e-6
DEEPNORM_ALPHA = (2 * DEPTH) ** 0.25
NEG_INF = float("-inf")
SB_DEAD_LOG = -105.0
SUM_ROWS = 16
DIFF_UNROLL = 4


def _dot(a, b):
    return jnp.dot(a, b, preferred_element_type=F32)


def _dot_nt(a, b):
    return lax.dot_general(a, b, (((1,), (1,)), ((), ())), preferred_element_type=F32)


def _params(*semantics):
    return pltpu.CompilerParams(dimension_semantics=semantics, vmem_limit_bytes=VMEM_LIMIT_BYTES)


def _matmul_kernel(a_ref, b_ref, o_ref):
    o_ref[...] = _dot(a_ref[...], b_ref[...]).astype(o_ref.dtype)


def _matmul(a, b, tm, tn, out_dtype=BF16):
    m, k = a.shape
    n = b.shape[1]
    return pl.pallas_call(
        _matmul_kernel,
        grid=(m // tm, n // tn),
        in_specs=[pl.BlockSpec((tm, k), lambda i, j: (i, 0)),
                  pl.BlockSpec((k, tn), lambda i, j: (0, j))],
        out_specs=pl.BlockSpec((tm, tn), lambda i, j: (i, j)),
        out_shape=jax.ShapeDtypeStruct((m, n), out_dtype),
        compiler_params=_params("parallel", "parallel"),
        name="matmul",
    )(a, b)


def _inproj_kernel(a_ref, b_ref, c_ref, s1_ref, s2_ref, o_ref, vt_ref, *, rope_lo, rope_hi, vt_col):
    j = pl.program_id(1)
    acc = _dot(a_ref[...].astype(BF16), b_ref[...])
    o_ref[...] = acc.astype(o_ref.dtype)

    @pl.when(j == vt_col)
    def _():
        tk = vt_ref.shape[2]
        for kb in range(vt_ref.shape[0]):
            vt_ref[kb] = acc[kb * tk:(kb + 1) * tk, :].T.astype(vt_ref.dtype)

    @pl.when(jnp.logical_and(j >= rope_lo, j < rope_hi))
    def _():
        cos, sin_up, sin_dn = c_ref[...], s1_ref[...], s2_ref[...]
        for cc in range(acc.shape[1] // LANES):
            t = acc[:, cc * LANES:(cc + 1) * LANES]
            up = pltpu.roll(t, LANES - ROPE_DIM // 2, axis=1)
            dn = pltpu.roll(t, ROPE_DIM // 2, axis=1)
            o_ref[:, cc * LANES:(cc + 1) * LANES] = (t * cos + up * sin_up + dn * sin_dn).astype(o_ref.dtype)


def _rope_tables(seq):
    half = ROPE_DIM // 2
    inv_freq = jnp.power(ROPE_THETA, -jnp.arange(half, dtype=F32) * 2.0 / ROPE_DIM)
    ang = jnp.arange(seq).astype(F32)[:, None] * inv_freq[None, :]
    cos, sin = jnp.cos(ang), jnp.sin(ang)
    ones = jnp.ones((seq, DIFF_QK_DIM - ROPE_DIM), F32)
    zeros = jnp.zeros((seq, DIFF_QK_DIM - ROPE_DIM), F32)
    zh = jnp.zeros((seq, half), F32)
    c64 = jnp.concatenate([cos, cos, ones], axis=1)
    up64 = jnp.concatenate([-sin, zh, zeros], axis=1)
    dn64 = jnp.concatenate([zh, sin, zeros], axis=1)
    rep = LANES // DIFF_QK_DIM
    return jnp.tile(c64, (1, rep)), jnp.tile(up64, (1, rep)), jnp.tile(dn64, (1, rep))


def _inproj(x, w_in, tm, tn, tk, rope_cols, vt_cols):
    seq, k = x.shape
    n = w_in.shape[1]
    assert vt_cols == (n - tn, n)
    cos, sin_up, sin_dn = _rope_tables(seq)
    tab = pl.BlockSpec((tm, LANES), lambda i, j: (i, 0))
    kern = functools.partial(_inproj_kernel, rope_lo=rope_cols[0] // tn, rope_hi=rope_cols[1] // tn,
                             vt_col=vt_cols[0] // tn)
    return pl.pallas_call(
        kern,
        grid=(seq // tm, n // tn),
        in_specs=[pl.BlockSpec((tm, k), lambda i, j: (i, 0)),
                  pl.BlockSpec((k, tn), lambda i, j: (0, j)),
                  tab, tab, tab],
        out_specs=[pl.BlockSpec((tm, tn), lambda i, j: (i, j)),
                   pl.BlockSpec((tm // tk, tn, tk), lambda i, j: (i, 0, 0))],
        out_shape=[jax.ShapeDtypeStruct((seq, n), BF16),
                   jax.ShapeDtypeStruct((seq // tk, tn, tk), BF16)],
        compiler_params=_params("parallel", "arbitrary"),
        name="inproj_rope",
    )(x, w_in, cos, sin_up, sin_dn)


def _sb_kernel(*refs, tq, scale, n_heads, head_stride):
    g_ref = refs[3 * n_heads]
    o_refs = refs[3 * n_heads + 1:]
    h = pl.program_id(0)
    i = pl.program_id(1)
    row = lax.broadcasted_iota(jnp.int32, (tq, tq), 0)
    col = lax.broadcasted_iota(jnp.int32, (tq, tq), 1)
    later = (row > col).astype(BF16)
    before = col < row

    def block_terms(q, k_ref, v_ref, kb, masked):
        start = pl.multiple_of(kb * tq, tq)
        k = k_ref[pl.ds(start, tq), :]
        z = _dot_nt(q, k) * scale
        sp = jnp.log(1.0 + jnp.exp(-jnp.abs(z)))
        log_beta = jnp.minimum(z, 0.0) - sp
        log_keep = -jnp.maximum(z, 0.0) - sp
        if masked:
            log_keep = jnp.where(before, log_keep, 0.0)
        hi = log_keep.astype(BF16)
        lo = (log_keep - hi.astype(F32)).astype(BF16)
        stick = _dot(hi, later) + _dot(lo, later)
        return log_beta + stick, jnp.sum(log_keep, axis=1, keepdims=True), v_ref[pl.ds(start, tq), :]

    def block_apply(terms, acc, c, masked, valid=None):
        log_w, keep_sum, v = terms
        w = jnp.exp(log_w + c)
        if masked:
            w = jnp.where(before, w, 0.0)
        if valid is not None:
            w = jnp.where(valid, w, 0.0)
            keep_sum = jnp.where(valid, keep_sum, 0.0)
        return acc + _dot(w.astype(BF16), v), c + keep_sum

    acc0 = jnp.zeros((tq, HEAD_DIM), F32)
    c0 = jnp.zeros((tq, 1), F32)
    heads = [(refs[3 * n][...], refs[3 * n + 1], refs[3 * n + 2]) for n in range(n_heads)]
    first = [(block_terms(q, k_ref, v_ref, i, True), block_terms(q, k_ref, v_ref, jnp.maximum(i - 1, 0), False))
             for q, k_ref, v_ref in heads]
    state = []
    for diag, prev in first:
        acc, c = block_apply(diag, acc0, c0, True)
        state.append(block_apply(prev, acc, c, False, valid=i >= 1))

    def live(carry):
        jj, _, _, cmax = carry
        return jnp.logical_and(jj < i, cmax > SB_DEAD_LOG)

    for n, ((q, k_ref, v_ref), (acc, c), o_ref) in enumerate(zip(heads, state, o_refs)):
        def older(carry, q=q, k_ref=k_ref, v_ref=v_ref):
            jj, acc, c, _ = carry
            acc, c = block_apply(block_terms(q, k_ref, v_ref, i - 1 - jj, False), acc, c, False)
            return jj + 1, acc, c, jnp.max(c)

        _, acc, c, _ = lax.while_loop(live, older, (jnp.int32(1), acc, c, jnp.max(c)))
        g = g_ref[pl.ds(h + n * head_stride, 1), :]
        ms = jnp.mean(acc * acc, axis=1, keepdims=True)
        o_ref[...] = (acc * lax.rsqrt(ms + RMS_EPS) * g).astype(o_ref.dtype)


def _sb_attention(proj, gain, tq, n_heads):
    seq = proj.shape[0]
    stride = SB_HEADS // n_heads
    kern = functools.partial(_sb_kernel, tq=tq, scale=1.0 / math.sqrt(HEAD_DIM), n_heads=n_heads,
                             head_stride=stride)
    specs = []
    for n in range(n_heads):
        off = n * stride
        specs += [pl.BlockSpec((tq, HEAD_DIM), lambda h, i, off=off: (i, off + h)),
                  pl.BlockSpec((seq, HEAD_DIM), lambda h, i, off=off: (0, SB_HEADS + off + h)),
                  pl.BlockSpec((seq, HEAD_DIM), lambda h, i, off=off: (0, 2 * SB_HEADS + off + h))]
    out = jax.ShapeDtypeStruct((seq, stride * HEAD_DIM), BF16)
    return pl.pallas_call(
        kern,
        grid=(stride, seq // tq),
        in_specs=specs + [pl.BlockSpec((SB_HEADS, HEAD_DIM), lambda h, i: (0, 0))],
        out_specs=[pl.BlockSpec((tq, HEAD_DIM), lambda h, i: (i, h))] * n_heads,
        out_shape=[out] * n_heads,
        compiler_params=_params("parallel", "arbitrary"),
        name="stickbreak_attn",
    )(*([proj] * (3 * n_heads)), gain)


def _diff_kernel(q_ref, k_ref, vt_ref, lam_ref, g_ref, o_ref, s0_ref, s1_ref, p0_ref, p1_ref, acc_ref,
                 *, tq, lambda_init):
    i = pl.program_id(1)
    s_refs, p_refs = (s0_ref, s1_ref), (p0_ref, p1_ref)
    q = q_ref[...] * jnp.asarray(1.0 / math.sqrt(DIFF_QK_DIM), BF16)
    lane = lax.broadcasted_iota(jnp.int32, (tq, HEAD_DIM), 1)
    zero = jnp.zeros_like(q)
    qmaps = (jnp.where(lane < DIFF_QK_DIM, q, zero),
             jnp.where(lane >= DIFF_QK_DIM, q, zero))
    ones = jnp.ones((SUM_ROWS, tq), BF16)

    def scores(kb, mp):
        k = k_ref[pl.ds(pl.multiple_of(kb * tq, tq), tq), :]
        return _dot_nt(k, qmaps[mp])

    def softmax_step(s, m, slot, mp):
        m_new = jnp.maximum(m, jnp.max(s, axis=0, keepdims=True))
        p_refs[slot][mp] = jnp.exp(s - m_new).astype(BF16)
        return m_new, jnp.exp(m - m_new)

    def accumulate(t, alpha, slot, mp):
        kb = jnp.where(t == 0, i, jnp.minimum(t, i) - 1)
        v_aug = jnp.concatenate([vt_ref[kb], ones], axis=0)
        v_aug = jnp.where(t <= i, v_aug, jnp.zeros_like(v_aug))
        acc_ref[mp] = alpha * acc_ref[mp] + _dot(v_aug, p_refs[slot][mp])

    key = lax.broadcasted_iota(jnp.int32, (tq, tq), 0)
    qry = lax.broadcasted_iota(jnp.int32, (tq, tq), 1)
    causal = key <= qry
    m0 = jnp.full((1, tq), NEG_INF, F32)
    acc_ref[...] = jnp.zeros_like(acc_ref)
    stats = []
    for mp in range(2):
        s_refs[1][mp] = scores(0, mp)
        stats.extend(softmax_step(jnp.where(causal, scores(i, mp), NEG_INF), m0, 0, mp))
    stats = tuple(stats)

    def step(t, carry, cur):
        nxt = 1 - cur
        out = []
        for mp in range(2):
            m, alpha = carry[2 * mp], carry[2 * mp + 1]
            s_refs[cur][mp] = scores(jnp.minimum(t + 1, i - 1), mp)
            accumulate(t, alpha, cur, mp)
            out.extend(softmax_step(s_refs[nxt][mp], m, nxt, mp))
        return tuple(out)

    def unrolled(u, carry):
        for j in range(DIFF_UNROLL):
            carry = step(DIFF_UNROLL * u + j, carry, j % 2)
        return carry

    n_iter = (i + DIFF_UNROLL - 1) // DIFF_UNROLL
    stats = lax.fori_loop(0, n_iter, unrolled, stats)
    for mp in range(2):
        accumulate(n_iter * DIFF_UNROLL, stats[2 * mp + 1], 0, mp)

    lf = lam_ref[...]
    lam = (jnp.exp(jnp.sum(lf[0:1, :] * lf[1:2, :], axis=1, keepdims=True))
           - jnp.exp(jnp.sum(lf[2:3, :] * lf[3:4, :], axis=1, keepdims=True)) + lambda_init)
    o1 = acc_ref[0, :HEAD_DIM, :] / acc_ref[0, HEAD_DIM:HEAD_DIM + 1, :]
    o2 = acc_ref[1, :HEAD_DIM, :] / acc_ref[1, HEAD_DIM:HEAD_DIM + 1, :]
    d = o1 - lam * o2
    ms = jnp.mean(d * d, axis=0, keepdims=True)
    dn = (d * lax.rsqrt(ms + RMS_EPS)).T
    o_ref[...] = (dn * g_ref[...] * (1.0 - lambda_init)).astype(o_ref.dtype)


def _diff_attention(proj, vt, df_lambda, gain, tq, lambda_init):
    seq = proj.shape[0]
    qcol = 3 * SB_HEADS
    kcol = qcol + DIFF_HEADS
    kern = functools.partial(_diff_kernel, tq=tq, lambda_init=lambda_init)
    return pl.pallas_call(
        kern,
        grid=(DIFF_HEADS, seq // tq),
        in_specs=[pl.BlockSpec((tq, HEAD_DIM), lambda h, i: (i, qcol + h)),
                  pl.BlockSpec((seq, HEAD_DIM), lambda h, i: (0, kcol + h)),
                  pl.BlockSpec((seq // tq, HEAD_DIM, tq), lambda h, i: (0, h, 0)),
                  pl.BlockSpec((4, DIFF_QK_DIM), lambda h, i: (0, 0)),
                  pl.BlockSpec((1, HEAD_DIM), lambda h, i: (0, 0))],
        out_specs=pl.BlockSpec((tq, HEAD_DIM), lambda h, i: (i, h)),
        out_shape=jax.ShapeDtypeStruct((seq, DIFF_HEADS * HEAD_DIM), BF16),
        scratch_shapes=[pltpu.VMEM((2, tq, tq), F32), pltpu.VMEM((2, tq, tq), F32),
                        pltpu.VMEM((2, tq, tq), BF16), pltpu.VMEM((2, tq, tq), BF16),
                        pltpu.VMEM((2, HEAD_DIM + SUM_ROWS, tq), F32)],
        compiler_params=_params("parallel", "arbitrary"),
        name="diff_attn",
    )(proj, proj, vt, df_lambda, gain)


def _layer_norm(r, g, b):
    mu = jnp.mean(r, axis=1, keepdims=True)
    d = r - mu
    var = jnp.mean(d * d, axis=1, keepdims=True)
    return d * lax.rsqrt(var + LN_EPS) * g + b


def _proj_ln_kernel(*refs, n_parts, emit_bf16):
    a_refs = refs[:n_parts]
    w_ref, x_ref, g_ref, b_ref, wn_ref, o_ref = refs[n_parts:n_parts + 6]
    rest = refs[n_parts + 6:]
    y = None
    off = 0
    for a_ref in a_refs:
        kk = a_ref.shape[1]
        part = _dot(a_ref[...], w_ref[off:off + kk, :])
        y = part if y is None else y + part
        off += kk
    out = _layer_norm(DEEPNORM_ALPHA * x_ref[...] + y, g_ref[...], b_ref[...])
    o_ref[...] = out
    out_bf16 = out.astype(BF16)
    if emit_bf16:
        rest[0][...] = out_bf16
    rest[-1][...] = _dot(out_bf16, wn_ref[...]).astype(BF16)


def _proj_ln(parts, w, x, gain, bias, w_next, tm, emit_bf16):
    seq, d = x.shape
    dn = w_next.shape[1]
    kern = functools.partial(_proj_ln_kernel, n_parts=len(parts), emit_bf16=emit_bf16)
    row = lambda i: (i, 0)
    fixed = lambda i: (0, 0)
    out_specs = [pl.BlockSpec((tm, d), row)]
    out_shape = [jax.ShapeDtypeStruct((seq, d), F32)]
    if emit_bf16:
        out_specs.append(pl.BlockSpec((tm, d), row))
        out_shape.append(jax.ShapeDtypeStruct((seq, d), BF16))
    out_specs.append(pl.BlockSpec((tm, dn), row))
    out_shape.append(jax.ShapeDtypeStruct((seq, dn), BF16))
    return pl.pallas_call(
        kern,
        grid=(seq // tm,),
        in_specs=[pl.BlockSpec((tm, p.shape[1]), row) for p in parts] + [
            pl.BlockSpec(w.shape, fixed),
            pl.BlockSpec((tm, d), row),
            pl.BlockSpec((1, d), fixed),
            pl.BlockSpec((1, d), fixed),
            pl.BlockSpec(w_next.shape, fixed)],
        out_specs=out_specs,
        out_shape=out_shape,
        compiler_params=_params("parallel"),
        name="proj_residual_ln",
    )(*parts, w, x, gain, bias, w_next)


def _xattn_kernel(q_ref, kv_ref, o_ref, *, d_model, head_dim):
    scale = 1.0 / math.sqrt(head_dim)
    for hh in range(d_model // head_dim):
        lo = hh * head_dim
        q = q_ref[:, lo:lo + head_dim]
        k = kv_ref[:, lo:lo + head_dim]
        v = kv_ref[:, d_model + lo:d_model + lo + head_dim]
        s = _dot_nt(q, k) * scale
        m = jnp.max(s, axis=1, keepdims=True)
        p = jnp.exp(s - m)
        p = p / jnp.sum(p, axis=1, keepdims=True)
        o_ref[:, lo:lo + head_dim] = _dot(p.astype(BF16), v).astype(o_ref.dtype)


def _xattn(q, kv, tm):
    seq, d = q.shape
    kern = functools.partial(_xattn_kernel, d_model=d, head_dim=d // MEM_HEADS)
    return pl.pallas_call(
        kern,
        grid=(seq // tm,),
        in_specs=[pl.BlockSpec((tm, d), lambda i: (i, 0)),
                  pl.BlockSpec(kv.shape, lambda i: (0, 0))],
        out_specs=pl.BlockSpec((tm, d), lambda i: (i, 0)),
        out_shape=jax.ShapeDtypeStruct((seq, d), BF16),
        compiler_params=_params("parallel"),
        name="memory_xattn",
    )(q, kv)


def _top16(s, vals_ref, exact_ties):
    tie_free = float(sum(range(PEER_TOPK)) + PEER_TOPK * (s.shape[0] - PEER_TOPK))
    assert tie_free != float((PEER_TOPK - 1) * s.shape[0])

    def step(k, carry):
        work, rank = carry
        m = jnp.max(work, axis=0, keepdims=True)
        sel = work == m
        if exact_ties:
            iota = lax.broadcasted_iota(jnp.int32, s.shape, 0).astype(F32)
            sel = iota == jnp.min(jnp.where(sel, iota, float(s.shape[0])), axis=0, keepdims=True)
        vals_ref[pl.ds(k, 1), :] = m
        return jnp.where(sel, NEG_INF, work), jnp.where(sel, lax.convert_element_type(k, F32), rank)

    _, rank = lax.fori_loop(0, PEER_TOPK, step, (s, jnp.full(s.shape, float(PEER_TOPK), F32)), unroll=True)
    return rank, tie_free - jnp.sum(rank, axis=0, keepdims=True)


CAND_KEEP = tuple(PEER_TOPK // (k1 + 1) for k1 in range(PEER_TOPK))
CAND_ROWS = -(-sum(CAND_KEEP) // 8) * 8


def _route_kernel(pq_ref, keys_ref, e2_ref, r2_ref, w1_ref, n1_ref, v1_ref, v2_ref, vc_ref, cand_ref):
    heads_per_step = e2_ref.shape[0]
    qd = 2 * N_KEYS

    def route_head(g, exact_ties):
        q = pq_ref[:, g * qd:(g + 1) * qd]
        s1 = _dot_nt(keys_ref[0], q[:, :N_KEYS])
        s2 = _dot_nt(keys_ref[1], q[:, N_KEYS:])
        v1, v2, vc, cand = v1_ref.at[g], v2_ref.at[g], vc_ref.at[g], cand_ref.at[g]
        rank1, deficit1 = _top16(s1, v1, exact_ties)
        rank2, deficit2 = _top16(s2, v2, exact_ties)
        cand[...] = jnp.full(cand.shape, NEG_INF, F32)
        off = 0
        for k1, keep in enumerate(CAND_KEEP):
            cand[off:off + keep, :] = v1[k1:k1 + 1, :] + v2[0:keep, :]
            off += keep
        crank, deficit3 = _top16(cand[...], vc, exact_ties)
        top = vc[...]
        z = jnp.sum(jnp.exp(top - top[0:1, :]), axis=0, keepdims=True)
        cand[...] = jnp.where(crank < float(PEER_TOPK), 1.0, 0.0)
        rank1_packed = rank1.astype(BF16)
        n1 = jnp.zeros(s1.shape, BF16)
        off = 0
        for k1, keep in enumerate(CAND_KEEP):
            cnt = jnp.sum(cand[off:off + keep, :], axis=0, keepdims=True)
            off += keep
            cnt = jnp.tile(jnp.broadcast_to(cnt, (BF16_SUBLANES, cnt.shape[1])).astype(BF16),
                           (N_KEYS // BF16_SUBLANES, 1))
            n1 = jnp.where(rank1_packed == jnp.asarray(k1, BF16), cnt, n1)
        e2 = jnp.exp(s2 - v2[0:1, :]).astype(e2_ref.dtype)
        w1 = jnp.exp(s1 - v1[0:1, :]) / z
        deficit = jnp.maximum(jnp.maximum(jnp.abs(deficit1), jnp.abs(deficit2)), jnp.abs(deficit3))
        return (e2, rank2.astype(r2_ref.dtype), w1, n1.astype(F32)), deficit

    def route(exact_ties):
        done = [route_head(g, exact_ties) for g in range(heads_per_step)]
        deficit = done[0][1]
        for _, other in done[1:]:
            deficit = jnp.maximum(deficit, other)
        return tuple(tables for tables, _ in done), jnp.max(deficit) > 0.0

    tables, tied = route(False)
    tables = lax.cond(tied, lambda _: route(True)[0], lambda t: t, tables)
    for g, (e2, r2, w1, n1) in enumerate(tables):
        e2_ref[g], r2_ref[g], w1_ref[g], n1_ref[g] = e2, r2, w1, n1


def _peer_route(pq, keys, tt, heads_per_step):
    seq = pq.shape[0]
    qd = 2 * N_KEYS
    tab = pl.BlockSpec((heads_per_step, N_KEYS, tt), lambda i, h: (h, 0, i))
    shape = jax.ShapeDtypeStruct((PEER_HEADS, N_KEYS, seq), F32)
    packed = jax.ShapeDtypeStruct((PEER_HEADS, N_KEYS, seq), BF16)
    return pl.pallas_call(
        _route_kernel,
        grid=(seq // tt, PEER_HEADS // heads_per_step),
        in_specs=[pl.BlockSpec((tt, heads_per_step * qd), lambda i, h: (i, h)),
                  pl.BlockSpec(keys.shape, lambda i, h: (0, 0, 0))],
        out_specs=[tab, tab, tab, tab],
        out_shape=[packed, packed, shape, shape],
        scratch_shapes=[pltpu.VMEM((heads_per_step, PEER_TOPK, tt), F32),
                        pltpu.VMEM((heads_per_step, PEER_TOPK, tt), F32),
                        pltpu.VMEM((heads_per_step, PEER_TOPK, tt), F32),
                        pltpu.VMEM((heads_per_step, CAND_ROWS, tt), F32)],
        compiler_params=_params("parallel", "parallel"),
        name="peer_route",
    )(pq, keys)


def _peer_kernel(xb_ref, u_ref, v_ref, e2_ref, r2_ref, w1_ref, n1_ref, x_ref, g_ref, b_ref,
                 o_ref, a_ref, gate_ref, *, keys_per_step, n_chunks):
    step = pl.program_id(0)
    cur, prev = step % 2, (step + 1) % 2
    chunk = step % n_chunks
    chunk_prev = (step + n_chunks - 1) % n_chunks

    @pl.when(step == 0)
    def _():
        a_ref[...] = jnp.zeros_like(a_ref)
        gate_ref[...] = jnp.zeros_like(gate_ref)

    @pl.when(jnp.logical_or(chunk_prev == 0, step == 0))
    def _():
        o_ref[...] = jnp.zeros_like(o_ref)

    gated = (a_ref[prev].astype(F32) * gate_ref[prev]).astype(BF16)
    o_ref[...] += _dot(gated, v_ref[...])

    hidden = _dot(xb_ref[...], u_ref[...])
    a_ref[cur] = (0.5 * hidden * (1.0 + lax.erf(hidden * math.sqrt(0.5)))).astype(BF16)

    zero = jnp.zeros(r2_ref.shape[1:], BF16)

    def rows(row):
        tile = jnp.broadcast_to(row, (BF16_SUBLANES, row.shape[1])).astype(BF16)
        return jnp.tile(tile, (N_KEYS // BF16_SUBLANES, 1))

    for part in range(keys_per_step):
        c = chunk * keys_per_step + part
        gt = None
        for h in range(PEER_HEADS):
            n1 = rows(n1_ref[h, pl.ds(c, 1), :])
            w1 = rows(w1_ref[h, pl.ds(c, 1), :])
            term = jnp.where(r2_ref[h] < n1, e2_ref[h] * w1, zero)
            gt = term if gt is None else gt + term
        gate_ref[cur, :, part * N_KEYS:(part + 1) * N_KEYS] = gt.astype(F32).T

    @pl.when(jnp.logical_and(chunk_prev == n_chunks - 1, step > 0))
    def _():
        o_ref[...] = _layer_norm(DEEPNORM_ALPHA * x_ref[...] + o_ref[...], g_ref[...], b_ref[...])


def _peer_dense(xb, x, ut, v, tables, gain, bias, tt, keys_per_step):
    seq, d = x.shape
    te = keys_per_step * N_KEYS
    n_chunks = v.shape[0] // te
    n_tiles = seq // tt
    tile = lambda s: jnp.minimum(s // n_chunks, n_tiles - 1)
    tile_prev = lambda s: jnp.maximum(s - 1, 0) // n_chunks
    tab = pl.BlockSpec((PEER_HEADS, N_KEYS, tt), lambda s: (0, 0, tile(s)))
    fixed = lambda s: (0, 0)
    kern = functools.partial(_peer_kernel, keys_per_step=keys_per_step, n_chunks=n_chunks)
    return pl.pallas_call(
        kern,
        grid=(n_tiles * n_chunks + 1,),
        in_specs=[pl.BlockSpec((tt, d), lambda s: (tile(s), 0)),
                  pl.BlockSpec((d, te), lambda s: (0, s % n_chunks)),
                  pl.BlockSpec((te, d), lambda s: ((s + n_chunks - 1) % n_chunks, 0)),
                  tab, tab, tab, tab,
                  pl.BlockSpec((tt, d), lambda s: (tile_prev(s), 0)),
                  pl.BlockSpec((1, d), fixed),
                  pl.BlockSpec((1, d), fixed)],
        out_specs=pl.BlockSpec((tt, d), lambda s: (tile_prev(s), 0)),
        out_shape=jax.ShapeDtypeStruct((seq, d), F32),
        scratch_shapes=[pltpu.VMEM((2, tt, te), BF16), pltpu.VMEM((2, tt, te), F32)],
        compiler_params=_params("arbitrary"),
        name="peer_dense",
    )(xb, ut, v, *tables, x, gain, bias)


def _tile(n, want):
    return min(n, want)


def kernel(x, mem, w_in, sb_norm_gain, df_lambda, df_subln_gain, w_o, ln1_gain, ln1_bias, w_mq, w_mkv, w_mo, ln2_gain, ln2_bias, w_pq, peer_sub_keys, peer_u, peer_v, ln3_gain, ln3_bias):
    b, seq, d = x.shape
    assert b == 1 and w_in.shape[0] == DEPTH
    for rows in (ATTN_BLOCK, PROJ_ROWS, LN_ROWS, ROUTE_TOKENS, PEER_TOKENS):
        assert seq % _tile(seq, rows) == 0, (seq, rows)
    x2d = x.reshape(seq, d)
    mem2d = mem.reshape(mem.shape[1], d).astype(BF16)
    for l in range(DEPTH):
        lambda_init = 0.8 - 0.6 * math.exp(-0.3 * l)
        row1 = lambda a: a.reshape(1, -1)
        tq = _tile(seq, ATTN_BLOCK)
        width = SB_HEADS * HEAD_DIM
        proj, vt = _inproj(x2d, w_in[l].astype(BF16), _tile(seq, PROJ_ROWS), width, tq,
                           rope_cols=(3 * width, 5 * width), vt_cols=(5 * width, 6 * width))
        sb_parts = _sb_attention(proj, sb_norm_gain[l], tq, SB_HEADS_PER_STEP)
        df_o = _diff_attention(proj, vt, df_lambda[l], row1(df_subln_gain[l]), tq, lambda_init)
        x1, q = _proj_ln([*sb_parts, df_o], w_o[l].astype(BF16), x2d, row1(ln1_gain[l]), row1(ln1_bias[l]),
                         w_mq[l].astype(BF16), _tile(seq, LN_ROWS), emit_bf16=False)
        kv = _matmul(mem2d, w_mkv[l].astype(BF16), mem2d.shape[0], PROJ_COLS)
        xa = _xattn(q, kv, _tile(seq, LN_ROWS))
        x2, x2b, pq = _proj_ln([xa], w_mo[l].astype(BF16), x1, row1(ln2_gain[l]), row1(ln2_bias[l]),
                               w_pq[l].astype(BF16), _tile(seq, LN_ROWS), emit_bf16=True)
        tables = _peer_route(pq, peer_sub_keys[l].astype(BF16), _tile(seq, ROUTE_TOKENS), ROUTE_HEADS_PER_STEP)
        x2d = _peer_dense(x2b, x2, peer_u[l].T.astype(BF16), peer_v[l].astype(BF16), tables,
                          row1(ln3_gain[l]), row1(ln3_bias[l]), _tile(seq, PEER_TOKENS), PEER_KEYS_PER_STEP)
    return x2d.reshape(b, seq, d)
```

```python
import functools
import math

import jax
import jax.numpy as jnp
from jax import lax
from jax.experimental import pallas as pl
from jax.experimental.pallas import tpu as pltpu

F32 = jnp.float32
BF16 = jnp.bfloat16

LANES = 128
BF16_SUBLANES = 16
VMEM_LIMIT_BYTES = 56 * 1024 * 1024

ATTN_BLOCK = 256
PROJ_ROWS = 1024
PROJ_COLS = 1024
LN_ROWS = 256
ROUTE_TOKENS = 256
ROUTE_HEADS_PER_STEP = 4
PEER_TOKENS = 512
PEER_KEYS_PER_STEP = 4
SB_HEADS_PER_STEP = 4

DEPTH = 1
SB_HEADS = 8
DIFF_HEADS = 8
HEAD_DIM = 128
DIFF_QK_DIM = 64
ROPE_DIM = 16
ROPE_THETA = 500000.0
MEM_HEADS = 4
PEER_HEADS = 8
N_KEYS = 128
PEER_TOPK = 16
LN_EPS = 1e-5
RMS_EPS = 1e-6
DEEPNORM_ALPHA = (2 * DEPTH) ** 0.25
NEG_INF = float("-inf")
SB_DEAD_LOG = -105.0
SUM_ROWS = 16
DIFF_UNROLL = 4


def _dot(a, b):
    return jnp.dot(a, b, preferred_element_type=F32)


def _dot_nt(a, b):
    return lax.dot_general(a, b, (((1,), (1,)), ((), ())), preferred_element_type=F32)


def _params(*semantics):
    return pltpu.CompilerParams(dimension_semantics=semantics, vmem_limit_bytes=VMEM_LIMIT_BYTES)


def _matmul_kernel(a_ref, b_ref, o_ref):
    o_ref[...] = _dot(a_ref[...], b_ref[...]).astype(o_ref.dtype)


def _matmul(a, b, tm, tn, out_dtype=BF16):
    m, k = a.shape
    n = b.shape[1]
    return pl.pallas_call(
        _matmul_kernel,
        grid=(m // tm, n // tn),
        in_specs=[pl.BlockSpec((tm, k), lambda i, j: (i, 0)),
                  pl.BlockSpec((k, tn), lambda i, j: (0, j))],
        out_specs=pl.BlockSpec((tm, tn), lambda i, j: (i, j)),
        out_shape=jax.ShapeDtypeStruct((m, n), out_dtype),
        compiler_params=_params("parallel", "parallel"),
        name="matmul",
    )(a, b)


def _inproj_kernel(a_ref, b_ref, c_ref, s1_ref, s2_ref, o_ref, vt_ref, *, rope_lo, rope_hi, vt_col):
    j = pl.program_id(1)
    acc = _dot(a_ref[...].astype(BF16), b_ref[...])
    o_ref[...] = acc.astype(o_ref.dtype)

    @pl.when(j == vt_col)
    def _():
        tk = vt_ref.shape[2]
        for kb in range(vt_ref.shape[0]):
            vt_ref[kb] = acc[kb * tk:(kb + 1) * tk, :].T.astype(vt_ref.dtype)

    @pl.when(jnp.logical_and(j >= rope_lo, j < rope_hi))
    def _():
        cos, sin_up, sin_dn = c_ref[...], s1_ref[...], s2_ref[...]
        for cc in range(acc.shape[1] // LANES):
            t = acc[:, cc * LANES:(cc + 1) * LANES]
            up = pltpu.roll(t, LANES - ROPE_DIM // 2, axis=1)
            dn = pltpu.roll(t, ROPE_DIM // 2, axis=1)
            o_ref[:, cc * LANES:(cc + 1) * LANES] = (t * cos + up * sin_up + dn * sin_dn).astype(o_ref.dtype)


def _rope_tables(seq):
    half = ROPE_DIM // 2
    inv_freq = jnp.power(ROPE_THETA, -jnp.arange(half, dtype=F32) * 2.0 / ROPE_DIM)
    ang = jnp.arange(seq).astype(F32)[:, None] * inv_freq[None, :]
    cos, sin = jnp.cos(ang), jnp.sin(ang)
    ones = jnp.ones((seq, DIFF_QK_DIM - ROPE_DIM), F32)
    zeros = jnp.zeros((seq, DIFF_QK_DIM - ROPE_DIM), F32)
    zh = jnp.zeros((seq, half), F32)
    c64 = jnp.concatenate([cos, cos, ones], axis=1)
    up64 = jnp.concatenate([-sin, zh, zeros], axis=1)
    dn64 = jnp.concatenate([zh, sin, zeros], axis=1)
    rep = LANES // DIFF_QK_DIM
    return jnp.tile(c64, (1, rep)), jnp.tile(up64, (1, rep)), jnp.tile(dn64, (1, rep))


def _inproj(x, w_in, tm, tn, tk, rope_cols, vt_cols):
    seq, k = x.shape
    n = w_in.shape[1]
    assert vt_cols == (n - tn, n)
    cos, sin_up, sin_dn = _rope_tables(seq)
    tab = pl.BlockSpec((tm, LANES), lambda i, j: (i, 0))
    kern = functools.partial(_inproj_kernel, rope_lo=rope_cols[0] // tn, rope_hi=rope_cols[1] // tn,
                             vt_col=vt_cols[0] // tn)
    return pl.pallas_call(
        kern,
        grid=(seq // tm, n // tn),
        in_specs=[pl.BlockSpec((tm, k), lambda i, j: (i, 0)),
                  pl.BlockSpec((k, tn), lambda i, j: (0, j)),
                  tab, tab, tab],
        out_specs=[pl.BlockSpec((tm, tn), lambda i, j: (i, j)),
                   pl.BlockSpec((tm // tk, tn, tk), lambda i, j: (i, 0, 0))],
        out_shape=[jax.ShapeDtypeStruct((seq, n), BF16),
                   jax.ShapeDtypeStruct((seq // tk, tn, tk), BF16)],
        compiler_params=_params("parallel", "arbitrary"),
        name="inproj_rope",
    )(x, w_in, cos, sin_up, sin_dn)


def _sb_kernel(*refs, tq, scale, n_heads, head_stride):
    g_ref = refs[3 * n_heads]
    o_refs = refs[3 * n_heads + 1:]
    h = pl.program_id(0)
    i = pl.program_id(1)
    row = lax.broadcasted_iota(jnp.int32, (tq, tq), 0)
    col = lax.broadcasted_iota(jnp.int32, (tq, tq), 1)
    later = (row > col).astype(BF16)
    before = col < row

    def block_terms(q, k_ref, v_ref, kb, masked):
        start = pl.multiple_of(kb * tq, tq)
        k = k_ref[pl.ds(start, tq), :]
        z = _dot_nt(q, k) * scale
        sp = jnp.log(1.0 + jnp.exp(-jnp.abs(z)))
        log_beta = jnp.minimum(z, 0.0) - sp
        log_keep = -jnp.maximum(z, 0.0) - sp
        if masked:
            log_keep = jnp.where(before, log_keep, 0.0)
        hi = log_keep.astype(BF16)
        lo = (log_keep - hi.astype(F32)).astype(BF16)
        stick = _dot(hi, later) + _dot(lo, later)
        return log_beta + stick, jnp.sum(log_keep, axis=1, keepdims=True), v_ref[pl.ds(start, tq), :]

    def block_apply(terms, acc, c, masked, valid=None):
        log_w, keep_sum, v = terms
        w = jnp.exp(log_w + c)
        if masked:
            w = jnp.where(before, w, 0.0)
        if valid is not None:
            w = jnp.where(valid, w, 0.0)
            keep_sum = jnp.where(valid, keep_sum, 0.0)
        return acc + _dot(w.astype(BF16), v), c + keep_sum

    acc0 = jnp.zeros((tq, HEAD_DIM), F32)
    c0 = jnp.zeros((tq, 1), F32)
    heads = [(refs[3 * n][...], refs[3 * n + 1], refs[3 * n + 2]) for n in range(n_heads)]
    first = [(block_terms(q, k_ref, v_ref, i, True), block_terms(q, k_ref, v_ref, jnp.maximum(i - 1, 0), False))
             for q, k_ref, v_ref in heads]
    state = []
    for diag, prev in first:
        acc, c = block_apply(diag, acc0, c0, True)
        state.append(block_apply(prev, acc, c, False, valid=i >= 1))

    def live(carry):
        jj, _, _, cmax = carry
        return jnp.logical_and(jj < i, cmax > SB_DEAD_LOG)

    for n, ((q, k_ref, v_ref), (acc, c), o_ref) in enumerate(zip(heads, state, o_refs)):
        def older(carry, q=q, k_ref=k_ref, v_ref=v_ref):
            jj, acc, c, _ = carry
            acc, c = block_apply(block_terms(q, k_ref, v_ref, i - 1 - jj, False), acc, c, False)
            return jj + 1, acc, c, jnp.max(c)

        _, acc, c, _ = lax.while_loop(live, older, (jnp.int32(1), acc, c, jnp.max(c)))
        g = g_ref[pl.ds(h + n * head_stride, 1), :]
        ms = jnp.mean(acc * acc, axis=1, keepdims=True)
        o_ref[...] = (acc * lax.rsqrt(ms + RMS_EPS) * g).astype(o_ref.dtype)


def _sb_attention(proj, gain, tq, n_heads):
    seq = proj.shape[0]
    stride = SB_HEADS // n_heads
    kern = functools.partial(_sb_kernel, tq=tq, scale=1.0 / math.sqrt(HEAD_DIM), n_heads=n_heads,
                             head_stride=stride)
    specs = []
    for n in range(n_heads):
        off = n * stride
        specs += [pl.BlockSpec((tq, HEAD_DIM), lambda h, i, off=off: (i, off + h)),
                  pl.BlockSpec((seq, HEAD_DIM), lambda h, i, off=off: (0, SB_HEADS + off + h)),
                  pl.BlockSpec((seq, HEAD_DIM), lambda h, i, off=off: (0, 2 * SB_HEADS + off + h))]
    out = jax.ShapeDtypeStruct((seq, stride * HEAD_DIM), BF16)
    return pl.pallas_call(
        kern,
        grid=(stride, seq // tq),
        in_specs=specs + [pl.BlockSpec((SB_HEADS, HEAD_DIM), lambda h, i: (0, 0))],
        out_specs=[pl.BlockSpec((tq, HEAD_DIM), lambda h, i: (i, h))] * n_heads,
        out_shape=[out] * n_heads,
        compiler_params=_params("parallel", "arbitrary"),
        name="stickbreak_attn",
    )(*([proj] * (3 * n_heads)), gain)


def _diff_kernel(q_ref, k_ref, vt_ref, lam_ref, g_ref, o_ref, s0_ref, s1_ref, p0_ref, p1_ref, acc_ref,
                 *, tq, lambda_init):
    i = pl.program_id(1)
    s_refs, p_refs = (s0_ref, s1_ref), (p0_ref, p1_ref)
    q = q_ref[...] * jnp.asarray(1.0 / math.sqrt(DIFF_QK_DIM), BF16)
    lane = lax.broadcasted_iota(jnp.int32, (tq, HEAD_DIM), 1)
    zero = jnp.zeros_like(q)
    qmaps = (jnp.where(lane < DIFF_QK_DIM, q, zero),
             jnp.where(lane >= DIFF_QK_DIM, q, zero))
    ones = jnp.ones((SUM_ROWS, tq), BF16)

    def scores(kb, mp):
        k = k_ref[pl.ds(pl.multiple_of(kb * tq, tq), tq), :]
        return _dot_nt(k, qmaps[mp])

    def softmax_step(s, m, slot, mp):
        m_new = jnp.maximum(m, jnp.max(s, axis=0, keepdims=True))
        p_refs[slot][mp] = jnp.exp(s - m_new).astype(BF16)
        return m_new, jnp.exp(m - m_new)

    def accumulate(t, alpha, slot, mp):
        kb = jnp.where(t == 0, i, jnp.minimum(t, i) - 1)
        v_aug = jnp.concatenate([vt_ref[kb], ones], axis=0)
        v_aug = jnp.where(t <= i, v_aug, jnp.zeros_like(v_aug))
        acc_ref[mp] = alpha * acc_ref[mp] + _dot(v_aug, p_refs[slot][mp])

    key = lax.broadcasted_iota(jnp.int32, (tq, tq), 0)
    qry = lax.broadcasted_iota(jnp.int32, (tq, tq), 1)
    causal = key <= qry
    m0 = jnp.full((1, tq), NEG_INF, F32)
    acc_ref[...] = jnp.zeros_like(acc_ref)
    stats = []
    for mp in range(2):
        s_refs[1][mp] = scores(0, mp)
        stats.extend(softmax_step(jnp.where(causal, scores(i, mp), NEG_INF), m0, 0, mp))
    stats = tuple(stats)

    def step(t, carry, cur):
        nxt = 1 - cur
        out = []
        for mp in range(2):
            m, alpha = carry[2 * mp], carry[2 * mp + 1]
            s_refs[cur][mp] = scores(jnp.minimum(t + 1, i - 1), mp)
            accumulate(t, alpha, cur, mp)
            out.extend(softmax_step(s_refs[nxt][mp], m, nxt, mp))
        return tuple(out)

    def unrolled(u, carry):
        for j in range(DIFF_UNROLL):
            carry = step(DIFF_UNROLL * u + j, carry, j % 2)
        return carry

    n_iter = (i + DIFF_UNROLL - 1) // DIFF_UNROLL
    stats = lax.fori_loop(0, n_iter, unrolled, stats)
    for mp in range(2):
        accumulate(n_iter * DIFF_UNROLL, stats[2 * mp + 1], 0, mp)

    lf = lam_ref[...]
    lam = (jnp.exp(jnp.sum(lf[0:1, :] * lf[1:2, :], axis=1, keepdims=True))
           - jnp.exp(jnp.sum(lf[2:3, :] * lf[3:4, :], axis=1, keepdims=True)) + lambda_init)
    o1 = acc_ref[0, :HEAD_DIM, :] / acc_ref[0, HEAD_DIM:HEAD_DIM + 1, :]
    o2 = acc_ref[1, :HEAD_DIM, :] / acc_ref[1, HEAD_DIM:HEAD_DIM + 1, :]
    d = o1 - lam * o2
    ms = jnp.mean(d * d, axis=0, keepdims=True)
    dn = (d * lax.rsqrt(ms + RMS_EPS)).T
    o_ref[...] = (dn * g_ref[...] * (1.0 - lambda_init)).astype(o_ref.dtype)


def _diff_attention(proj, vt, df_lambda, gain, tq, lambda_init):
    seq = proj.shape[0]
    qcol = 3 * SB_HEADS
    kcol = qcol + DIFF_HEADS
    kern = functools.partial(_diff_kernel, tq=tq, lambda_init=lambda_init)
    return pl.pallas_call(
        kern,
        grid=(DIFF_HEADS, seq // tq),
        in_specs=[pl.BlockSpec((tq, HEAD_DIM), lambda h, i: (i, qcol + h)),
                  pl.BlockSpec((seq, HEAD_DIM), lambda h, i: (0, kcol + h)),
                  pl.BlockSpec((seq // tq, HEAD_DIM, tq), lambda h, i: (0, h, 0)),
                  pl.BlockSpec((4, DIFF_QK_DIM), lambda h, i: (0, 0)),
                  pl.BlockSpec((1, HEAD_DIM), lambda h, i: (0, 0))],
        out_specs=pl.BlockSpec((tq, HEAD_DIM), lambda h, i: (i, h)),
        out_shape=jax.ShapeDtypeStruct((seq, DIFF_HEADS * HEAD_DIM), BF16),
        scratch_shapes=[pltpu.VMEM((2, tq, tq), F32), pltpu.VMEM((2, tq, tq), F32),
                        pltpu.VMEM((2, tq, tq), BF16), pltpu.VMEM((2, tq, tq), BF16),
                        pltpu.VMEM((2, HEAD_DIM + SUM_ROWS, tq), F32)],
        compiler_params=_params("parallel", "arbitrary"),
        name="diff_attn",
    )(proj, proj, vt, df_lambda, gain)


def _layer_norm(r, g, b):
    mu = jnp.mean(r, axis=1, keepdims=True)
    d = r - mu
    var = jnp.mean(d * d, axis=1, keepdims=True)
    return d * lax.rsqrt(var + LN_EPS) * g + b


def _proj_ln_kernel(*refs, n_parts, emit_bf16):
    a_refs = refs[:n_parts]
    w_ref, x_ref, g_ref, b_ref, wn_ref, o_ref = refs[n_parts:n_parts + 6]
    rest = refs[n_parts + 6:]
    y = None
    off = 0
    for a_ref in a_refs:
        kk = a_ref.shape[1]
        part = _dot(a_ref[...], w_ref[off:off + kk, :])
        y = part if y is None else y + part
        off += kk
    out = _layer_norm(DEEPNORM_ALPHA * x_ref[...] + y, g_ref[...], b_ref[...])
    o_ref[...] = out
    out_bf16 = out.astype(BF16)
    if emit_bf16:
        rest[0][...] = out_bf16
    rest[-1][...] = _dot(out_bf16, wn_ref[...]).astype(BF16)


def _proj_ln(parts, w, x, gain, bias, w_next, tm, emit_bf16):
    seq, d = x.shape
    dn = w_next.shape[1]
    kern = functools.partial(_proj_ln_kernel, n_parts=len(parts), emit_bf16=emit_bf16)
    row = lambda i: (i, 0)
    fixed = lambda i: (0, 0)
    out_specs = [pl.BlockSpec((tm, d), row)]
    out_shape = [jax.ShapeDtypeStruct((seq, d), F32)]
    if emit_bf16:
        out_specs.append(pl.BlockSpec((tm, d), row))
        out_shape.append(jax.ShapeDtypeStruct((seq, d), BF16))
    out_specs.append(pl.BlockSpec((tm, dn), row))
    out_shape.append(jax.ShapeDtypeStruct((seq, dn), BF16))
    return pl.pallas_call(
        kern,
        grid=(seq // tm,),
        in_specs=[pl.BlockSpec((tm, p.shape[1]), row) for p in parts] + [
            pl.BlockSpec(w.shape, fixed),
            pl.BlockSpec((tm, d), row),
            pl.BlockSpec((1, d), fixed),
            pl.BlockSpec((1, d), fixed),
            pl.BlockSpec(w_next.shape, fixed)],
        out_specs=out_specs,
        out_shape=out_shape,
        compiler_params=_params("parallel"),
        name="proj_residual_ln",
    )(*parts, w, x, gain, bias, w_next)


def _xattn_kernel(q_ref, kv_ref, o_ref, *, d_model, head_dim):
    scale = 1.0 / math.sqrt(head_dim)
    for hh in range(d_model // head_dim):
        lo = hh * head_dim
        q = q_ref[:, lo:lo + head_dim]
        k = kv_ref[:, lo:lo + head_dim]
        v = kv_ref[:, d_model + lo:d_model + lo + head_dim]
        s = _dot_nt(q, k) * scale
        m = jnp.max(s, axis=1, keepdims=True)
        p = jnp.exp(s - m)
        p = p / jnp.sum(p, axis=1, keepdims=True)
        o_ref[:, lo:lo + head_dim] = _dot(p.astype(BF16), v).astype(o_ref.dtype)


def _xattn(q, kv, tm):
    seq, d = q.shape
    kern = functools.partial(_xattn_kernel, d_model=d, head_dim=d // MEM_HEADS)
    return pl.pallas_call(
        kern,
        grid=(seq // tm,),
        in_specs=[pl.BlockSpec((tm, d), lambda i: (i, 0)),
                  pl.BlockSpec(kv.shape, lambda i: (0, 0))],
        out_specs=pl.BlockSpec((tm, d), lambda i: (i, 0)),
        out_shape=jax.ShapeDtypeStruct((seq, d), BF16),
        compiler_params=_params("parallel"),
        name="memory_xattn",
    )(q, kv)


def _top16(s, vals_ref, exact_ties):
    tie_free = float(sum(range(PEER_TOPK)) + PEER_TOPK * (s.shape[0] - PEER_TOPK))
    assert tie_free != float((PEER_TOPK - 1) * s.shape[0])

    def step(k, carry):
        work, rank = carry
        m = jnp.max(work, axis=0, keepdims=True)
        sel = work == m
        if exact_ties:
            iota = lax.broadcasted_iota(jnp.int32, s.shape, 0).astype(F32)
            sel = iota == jnp.min(jnp.where(sel, iota, float(s.shape[0])), axis=0, keepdims=True)
        vals_ref[pl.ds(k, 1), :] = m
        return jnp.where(sel, NEG_INF, work), jnp.where(sel, lax.convert_element_type(k, F32), rank)

    _, rank = lax.fori_loop(0, PEER_TOPK, step, (s, jnp.full(s.shape, float(PEER_TOPK), F32)), unroll=True)
    return rank, tie_free - jnp.sum(rank, axis=0, keepdims=True)


CAND_KEEP = tuple(PEER_TOPK // (k1 + 1) for k1 in range(PEER_TOPK))
CAND_ROWS = -(-sum(CAND_KEEP) // 8) * 8


def _route_kernel(pq_ref, keys_ref, e2_ref, r2_ref, w1_ref, n1_ref, v1_ref, v2_ref, vc_ref, cand_ref):
    heads_per_step = e2_ref.shape[0]
    qd = 2 * N_KEYS

    def route_head(g, exact_ties):
        q = pq_ref[:, g * qd:(g + 1) * qd]
        s1 = _dot_nt(keys_ref[0], q[:, :N_KEYS])
        s2 = _dot_nt(keys_ref[1], q[:, N_KEYS:])
        v1, v2, vc, cand = v1_ref.at[g], v2_ref.at[g], vc_ref.at[g], cand_ref.at[g]
        rank1, deficit1 = _top16(s1, v1, exact_ties)
        rank2, deficit2 = _top16(s2, v2, exact_ties)
        cand[...] = jnp.full(cand.shape, NEG_INF, F32)
        off = 0
        for k1, keep in enumerate(CAND_KEEP):
            cand[off:off + keep, :] = v1[k1:k1 + 1, :] + v2[0:keep, :]
            off += keep
        crank, deficit3 = _top16(cand[...], vc, exact_ties)
        top = vc[...]
        z = jnp.sum(jnp.exp(top - top[0:1, :]), axis=0, keepdims=True)
        cand[...] = jnp.where(crank < float(PEER_TOPK), 1.0, 0.0)
        rank1_packed = rank1.astype(BF16)
        n1 = jnp.zeros(s1.shape, BF16)
        off = 0
        for k1, keep in enumerate(CAND_KEEP):
            cnt = jnp.sum(cand[off:off + keep, :], axis=0, keepdims=True)
            off += keep
            cnt = jnp.tile(jnp.broadcast_to(cnt, (BF16_SUBLANES, cnt.shape[1])).astype(BF16),
                           (N_KEYS // BF16_SUBLANES, 1))
            n1 = jnp.where(rank1_packed == jnp.asarray(k1, BF16), cnt, n1)
        e2 = jnp.exp(s2 - v2[0:1, :]).astype(e2_ref.dtype)
        w1 = jnp.exp(s1 - v1[0:1, :]) / z
        deficit = jnp.maximum(jnp.maximum(jnp.abs(deficit1), jnp.abs(deficit2)), jnp.abs(deficit3))
        return (e2, rank2.astype(r2_ref.dtype), w1, n1.astype(F32)), deficit

    def route(exact_ties):
        done = [route_head(g, exact_ties) for g in range(heads_per_step)]
        deficit = done[0][1]
        for _, other in done[1:]:
            deficit = jnp.maximum(deficit, other)
        return tuple(tables for tables, _ in done), jnp.max(deficit) > 0.0

    tables, tied = route(False)
    tables = lax.cond(tied, lambda _: route(True)[0], lambda t: t, tables)
    for g, (e2, r2, w1, n1) in enumerate(tables):
        e2_ref[g], r2_ref[g], w1_ref[g], n1_ref[g] = e2, r2, w1, n1


def _peer_route(pq, keys, tt, heads_per_step):
    seq = pq.shape[0]
    qd = 2 * N_KEYS
    tab = pl.BlockSpec((heads_per_step, N_KEYS, tt), lambda i, h: (h, 0, i))
    shape = jax.ShapeDtypeStruct((PEER_HEADS, N_KEYS, seq), F32)
    packed = jax.ShapeDtypeStruct((PEER_HEADS, N_KEYS, seq), BF16)
    return pl.pallas_call(
        _route_kernel,
        grid=(seq // tt, PEER_HEADS // heads_per_step),
        in_specs=[pl.BlockSpec((tt, heads_per_step * qd), lambda i, h: (i, h)),
                  pl.BlockSpec(keys.shape, lambda i, h: (0, 0, 0))],
        out_specs=[tab, tab, tab, tab],
        out_shape=[packed, packed, shape, shape],
        scratch_shapes=[pltpu.VMEM((heads_per_step, PEER_TOPK, tt), F32),
                        pltpu.VMEM((heads_per_step, PEER_TOPK, tt), F32),
                        pltpu.VMEM((heads_per_step, PEER_TOPK, tt), F32),
                        pltpu.VMEM((heads_per_step, CAND_ROWS, tt), F32)],
        compiler_params=_params("parallel", "parallel"),
        name="peer_route",
    )(pq, keys)


def _peer_kernel(xb_ref, u_ref, v_ref, e2_ref, r2_ref, w1_ref, n1_ref, x_ref, g_ref, b_ref,
                 o_ref, a_ref, gate_ref, *, keys_per_step, n_chunks):
    step = pl.program_id(0)
    cur, prev = step % 2, (step + 1) % 2
    chunk = step % n_chunks
    chunk_prev = (step + n_chunks - 1) % n_chunks

    @pl.when(step == 0)
    def _():
        a_ref[...] = jnp.zeros_like(a_ref)
        gate_ref[...] = jnp.zeros_like(gate_ref)

    @pl.when(jnp.logical_or(chunk_prev == 0, step == 0))
    def _():
        o_ref[...] = jnp.zeros_like(o_ref)

    gated = (a_ref[prev].astype(F32) * gate_ref[prev]).astype(BF16)
    o_ref[...] += _dot(gated, v_ref[...])

    hidden = _dot(xb_ref[...], u_ref[...])
    a_ref[cur] = (0.5 * hidden * (1.0 + lax.erf(hidden * math.sqrt(0.5)))).astype(BF16)

    zero = jnp.zeros(r2_ref.shape[1:], BF16)

    def rows(row):
        tile = jnp.broadcast_to(row, (BF16_SUBLANES, row.shape[1])).astype(BF16)
        return jnp.tile(tile, (N_KEYS // BF16_SUBLANES, 1))

    for part in range(keys_per_step):
        c = chunk * keys_per_step + part
        gt = None
        for h in range(PEER_HEADS):
            n1 = rows(n1_ref[h, pl.ds(c, 1), :])
            w1 = rows(w1_ref[h, pl.ds(c, 1), :])
            term = jnp.where(r2_ref[h] < n1, e2_ref[h] * w1, zero)
            gt = term if gt is None else gt + term
        gate_ref[cur, :, part * N_KEYS:(part + 1) * N_KEYS] = gt.astype(F32).T

    @pl.when(jnp.logical_and(chunk_prev == n_chunks - 1, step > 0))
    def _():
        o_ref[...] = _layer_norm(DEEPNORM_ALPHA * x_ref[...] + o_ref[...], g_ref[...], b_ref[...])


def _peer_dense(xb, x, ut, v, tables, gain, bias, tt, keys_per_step):
    seq, d = x.shape
    te = keys_per_step * N_KEYS
    n_chunks = v.shape[0] // te
    n_tiles = seq // tt
    tile = lambda s: jnp.minimum(s // n_chunks, n_tiles - 1)
    tile_prev = lambda s: jnp.maximum(s - 1, 0) // n_chunks
    tab = pl.BlockSpec((PEER_HEADS, N_KEYS, tt), lambda s: (0, 0, tile(s)))
    fixed = lambda s: (0, 0)
    kern = functools.partial(_peer_kernel, keys_per_step=keys_per_step, n_chunks=n_chunks)
    return pl.pallas_call(
        kern,
        grid=(n_tiles * n_chunks + 1,),
        in_specs=[pl.BlockSpec((tt, d), lambda s: (tile(s), 0)),
                  pl.BlockSpec((d, te), lambda s: (0, s % n_chunks)),
                  pl.BlockSpec((te, d), lambda s: ((s + n_chunks - 1) % n_chunks, 0)),
                  tab, tab, tab, tab,
                  pl.BlockSpec((tt, d), lambda s: (tile_prev(s), 0)),
                  pl.BlockSpec((1, d), fixed),
                  pl.BlockSpec((1, d), fixed)],
        out_specs=pl.BlockSpec((tt, d), lambda s: (tile_prev(s), 0)),
        out_shape=jax.ShapeDtypeStruct((seq, d), F32),
        scratch_shapes=[pltpu.VMEM((2, tt, te), BF16), pltpu.VMEM((2, tt, te), F32)],
        compiler_params=_params("arbitrary"),
        name="peer_dense",
    )(xb, ut, v, *tables, x, gain, bias)


def _tile(n, want):
    return min(n, want)


def kernel(x, mem, w_in, sb_norm_gain, df_lambda, df_subln_gain, w_o, ln1_gain, ln1_bias, w_mq, w_mkv, w_mo, ln2_gain, ln2_bias, w_pq, peer_sub_keys, peer_u, peer_v, ln3_gain, ln3_bias):
    b, seq, d = x.shape
    assert b == 1 and w_in.shape[0] == DEPTH
    for rows in (ATTN_BLOCK, PROJ_ROWS, LN_ROWS, ROUTE_TOKENS, PEER_TOKENS):
        assert seq % _tile(seq, rows) == 0, (seq, rows)
    x2d = x.reshape(seq, d)
    mem2d = mem.reshape(mem.shape[1], d).astype(BF16)
    for l in range(DEPTH):
        lambda_init = 0.8 - 0.6 * math.exp(-0.3 * l)
        row1 = lambda a: a.reshape(1, -1)
        tq = _tile(seq, ATTN_BLOCK)
        width = SB_HEADS * HEAD_DIM
        proj, vt = _inproj(x2d, w_in[l].astype(BF16), _tile(seq, PROJ_ROWS), width, tq,
                           rope_cols=(3 * width, 5 * width), vt_cols=(5 * width, 6 * width))
        sb_parts = _sb_attention(proj, sb_norm_gain[l], tq, SB_HEADS_PER_STEP)
        df_o = _diff_attention(proj, vt, df_lambda[l], row1(df_subln_gain[l]), tq, lambda_init)
        x1, q = _proj_ln([*sb_parts, df_o], w_o[l].astype(BF16), x2d, row1(ln1_gain[l]), row1(ln1_bias[l]),
                         w_mq[l].astype(BF16), _tile(seq, LN_ROWS), emit_bf16=False)
        kv = _matmul(mem2d, w_mkv[l].astype(BF16), mem2d.shape[0], PROJ_COLS)
        xa = _xattn(q, kv, _tile(seq, LN_ROWS))
        x2, x2b, pq = _proj_ln([xa], w_mo[l].astype(BF16), x1, row1(ln2_gain[l]), row1(ln2_bias[l]),
                               w_pq[l].astype(BF16), _tile(seq, LN_ROWS), emit_bf16=True)
        tables = _peer_route(pq, peer_sub_keys[l].astype(BF16), _tile(seq, ROUTE_TOKENS), ROUTE_HEADS_PER_STEP)
        x2d = _peer_dense(x2b, x2, peer_u[l].T.astype(BF16), peer_v[l].astype(BF16), tables,
                          row1(ln3_gain[l]), row1(ln3_bias[l]), _tile(seq, PEER_TOKENS), PEER_KEYS_PER_STEP)
    return x2d.reshape(b, seq, d)
```
